```python
import math
import jax, jax.numpy as jnp
from jax import lax
import numpy as np

D_MODEL = 1024
BATCH = 16
SEQ = 256
DEPTH = 1
DEC_BATCH = 8
DEC_SEQ = 1024
PAST_LEN = 256

GRID_W = 64
N_HEADS = 8
QK_DIM = 64
V_DIM = 2 * QK_DIM
QK_COLS = N_HEADS * 2 * QK_DIM
V_COLS = N_HEADS * V_DIM
AXIS_DIM = QK_DIM // 2
ROPE_BASE = 10000.0
CONV_W = D_MODEL
CONV_K = 3
N_GROUPS = 4
EXPERTS_PER_GROUP = 4
N_EXPERTS = N_GROUPS * EXPERTS_PER_GROUP
TOP_K = 2
D_EXPERT = 512
Q_BLOCK = 128
IN_COLS = 3 * CONV_W + 2 * QK_COLS + V_COLS + 2 * D_MODEL
EPS = 1e-6

kernel_name = "hybrid_diffusion_conv_diffattn_hmoe_step"


def rms_norm(x, g):
    xf = x.astype(jnp.float32)
    xf = xf * lax.rsqrt(jnp.mean(xf * xf, axis=-1, keepdims=True) + EPS)
    return xf.astype(x.dtype) * g


def short_conv3(u, w):
    up = jnp.pad(u, ((0, 0), (1, 1), (0, 0)))
    return w[0] * up[:, :-2] + w[1] * up[:, 1:-1] + w[2] * up[:, 2:]


def axial_rope(x):
    n_tok = x.shape[1]
    rows = n_tok // GRID_W
    pos_row = jnp.repeat(jnp.arange(rows, dtype=jnp.float32), GRID_W)
    pos_col = jnp.tile(jnp.arange(GRID_W, dtype=jnp.float32), rows)
    inv_freq = 1.0 / (ROPE_BASE ** (jnp.arange(0, AXIS_DIM, 2, dtype=jnp.float32) / AXIS_DIM))
    half = AXIS_DIM // 2

    def rot(xa, pos):
        ang = pos[:, None] * inv_freq[None, :]
        cos = jnp.cos(ang)[None, :, None, None, :].astype(xa.dtype)
        sin = jnp.sin(ang)[None, :, None, None, :].astype(xa.dtype)
        x1, x2 = xa[..., :half], xa[..., half:]
        return jnp.concatenate([x1 * cos - x2 * sin, x1 * sin + x2 * cos], axis=-1)

    return jnp.concatenate([rot(x[..., :AXIS_DIM], pos_row), rot(x[..., AXIS_DIM:], pos_col)], axis=-1)


def diff_attention(q, k, v, lam):
    b, lq = q.shape[0], q.shape[1]
    nblk = lq // Q_BLOCK
    scale = 1.0 / math.sqrt(QK_DIM)
    qb = q.reshape(b, nblk, Q_BLOCK, N_HEADS, 2, QK_DIM).transpose(1, 0, 2, 3, 4, 5)

    def one_block(qblk):
        s = jnp.einsum('bqhcd,bkhcd->bchqk', qblk, k).astype(jnp.float32) * scale
        p = jax.nn.softmax(s, axis=-1)
        a = p[:, 0] - lam * p[:, 1]
        return jnp.einsum('bhqk,bkhv->bqhv', a.astype(v.dtype), v)

    o = lax.map(one_block, qb)
    return o.transpose(1, 0, 2, 3, 4).reshape(b, lq, N_HEADS, V_DIM)


def hier_moe(h, w_grp, b_grp, w_exp, b_exp, w1, w3, w2):
    shp = h.shape
    t = h.reshape(-1, shp[-1])
    g_logit = (t @ w_grp + b_grp).astype(jnp.float32)
    g_prob = jax.nn.softmax(g_logit, axis=-1)
    _, g_idx = lax.top_k(g_logit, 1)
    g_p = jnp.take_along_axis(g_prob, g_idx, axis=-1)
    e_logit = (t @ w_exp + b_exp).astype(jnp.float32).reshape(-1, N_GROUPS, EXPERTS_PER_GROUP)
    e_sel = jnp.take_along_axis(e_logit, g_idx[:, :, None], axis=1)[:, 0]
    top_v, top_i = lax.top_k(e_sel, TOP_K)
    w_top = jax.nn.softmax(top_v, axis=-1) * g_p
    expert_id = g_idx * EXPERTS_PER_GROUP + top_i
    combine = jnp.sum(jax.nn.one_hot(expert_id, N_EXPERTS, dtype=jnp.float32) * w_top[..., None], axis=1)
    combine = combine.astype(t.dtype)
    y = jnp.zeros_like(t)
    for e in range(N_EXPERTS):
        he = jax.nn.silu(t @ w1[e]) * (t @ w3[e])
        y = y + combine[:, e:e + 1] * (he @ w2[e])
    return y.reshape(shp)


def layer_forward(x, mod, is_latent, ctx_k, ctx_v, lam, lam_init,
                  norm1_g, w_in, conv_w, w_conv_out, q_norm_g, k_norm_g, subln_g,
                  w_attn_out, w_o, norm2_g, w_grp, b_grp, w_exp, b_exp, w1, w3, w2):
    b, l = x.shape[0], x.shape[1]
    shift1, scale1, gate1, shift2, scale2, gate2 = jnp.split(mod, 6, axis=-1)
    h = rms_norm(x, norm1_g) * (1 + scale1) + shift1
    z = h @ w_in
    sizes = [CONV_W, CONV_W, CONV_W, QK_COLS, QK_COLS, V_COLS, D_MODEL]
    idx = list(np.cumsum(sizes))
    cb, cc, cx, q, k, v, gc, ga = jnp.split(z, idx, axis=-1)
    y_conv = (cb * short_conv3(cc * cx, conv_w)) @ w_conv_out
    q = rms_norm(q.reshape(b, l, N_HEADS, 2, QK_DIM), q_norm_g)
    k = rms_norm(k.reshape(b, l, N_HEADS, 2, QK_DIM), k_norm_g)
    v = v.reshape(b, l, N_HEADS, V_DIM)
    if is_latent:
        q = axial_rope(q)
        k_lat = axial_rope(k)
        k_all = jnp.concatenate([k_lat, ctx_k], axis=1)
        v_all = jnp.concatenate([v, ctx_v], axis=1)
    else:
        k_all, v_all = k, v
    o = diff_attention(q, k_all, v_all, lam)
    o = rms_norm(o, subln_g) * (1.0 - lam_init)
    y_attn = o.reshape(b, l, V_COLS) @ w_attn_out
    mixed = jax.nn.sigmoid(gc) * y_conv + jax.nn.sigmoid(ga) * y_attn
    x = x + gate1 * (mixed @ w_o)
    h2 = rms_norm(x, norm2_g) * (1 + scale2) + shift2
    x = x + gate2 * hier_moe(h2, w_grp, b_grp, w_exp, b_exp, w1, w3, w2)
    return x, k, v


def setup_inputs(seed: int = 0) -> dict:
    key = jax.random.key(seed)
    ks = jax.random.split(key, 32)
    f32 = jnp.float32

    def nrm(k, shape, scale=1.0):
        return jax.random.normal(k, shape, f32) * scale

    d = D_MODEL
    return {
        "x_prompt": nrm(ks[0], (BATCH, SEQ, d)),
        "x_sample": nrm(ks[1], (DEC_BATCH, DEC_SEQ, d)),
        "cache_k": nrm(ks[2], (DEC_BATCH, DEPTH, PAST_LEN, N_HEADS, 2, QK_DIM)),
        "cache_v": nrm(ks[3], (DEC_BATCH, DEPTH, PAST_LEN, N_HEADS, V_DIM)),
        "c": nrm(ks[4], (DEC_BATCH, d)),
        "c_ctx": nrm(ks[5], (d,)),
        "w_ada": nrm(ks[6], (DEPTH, d, 6 * d), d ** -0.5),
        "b_ada": nrm(ks[7], (DEPTH, 6 * d), 0.02),
        "norm1_g": 1.0 + nrm(ks[8], (DEPTH, d), 0.02),
        "w_in": nrm(ks[9], (DEPTH, d, IN_COLS), d ** -0.5),
        "conv_w": nrm(ks[10], (DEPTH, CONV_K, CONV_W), CONV_K ** -0.5),
        "w_conv_out": nrm(ks[11], (DEPTH, CONV_W, d), CONV_W ** -0.5),
        "q_norm_g": 1.0 + nrm(ks[12], (DEPTH, QK_DIM), 0.02),
        "k_norm_g": 1.0 + nrm(ks[13], (DEPTH, QK_DIM), 0.02),
        "lambda_q1": nrm(ks[14], (DEPTH, QK_DIM), 0.1),
        "lambda_k1": nrm(ks[15], (DEPTH, QK_DIM), 0.1),
        "lambda_q2": nrm(ks[16], (DEPTH, QK_DIM), 0.1),
        "lambda_k2": nrm(ks[17], (DEPTH, QK_DIM), 0.1),
        "subln_g": 1.0 + nrm(ks[18], (DEPTH, V_DIM), 0.02),
        "w_attn_out": nrm(ks[19], (DEPTH, V_COLS, d), V_COLS ** -0.5),
        "w_o": nrm(ks[20], (DEPTH, d, d), d ** -0.5),
        "norm2_g": 1.0 + nrm(ks[21], (DEPTH, d), 0.02),
        "w_grp": nrm(ks[22], (DEPTH, d, N_GROUPS), d ** -0.5),
        "b_grp": nrm(ks[23], (DEPTH, N_GROUPS), 0.01),
        "w_exp": nrm(ks[24], (DEPTH, d, N_EXPERTS), d ** -0.5),
        "b_exp": nrm(ks[25], (DEPTH, N_EXPERTS), 0.01),
        "w1": nrm(ks[26], (DEPTH, N_EXPERTS, d, D_EXPERT), d ** -0.5),
        "w3": nrm(ks[27], (DEPTH, N_EXPERTS, d, D_EXPERT), d ** -0.5),
        "w2": nrm(ks[28], (DEPTH, N_EXPERTS, D_EXPERT, d), D_EXPERT ** -0.5),
    }


def reference(x_prompt, x_sample, cache_k, cache_v, c, c_ctx,
              w_ada, b_ada, norm1_g, w_in, conv_w, w_conv_out, q_norm_g, k_norm_g,
              lambda_q1, lambda_k1, lambda_q2, lambda_k2, subln_g, w_attn_out, w_o,
              norm2_g, w_grp, b_grp, w_exp, b_exp, w1, w3, w2):
    xp = x_prompt
    xs = x_sample
    new_k_list = []
    new_v_list = []
    for l in range(DEPTH):
        lam_init = 0.8 - 0.6 * math.exp(-0.3 * l)
        lam = (jnp.exp(jnp.sum((lambda_q1[l] * lambda_k1[l]).astype(jnp.float32)))
               - jnp.exp(jnp.sum((lambda_q2[l] * lambda_k2[l]).astype(jnp.float32)))
               + lam_init)
        mod_ctx = (jax.nn.silu(c_ctx) @ w_ada[l] + b_ada[l])[None, None, :]
        mod_lat = (jax.nn.silu(c) @ w_ada[l] + b_ada[l])[:, None, :]
        shared = (norm1_g[l], w_in[l], conv_w[l], w_conv_out[l], q_norm_g[l], k_norm_g[l],
                  subln_g[l], w_attn_out[l], w_o[l], norm2_g[l], w_grp[l], b_grp[l],
                  w_exp[l], b_exp[l], w1[l], w3[l], w2[l])
        xp, k_ctx, v_ctx = layer_forward(xp, mod_ctx, False, None, None, lam, lam_init, *shared)
        new_k_list.append(k_ctx)
        new_v_list.append(v_ctx)
        xs, _, _ = layer_forward(xs, mod_lat, True, cache_k[:, l], cache_v[:, l], lam, lam_init, *shared)
    new_k = jnp.stack(new_k_list, axis=1)
    new_v = jnp.stack(new_v_list, axis=1)
    return (xp, xs, new_k, new_v)
```

```python
import functools
import math

import jax
import jax.numpy as jnp
from jax import lax
from jax.experimental import pallas as pl
from jax.experimental.pallas import tpu as pltpu

D_MODEL = 1024
GRID_W = 64
N_HEADS = 8
QK_DIM = 64
V_DIM = 2 * QK_DIM
AXIS_DIM = QK_DIM // 2
ROPE_BASE = 10000.0
N_GROUPS = 4
EXPERTS_PER_GROUP = 4
N_EXPERTS = N_GROUPS * EXPERTS_PER_GROUP
D_EXPERT = 512
EPS = 1e-6

LANES = 128
MXU_DIM = 256
VMEM_LIMIT_BYTES = 56 * 1024 * 1024
ROUTER_LANE0 = N_GROUPS

F32 = jnp.float32
BF16 = jnp.bfloat16


def _dot(a, b):
    return jnp.dot(a, b, preferred_element_type=F32)


def _dot_nt(a, b):
    return lax.dot_general(a, b, (((1,), (1,)), ((), ())), preferred_element_type=F32)


def _split_bf16(x):
    hi = x.astype(BF16)
    lo = (x - hi.astype(F32)).astype(BF16)
    return hi, lo


def _sigmoid(x):
    return 1.0 / (1.0 + jnp.exp(-x))


def _rms(x):
    return x * lax.rsqrt(jnp.mean(x * x, axis=-1, keepdims=True) + EPS)


def _params(*sem):
    return pltpu.CompilerParams(dimension_semantics=sem, vmem_limit_bytes=VMEM_LIMIT_BYTES)


def _ada_kernel(c_ref, w_ref, b_ref, o_ref):
    c = c_ref[...]
    s = c * _sigmoid(c)
    s_hi, s_lo = _split_bf16(s)
    w_hi, w_lo = _split_bf16(w_ref[...])
    o_ref[...] = _dot(s_hi, w_hi) + _dot(s_lo, w_hi) + _dot(s_hi, w_lo) + b_ref[...]


def _ada(cvec, w_ada, b_ada):
    rows, d = cvec.shape
    n = w_ada.shape[1]
    tn = 1024
    return pl.pallas_call(
        _ada_kernel,
        grid=(n // tn,),
        in_specs=[pl.BlockSpec((rows, d), lambda j: (0, 0)),
                  pl.BlockSpec((d, tn), lambda j: (0, j)),
                  pl.BlockSpec((1, tn), lambda j: (0, j))],
        out_specs=pl.BlockSpec((rows, tn), lambda j: (0, j)),
        out_shape=jax.ShapeDtypeStruct((rows, n), F32),
        compiler_params=_params("arbitrary"),
        name="ada",
    )(cvec, w_ada, b_ada)


def _norm_mod(x, g, scale, shift):
    return (_rms(x) * g) * (1.0 + scale) + shift


def _conv_kernel(x_ref, mod_ref, g1_ref, w_ref, cw_ref, o_ref, *, seq_len):
    x = x_ref[...]
    tm = x.shape[0]
    hb = _norm_mod(x, g1_ref[...], mod_ref[0, 1:2, :], mod_ref[0, 0:1, :]).astype(BF16)
    pos = lax.broadcasted_iota(jnp.int32, (tm, 1), 0) % seq_len
    first = pos == 0
    last = pos == seq_len - 1
    for c in range(D_MODEL // MXU_DIM):
        sl = slice(c * MXU_DIM, (c + 1) * MXU_DIM)
        cb = _dot(hb, w_ref[0, :, sl])
        u = _dot(hb, w_ref[1, :, sl]) * _dot(hb, w_ref[2, :, sl])
        up = jnp.where(first, 0.0, pltpu.roll(u, 1, 0))
        dn = jnp.where(last, 0.0, pltpu.roll(u, tm - 1, 0))
        cw = cw_ref[:, sl]
        conv = cw[0:1] * up + cw[1:2] * u + cw[2:3] * dn
        o_ref[:, sl] = (cb * conv).astype(BF16)


def _chunk_norm(z, bd, g):
    sq = (z * z).astype(BF16)
    ss = jnp.concatenate(
        [_dot(sq[:, s * MXU_DIM:(s + 1) * MXU_DIM], bd) for s in range(z.shape[1] // MXU_DIM)], axis=1)
    return z * lax.rsqrt(ss * (1.0 / QK_DIM) + EPS) * g


def _rope(z, cos, sin, hi_half):
    n = z.shape[1]
    swap = jnp.where(hi_half, pltpu.roll(z, AXIS_DIM // 2, 1), pltpu.roll(z, n - AXIS_DIM // 2, 1))
    return z * cos + swap * sin


def _qkv_kernel(*refs, latent):
    if latent:
        x_ref, mod_ref, g1_ref, w_ref, bd_ref, gq_ref, gk_ref, cos_ref, sin_ref, q_ref, k_ref, v_ref = refs
    else:
        x_ref, mod_ref, g1_ref, w_ref, bd_ref, gq_ref, gk_ref, q_ref, k_ref, v_ref, kn_ref, vn_ref = refs
    hb = _norm_mod(x_ref[...], g1_ref[...], mod_ref[0, 1:2, :], mod_ref[0, 0:1, :]).astype(BF16)
    bd = bd_ref[...]
    q = _chunk_norm(_dot(hb, w_ref[0]), bd, gq_ref[...])
    k = _chunk_norm(_dot(hb, w_ref[1]), bd, gk_ref[...])
    v = _dot(hb, w_ref[2])
    if latent:
        reps = D_MODEL // LANES
        cos = jnp.concatenate([cos_ref[...]] * reps, axis=1)
        sin = jnp.concatenate([sin_ref[...]] * reps, axis=1)
        lane = lax.broadcasted_iota(jnp.int32, (1, D_MODEL), 1)
        hi_half = (lane & (AXIS_DIM // 2)) != 0
        q = _rope(q, cos, sin, hi_half)
        k_att = _rope(k, cos, sin, hi_half)
    else:
        k_att = k
        kn_ref[...] = k
        vn_ref[...] = v
    q_ref[...] = (q * (1.0 / math.sqrt(QK_DIM))).astype(BF16)
    k_ref[...] = k_att.astype(BF16)
    v_ref[...] = v.astype(BF16)


def _attn_kernel(*refs, latent, lam_init):
    if latent:
        lamv_ref, q_ref, k_ref, v_ref, ck_ref, cv_ref, sg_ref, o_ref = refs
    else:
        lamv_ref, q_ref, k_ref, v_ref, sg_ref, o_ref = refs
    lv = lamv_ref[...]
    lam = (jnp.exp(jnp.sum(lv[0:1] * lv[1:2], axis=1, keepdims=True))
           - jnp.exp(jnp.sum(lv[2:3] * lv[3:4], axis=1, keepdims=True)) + lam_init)
    q = q_ref[0]
    lane = lax.broadcasted_iota(jnp.int32, (1, V_DIM), 1)
    q1 = jnp.where(lane < QK_DIM, q, jnp.zeros_like(q))
    q2 = jnp.where(lane >= QK_DIM, q, jnp.zeros_like(q))
    k = k_ref[0]
    v = v_ref[0]
    keys = [k]
    vals = [v]
    if latent:
        keys.append(ck_ref[0].astype(BF16))
        vals.append(cv_ref[0].astype(BF16))

    def probs(qh):
        s = [_dot_nt(qh, kk) for kk in keys]
        m = s[0].max(axis=1, keepdims=True)
        for si in s[1:]:
            m = jnp.maximum(m, si.max(axis=1, keepdims=True))
        e = [jnp.exp(si - m) for si in s]
        l = e[0].sum(axis=1, keepdims=True)
        for ei in e[1:]:
            l = l + ei.sum(axis=1, keepdims=True)
        return e, 1.0 / l

    e1, r1 = probs(q1)
    e2, r2 = probs(q2)
    r2 = r2 * lam
    o = None
    for a1, a2, vv in zip(e1, e2, vals):
        part = _dot((a1 * r1 - a2 * r2).astype(BF16), vv)
        o = part if o is None else o + part
    o_ref[0] = (_rms(o) * sg_ref[...] * (1.0 - lam_init)).astype(BF16)


def _merge_kernel(x_ref, mod_ref, g1_ref, yc_ref, oa_ref, wg_ref, wco_ref, wao_ref, wo_ref, g2_ref,
                  wrh_ref, wrl_ref, br_ref, x1_ref, h2_ref, comb_ref):
    x = x_ref[...]
    hb = _norm_mod(x, g1_ref[...], mod_ref[0, 1:2, :], mod_ref[0, 0:1, :]).astype(BF16)
    y_conv = _dot(yc_ref[...], wco_ref[...])
    y_attn = _dot(oa_ref[...], wao_ref[...])
    mixed = _sigmoid(_dot(hb, wg_ref[0])) * y_conv + _sigmoid(_dot(hb, wg_ref[1])) * y_attn
    x1 = x + mod_ref[0, 2:3, :] * _dot(mixed.astype(BF16), wo_ref[...])
    x1_ref[...] = x1
    h2 = _norm_mod(x1, g2_ref[...], mod_ref[0, 4:5, :], mod_ref[0, 3:4, :])
    h2_ref[...] = h2.astype(BF16)

    h_hi, h_lo = _split_bf16(h2)
    logits = _dot(h_hi, wrh_ref[...]) + _dot(h_lo, wrh_ref[...]) + _dot(h_hi, wrl_ref[...]) + br_ref[...]
    tm = x.shape[0]
    lane = lax.broadcasted_iota(jnp.int32, (tm, LANES), 1).astype(F32)
    ninf = -jnp.inf
    big = float(LANES)

    def top(vals):
        vmax = vals.max(axis=1, keepdims=True)
        idx = jnp.where(vals == vmax, lane, big).min(axis=1, keepdims=True)
        return vmax, idx

    is_grp = lane < N_GROUPS
    gmax, gidx = top(jnp.where(is_grp, logits, ninf))
    g_p = 1.0 / jnp.where(is_grp, jnp.exp(logits - gmax), 0.0).sum(axis=1, keepdims=True)
    lo = ROUTER_LANE0 + EXPERTS_PER_GROUP * gidx
    el = jnp.where((lane >= lo) & (lane < lo + EXPERTS_PER_GROUP), logits, ninf)
    v1, i1 = top(el)
    v2, i2 = top(jnp.where(lane == i1, ninf, el))
    e2 = jnp.exp(v2 - v1)
    den = 1.0 / (1.0 + e2)
    comb_ref[...] = (jnp.where(lane == i1, den * g_p, 0.0) + jnp.where(lane == i2, e2 * den * g_p, 0.0))


def _moe_kernel(h2_ref, comb_ref, x1_ref, mod_ref, w1_ref, w3_ref, w2_ref, o_ref, acc_ref):
    e = pl.program_id(1)

    @pl.when(e == 0)
    def _():
        acc_ref[...] = jnp.zeros_like(acc_ref)

    t = h2_ref[...]
    a = _dot(t, w1_ref[0])
    he = (a * _sigmoid(a)) * _dot(t, w3_ref[0])
    y = _dot(he.astype(BF16), w2_ref[0])
    lane = lax.broadcasted_iota(jnp.int32, comb_ref.shape, 1)
    c = jnp.where(lane == e + ROUTER_LANE0, comb_ref[...], 0.0).sum(axis=1, keepdims=True)
    acc_ref[...] += c * y

    @pl.when(e == N_EXPERTS - 1)
    def _():
        o_ref[...] = x1_ref[...] + mod_ref[0, 5:6, :] * acc_ref[...]


def _layer(x3, mod, row0, latent, cache, wts, lam_init):
    b, l, d = x3.shape
    t = b * l
    x = x3.reshape(t, d)
    tm = 512
    nt = t // tm
    per_seq = l // tm if l >= tm else 1
    if latent:
        mod_row = lambda i: row0 + i // per_seq
    else:
        mod_row = lambda i: row0

    x_spec = pl.BlockSpec((tm, d), lambda i: (i, 0))
    mod_spec = pl.BlockSpec((1, 6, d), lambda i: (mod_row(i), 0, 0))
    row_spec = pl.BlockSpec((1, d), lambda i: (0, 0))
    act_spec = pl.BlockSpec((tm, d), lambda i: (i, 0))

    def full(a):
        nd = a.ndim
        return pl.BlockSpec(a.shape, lambda i: (0,) * nd)

    tmc = max(tm, l)
    assert tmc % l == 0 and t % tmc == 0
    seq_per_tile = tmc // l
    xc_spec = pl.BlockSpec((tmc, d), lambda i: (i, 0))
    modc_spec = pl.BlockSpec((1, 6, d), lambda i: (row0 + (i * seq_per_tile if latent else 0), 0, 0))
    assert not latent or seq_per_tile == 1
    yc = pl.pallas_call(
        functools.partial(_conv_kernel, seq_len=l),
        grid=(t // tmc,),
        in_specs=[xc_spec, modc_spec, row_spec, full(wts["w_conv_in"]), full(wts["conv_w"])],
        out_specs=xc_spec,
        out_shape=jax.ShapeDtypeStruct((t, d), BF16),
        compiler_params=_params("parallel"),
        name="conv_in",
    )(x, mod, wts["norm1_g"], wts["w_conv_in"], wts["conv_w"])

    qkv_in = [x, mod, wts["norm1_g"], wts["w_qkv"], wts["bd"], wts["gq"], wts["gk"]]
    qkv_specs = [x_spec, mod_spec, row_spec, full(wts["w_qkv"]), full(wts["bd"]), row_spec, row_spec]
    out_shapes = [jax.ShapeDtypeStruct((t, d), BF16)] * 3
    out_specs = [act_spec] * 3
    if latent:
        qkv_in += [wts["cos"], wts["sin"]]
        tab_spec = pl.BlockSpec((tm, LANES), lambda i: (i % per_seq, 0))
        qkv_specs += [tab_spec, tab_spec]
    else:
        out_shapes += [jax.ShapeDtypeStruct((t, d), F32)] * 2
        out_specs += [act_spec] * 2
    qkv_out = pl.pallas_call(
        functools.partial(_qkv_kernel, latent=latent),
        grid=(nt,),
        in_specs=qkv_specs,
        out_specs=out_specs,
        out_shape=out_shapes,
        compiler_params=_params("parallel"),
        name="qkv",
    )(*qkv_in)
    q, k, v = qkv_out[:3]

    tq = 256
    q3, k3, v3 = (a.reshape(b, l, d) for a in (q, k, v))
    q_spec = pl.BlockSpec((1, tq, V_DIM), lambda bi, h, qi: (bi, qi, h))
    kv_spec = pl.BlockSpec((1, l, V_DIM), lambda bi, h, qi: (bi, 0, h))
    sg_spec = pl.BlockSpec((1, V_DIM), lambda bi, h, qi: (0, 0))
    lam_spec = pl.BlockSpec(wts["lamv"].shape, lambda bi, h, qi: (0, 0))
    attn_in = [wts["lamv"], q3, k3, v3]
    attn_specs = [lam_spec, q_spec, kv_spec, kv_spec]
    if latent:
        ck, cv = cache
        c_spec = pl.BlockSpec((1, ck.shape[1], V_DIM), lambda bi, h, qi: (bi, 0, h))
        attn_in += [ck, cv]
        attn_specs += [c_spec, c_spec]
    attn_in.append(wts["subln_g"])
    attn_specs.append(sg_spec)
    oa = pl.pallas_call(
        functools.partial(_attn_kernel, latent=latent, lam_init=lam_init),
        grid=(b, N_HEADS, l // tq),
        in_specs=attn_specs,
        out_specs=q_spec,
        out_shape=jax.ShapeDtypeStruct((b, l, d), BF16),
        compiler_params=_params("parallel", "parallel", "arbitrary"),
        name="attn",
    )(*attn_in).reshape(t, d)

    lane_spec = pl.BlockSpec((tm, LANES), lambda i: (i, 0))
    x1, h2, comb = pl.pallas_call(
        _merge_kernel,
        grid=(nt,),
        in_specs=[x_spec, mod_spec, row_spec, act_spec, act_spec, full(wts["w_gate"]), full(wts["w_conv_out"]),
                  full(wts["w_attn_out"]), full(wts["w_o"]), row_spec, full(wts["wr_hi"]), full(wts["wr_lo"]),
                  full(wts["b_router"])],
        out_specs=[x_spec, act_spec, lane_spec],
        out_shape=[jax.ShapeDtypeStruct((t, d), F32), jax.ShapeDtypeStruct((t, d), BF16),
                   jax.ShapeDtypeStruct((t, LANES), F32)],
        compiler_params=_params("parallel"),
        name="merge",
    )(x, mod, wts["norm1_g"], yc, oa, wts["w_gate"], wts["w_conv_out"], wts["w_attn_out"], wts["w_o"],
      wts["norm2_g"], wts["wr_hi"], wts["wr_lo"], wts["b_router"])

    y = pl.pallas_call(
        _moe_kernel,
        grid=(nt, N_EXPERTS),
        in_specs=[pl.BlockSpec((tm, d), lambda i, e: (i, 0)),
                  pl.BlockSpec((tm, LANES), lambda i, e: (i, 0)),
                  pl.BlockSpec((tm, d), lambda i, e: (i, 0)),
                  pl.BlockSpec((1, 6, d), lambda i, e: (mod_row(i), 0, 0)),
                  pl.BlockSpec((1, d, D_EXPERT), lambda i, e: (e, 0, 0)),
                  pl.BlockSpec((1, d, D_EXPERT), lambda i, e: (e, 0, 0)),
                  pl.BlockSpec((1, D_EXPERT, d), lambda i, e: (e, 0, 0))],
        out_specs=pl.BlockSpec((tm, d), lambda i, e: (i, 0)),
        out_shape=jax.ShapeDtypeStruct((t, d), F32),
        scratch_shapes=[pltpu.VMEM((tm, d), F32)],
        compiler_params=_params("parallel", "arbitrary"),
        name="moe",
    )(h2, comb, x1, mod, wts["w1"], wts["w3"], wts["w2"])

    outs = [y.reshape(b, l, d)]
    if not latent:
        outs += [qkv_out[3], qkv_out[4]]
    return outs


def _rope_tables(n_tok):
    rows = n_tok // GRID_W
    pos_row = jnp.repeat(jnp.arange(rows, dtype=F32), GRID_W)
    pos_col = jnp.tile(jnp.arange(GRID_W, dtype=F32), rows)
    inv_freq = 1.0 / (ROPE_BASE ** (jnp.arange(0, AXIS_DIM, 2, dtype=F32) / AXIS_DIM))
    ang = jnp.concatenate([pos_row[:, None] * inv_freq[None, :]] * 2
                          + [pos_col[:, None] * inv_freq[None, :]] * 2, axis=1)
    sign = jnp.tile(jnp.concatenate([-jnp.ones((AXIS_DIM // 2,), F32), jnp.ones((AXIS_DIM // 2,), F32)]), 2)
    cos = jnp.cos(ang)
    sin = jnp.sin(ang) * sign[None, :]
    return jnp.tile(cos, (1, LANES // QK_DIM)), jnp.tile(sin, (1, LANES // QK_DIM))


def kernel(x_prompt, x_sample, cache_k, cache_v, c, c_ctx, w_ada, b_ada, norm1_g, w_in, conv_w, w_conv_out,
           q_norm_g, k_norm_g, lambda_q1, lambda_k1, lambda_q2, lambda_k2, subln_g, w_attn_out, w_o, norm2_g,
           w_grp, b_grp, w_exp, b_exp, w1, w3, w2):
    depth = w_in.shape[0]
    assert depth == 1
    d = D_MODEL
    dec_b, dec_l = x_sample.shape[0], x_sample.shape[1]
    cos, sin = _rope_tables(dec_l)
    bd = (jnp.arange(MXU_DIM)[:, None] // QK_DIM == jnp.arange(MXU_DIM)[None, :] // QK_DIM).astype(BF16)
    xp, xs = x_prompt, x_sample
    new_k, new_v = [], []
    for l in range(depth):
        lam_init = 0.8 - 0.6 * math.exp(-0.3 * l)
        mod_rows = 16
        cvec = jnp.concatenate([c_ctx[None, :], c, jnp.zeros((mod_rows - 1 - dec_b, d), F32)], axis=0)
        mod = _ada(cvec, w_ada[l], b_ada[l][None, :]).reshape(mod_rows, 6, d)

        wi = w_in[l].astype(BF16)
        w_router = jnp.concatenate([w_grp[l], w_exp[l]], axis=1)
        w_router = jnp.pad(w_router, ((0, 0), (0, LANES - w_router.shape[1])))
        wr_hi = w_router.astype(BF16)
        wr_lo = (w_router - wr_hi.astype(F32)).astype(BF16)
        b_router = jnp.pad(jnp.concatenate([b_grp[l], b_exp[l]]), (0, LANES - N_GROUPS - N_EXPERTS))[None, :]
        wts = dict(
            norm1_g=norm1_g[l][None, :], norm2_g=norm2_g[l][None, :],
            w_conv_in=wi[:, :3 * d].reshape(d, 3, d).transpose(1, 0, 2),
            w_qkv=wi[:, 3 * d:6 * d].reshape(d, 3, d).transpose(1, 0, 2),
            w_gate=wi[:, 6 * d:].reshape(d, 2, d).transpose(1, 0, 2),
            conv_w=conv_w[l], w_conv_out=w_conv_out[l].astype(BF16),
            gq=jnp.tile(q_norm_g[l], d // QK_DIM)[None, :], gk=jnp.tile(k_norm_g[l], d // QK_DIM)[None, :],
            bd=bd, cos=cos, sin=sin,
            lamv=jnp.stack([lambda_q1[l], lambda_k1[l], lambda_q2[l], lambda_k2[l]]),
            subln_g=subln_g[l][None, :], w_attn_out=w_attn_out[l].astype(BF16), w_o=w_o[l].astype(BF16),
            wr_hi=wr_hi, wr_lo=wr_lo, b_router=b_router,
            w1=w1[l].astype(BF16), w3=w3[l].astype(BF16), w2=w2[l].astype(BF16),
        )
        xp, k_ctx, v_ctx = _layer(xp, mod, 0, False, None, wts, lam_init)
        new_k.append(k_ctx.reshape(x_prompt.shape[0], x_prompt.shape[1], N_HEADS, 2, QK_DIM))
        new_v.append(v_ctx.reshape(x_prompt.shape[0], x_prompt.shape[1], N_HEADS, V_DIM))
        past = cache_k.shape[2]
        cache = (cache_k[:, l].reshape(dec_b, past, d), cache_v[:, l].reshape(dec_b, past, d))
        (xs,) = _layer(xs, mod, 1, True, cache, wts, lam_init)
    return (xp, xs, jnp.stack(new_k, axis=1), jnp.stack(new_v, axis=1))
```

```python
import functools
import math

import jax
import jax.numpy as jnp
from jax import lax
from jax.experimental import pallas as pl
from jax.experimental.pallas import tpu as pltpu

D_MODEL = 1024
GRID_W = 64
N_HEADS = 8
QK_DIM = 64
V_DIM = 2 * QK_DIM
AXIS_DIM = QK_DIM // 2
ROPE_BASE = 10000.0
N_GROUPS = 4
EXPERTS_PER_GROUP = 4
N_EXPERTS = N_GROUPS * EXPERTS_PER_GROUP
D_EXPERT = 512
EPS = 1e-6

LANES = 128
MXU_DIM = 256
VMEM_LIMIT_BYTES = 56 * 1024 * 1024
ROUTER_LANE0 = N_GROUPS

F32 = jnp.float32
BF16 = jnp.bfloat16


def _dot(a, b):
    return jnp.dot(a, b, preferred_element_type=F32)


def _dot_nt(a, b):
    return lax.dot_general(a, b, (((1,), (1,)), ((), ())), preferred_element_type=F32)


def _split_bf16(x):
    hi = x.astype(BF16)
    lo = (x - hi.astype(F32)).astype(BF16)
    return hi, lo


def _sigmoid(x):
    return 1.0 / (1.0 + jnp.exp(-x))


def _rms(x):
    return x * lax.rsqrt(jnp.mean(x * x, axis=-1, keepdims=True) + EPS)


def _params(*sem):
    return pltpu.CompilerParams(dimension_semantics=sem, vmem_limit_bytes=VMEM_LIMIT_BYTES)


def _ada_kernel(c_ref, w_ref, b_ref, o_ref):
    c = c_ref[...]
    s = c * _sigmoid(c)
    s_hi, s_lo = _split_bf16(s)
    w_hi, w_lo = _split_bf16(w_ref[...])
    o_ref[...] = _dot(s_hi, w_hi) + _dot(s_lo, w_hi) + _dot(s_hi, w_lo) + b_ref[...]


def _ada(cvec, w_ada, b_ada):
    rows, d = cvec.shape
    n = w_ada.shape[1]
    tn = 1024
    return pl.pallas_call(
        _ada_kernel,
        grid=(n // tn,),
        in_specs=[pl.BlockSpec((rows, d), lambda j: (0, 0)),
                  pl.BlockSpec((d, tn), lambda j: (0, j)),
                  pl.BlockSpec((1, tn), lambda j: (0, j))],
        out_specs=pl.BlockSpec((rows, tn), lambda j: (0, j)),
        out_shape=jax.ShapeDtypeStruct((rows, n), F32),
        compiler_params=_params("arbitrary"),
        name="ada",
    )(cvec, w_ada, b_ada)


def _norm_mod(x, g, scale, shift):
    return (_rms(x) * g) * (1.0 + scale) + shift


def _conv_kernel(x_ref, mod_ref, g1_ref, w_ref, cw_ref, o_ref, *, seq_len):
    x = x_ref[...]
    tm = x.shape[0]
    hb = _norm_mod(x, g1_ref[...], mod_ref[0, 1:2, :], mod_ref[0, 0:1, :]).astype(BF16)
    pos = lax.broadcasted_iota(jnp.int32, (tm, 1), 0) % seq_len
    first = pos == 0
    last = pos == seq_len - 1
    for c in range(D_MODEL // MXU_DIM):
        sl = slice(c * MXU_DIM, (c + 1) * MXU_DIM)
        cb = _dot(hb, w_ref[0, :, sl])
        u = _dot(hb, w_ref[1, :, sl]) * _dot(hb, w_ref[2, :, sl])
        up = jnp.where(first, 0.0, pltpu.roll(u, 1, 0))
        dn = jnp.where(last, 0.0, pltpu.roll(u, tm - 1, 0))
        cw = cw_ref[:, sl]
        conv = cw[0:1] * up + cw[1:2] * u + cw[2:3] * dn
        o_ref[:, sl] = (cb * conv).astype(BF16)


def _chunk_norm(z, bd, g):
    sq = (z * z).astype(BF16)
    ss = jnp.concatenate(
        [_dot(sq[:, s * MXU_DIM:(s + 1) * MXU_DIM], bd) for s in range(z.shape[1] // MXU_DIM)], axis=1)
    return z * lax.rsqrt(ss * (1.0 / QK_DIM) + EPS) * g


def _rope(z, cos, sin, hi_half):
    n = z.shape[1]
    swap = jnp.where(hi_half, pltpu.roll(z, AXIS_DIM // 2, 1), pltpu.roll(z, n - AXIS_DIM // 2, 1))
    return z * cos + swap * sin


def _qkv_kernel(*refs, latent):
    if latent:
        x_ref, mod_ref, g1_ref, w_ref, bd_ref, gq_ref, gk_ref, cos_ref, sin_ref, q_ref, k_ref, v_ref = refs
    else:
        x_ref, mod_ref, g1_ref, w_ref, bd_ref, gq_ref, gk_ref, q_ref, k_ref, v_ref, kn_ref, vn_ref = refs
    hb = _norm_mod(x_ref[...], g1_ref[...], mod_ref[0, 1:2, :], mod_ref[0, 0:1, :]).astype(BF16)
    bd = bd_ref[...]
    q = _chunk_norm(_dot(hb, w_ref[0]), bd, gq_ref[...])
    k = _chunk_norm(_dot(hb, w_ref[1]), bd, gk_ref[...])
    v = _dot(hb, w_ref[2])
    if latent:
        reps = D_MODEL // LANES
        cos = jnp.concatenate([cos_ref[...]] * reps, axis=1)
        sin = jnp.concatenate([sin_ref[...]] * reps, axis=1)
        lane = lax.broadcasted_iota(jnp.int32, (1, D_MODEL), 1)
        hi_half = (lane & (AXIS_DIM // 2)) != 0
        q = _rope(q, cos, sin, hi_half)
        k_att = _rope(k, cos, sin, hi_half)
    else:
        k_att = k
        kn_ref[...] = k
        vn_ref[...] = v
    q_ref[...] = (q * (1.0 / math.sqrt(QK_DIM))).astype(BF16)
    k_ref[...] = k_att.astype(BF16)
    v_ref[...] = v.astype(BF16)


def _attn_kernel(*refs, latent, lam_init):
    if latent:
        lamv_ref, q_ref, k_ref, v_ref, ck_ref, cv_ref, sg_ref, o_ref = refs
    else:
        lamv_ref, q_ref, k_ref, v_ref, sg_ref, o_ref = refs
    lv = lamv_ref[...]
    lam = (jnp.exp(jnp.sum(lv[0:1] * lv[1:2], axis=1, keepdims=True))
           - jnp.exp(jnp.sum(lv[2:3] * lv[3:4], axis=1, keepdims=True)) + lam_init)
    q = q_ref[0]
    lane = lax.broadcasted_iota(jnp.int32, (1, V_DIM), 1)
    q1 = jnp.where(lane < QK_DIM, q, jnp.zeros_like(q))
    q2 = jnp.where(lane >= QK_DIM, q, jnp.zeros_like(q))
    k = k_ref[0]
    v = v_ref[0]
    keys = [k]
    vals = [v]
    if latent:
        keys.append(ck_ref[0].astype(BF16))
        vals.append(cv_ref[0].astype(BF16))

    def probs(qh):
        s = [_dot_nt(qh, kk) for kk in keys]
        m = s[0].max(axis=1, keepdims=True)
        for si in s[1:]:
            m = jnp.maximum(m, si.max(axis=1, keepdims=True))
        e = [jnp.exp(si - m) for si in s]
        l = e[0].sum(axis=1, keepdims=True)
        for ei in e[1:]:
            l = l + ei.sum(axis=1, keepdims=True)
        return e, 1.0 / l

    e1, r1 = probs(q1)
    e2, r2 = probs(q2)
    r2 = r2 * lam
    o = None
    for a1, a2, vv in zip(e1, e2, vals):
        part = _dot((a1 * r1 - a2 * r2).astype(BF16), vv)
        o = part if o is None else o + part
    o_ref[0] = (_rms(o) * sg_ref[...] * (1.0 - lam_init)).astype(BF16)


def _merge_kernel(x_ref, mod_ref, g1_ref, yc_ref, oa_ref, wg_ref, wco_ref, wao_ref, wo_ref, g2_ref,
                  wrh_ref, wrl_ref, br_ref, x1_ref, h2_ref, route_ref):
    x = x_ref[...]
    hb = _norm_mod(x, g1_ref[...], mod_ref[0, 1:2, :], mod_ref[0, 0:1, :]).astype(BF16)
    y_conv = _dot(yc_ref[...], wco_ref[...])
    y_attn = _dot(oa_ref[...], wao_ref[...])
    mixed = _sigmoid(_dot(hb, wg_ref[0])) * y_conv + _sigmoid(_dot(hb, wg_ref[1])) * y_attn
    x1 = x + mod_ref[0, 2:3, :] * _dot(mixed.astype(BF16), wo_ref[...])
    x1_ref[...] = x1
    h2 = _norm_mod(x1, g2_ref[...], mod_ref[0, 4:5, :], mod_ref[0, 3:4, :])
    h2_ref[...] = h2.astype(BF16)

    h_hi, h_lo = _split_bf16(h2)
    logits = _dot(h_hi, wrh_ref[...]) + _dot(h_lo, wrh_ref[...]) + _dot(h_hi, wrl_ref[...]) + br_ref[...]
    tm = x.shape[0]
    lane = lax.broadcasted_iota(jnp.int32, (tm, LANES), 1).astype(F32)
    ninf = -jnp.inf
    big = float(LANES)

    def top(vals):
        vmax = vals.max(axis=1, keepdims=True)
        idx = jnp.where(vals == vmax, lane, big).min(axis=1, keepdims=True)
        return vmax, idx

    is_grp = lane < N_GROUPS
    gmax, gidx = top(jnp.where(is_grp, logits, ninf))
    g_p = 1.0 / jnp.where(is_grp, jnp.exp(logits - gmax), 0.0).sum(axis=1, keepdims=True)
    lo = ROUTER_LANE0 + EXPERTS_PER_GROUP * gidx
    el = jnp.where((lane >= lo) & (lane < lo + EXPERTS_PER_GROUP), logits, ninf)
    v1, i1 = top(el)
    v2, i2 = top(jnp.where(lane == i1, ninf, el))
    e2 = jnp.exp(v2 - v1)
    den = 1.0 / (1.0 + e2)
    route_ref[...] = (jnp.where(lane == 0.0, i1 - ROUTER_LANE0, 0.0) + jnp.where(lane == 1.0, i2 - ROUTER_LANE0, 0.0)
                      + jnp.where(lane == 2.0, den * g_p, 0.0) + jnp.where(lane == 3.0, e2 * den * g_p, 0.0))


def _expert_kernel(te_ref, nu_ref, x_ref, w1_ref, w3_ref, w2_ref, y_ref, w1b, w3b, w2b):
    j = pl.program_id(0)

    @pl.when((j == 0) | (te_ref[j] != te_ref[jnp.maximum(j - 1, 0)]))
    def _():
        w1b[...] = w1_ref[0].astype(BF16)
        w3b[...] = w3_ref[0].astype(BF16)
        w2b[...] = w2_ref[0].astype(BF16)

    @pl.when(j < nu_ref[0])
    def _():
        t = x_ref[...]
        a = _dot(t, w1b[...])
        he = (a * _sigmoid(a)) * _dot(t, w3b[...])
        y_ref[...] = _dot(he.astype(BF16), w2b[...])


def _combine_kernel(x1p_ref, x1s_ref, route_ref, y0_ref, y1_ref, mod_ref, op_ref, os_ref, *, n_p):
    i = pl.program_id(0)
    r = route_ref[...]
    moe = mod_ref[0, 5:6, :] * (r[:, 2:3] * y0_ref[...] + r[:, 3:4] * y1_ref[...])

    @pl.when(i < n_p)
    def _():
        op_ref[...] = x1p_ref[...] + moe

    @pl.when(i >= n_p)
    def _():
        os_ref[...] = x1s_ref[...] + moe


def _layer(x3, mod, row0, latent, cache, wts, lam_init):
    b, l, d = x3.shape
    t = b * l
    x = x3.reshape(t, d)
    tm = 512
    nt = t // tm
    per_seq = l // tm if l >= tm else 1
    if latent:
        mod_row = lambda i: row0 + i // per_seq
    else:
        mod_row = lambda i: row0

    x_spec = pl.BlockSpec((tm, d), lambda i: (i, 0))
    mod_spec = pl.BlockSpec((1, 6, d), lambda i: (mod_row(i), 0, 0))
    row_spec = pl.BlockSpec((1, d), lambda i: (0, 0))
    act_spec = pl.BlockSpec((tm, d), lambda i: (i, 0))

    def full(a):
        nd = a.ndim
        return pl.BlockSpec(a.shape, lambda i: (0,) * nd)

    tmc = max(tm, l)
    assert tmc % l == 0 and t % tmc == 0
    seq_per_tile = tmc // l
    xc_spec = pl.BlockSpec((tmc, d), lambda i: (i, 0))
    modc_spec = pl.BlockSpec((1, 6, d), lambda i: (row0 + (i * seq_per_tile if latent else 0), 0, 0))
    assert not latent or seq_per_tile == 1
    yc = pl.pallas_call(
        functools.partial(_conv_kernel, seq_len=l),
        grid=(t // tmc,),
        in_specs=[xc_spec, modc_spec, row_spec, full(wts["w_conv_in"]), full(wts["conv_w"])],
        out_specs=xc_spec,
        out_shape=jax.ShapeDtypeStruct((t, d), BF16),
        compiler_params=_params("parallel"),
        name="conv_in",
    )(x, mod, wts["norm1_g"], wts["w_conv_in"], wts["conv_w"])

    qkv_in = [x, mod, wts["norm1_g"], wts["w_qkv"], wts["bd"], wts["gq"], wts["gk"]]
    qkv_specs = [x_spec, mod_spec, row_spec, full(wts["w_qkv"]), full(wts["bd"]), row_spec, row_spec]
    out_shapes = [jax.ShapeDtypeStruct((t, d), BF16)] * 3
    out_specs = [act_spec] * 3
    if latent:
        qkv_in += [wts["cos"], wts["sin"]]
        tab_spec = pl.BlockSpec((tm, LANES), lambda i: (i % per_seq, 0))
        qkv_specs += [tab_spec, tab_spec]
    else:
        out_shapes += [jax.ShapeDtypeStruct((t, d), F32)] * 2
        out_specs += [act_spec] * 2
    qkv_out = pl.pallas_call(
        functools.partial(_qkv_kernel, latent=latent),
        grid=(nt,),
        in_specs=qkv_specs,
        out_specs=out_specs,
        out_shape=out_shapes,
        compiler_params=_params("parallel"),
        name="qkv",
    )(*qkv_in)
    q, k, v = qkv_out[:3]

    tq = 256
    q3, k3, v3 = (a.reshape(b, l, d) for a in (q, k, v))
    q_spec = pl.BlockSpec((1, tq, V_DIM), lambda bi, h, qi: (bi, qi, h))
    kv_spec = pl.BlockSpec((1, l, V_DIM), lambda bi, h, qi: (bi, 0, h))
    sg_spec = pl.BlockSpec((1, V_DIM), lambda bi, h, qi: (0, 0))
    lam_spec = pl.BlockSpec(wts["lamv"].shape, lambda bi, h, qi: (0, 0))
    attn_in = [wts["lamv"], q3, k3, v3]
    attn_specs = [lam_spec, q_spec, kv_spec, kv_spec]
    if latent:
        ck, cv = cache
        c_spec = pl.BlockSpec((1, ck.shape[1], V_DIM), lambda bi, h, qi: (bi, 0, h))
        attn_in += [ck, cv]
        attn_specs += [c_spec, c_spec]
    attn_in.append(wts["subln_g"])
    attn_specs.append(sg_spec)
    oa = pl.pallas_call(
        functools.partial(_attn_kernel, latent=latent, lam_init=lam_init),
        grid=(b, N_HEADS, l // tq),
        in_specs=attn_specs,
        out_specs=q_spec,
        out_shape=jax.ShapeDtypeStruct((b, l, d), BF16),
        compiler_params=_params("parallel", "parallel", "arbitrary"),
        name="attn",
    )(*attn_in).reshape(t, d)

    lane_spec = pl.BlockSpec((tm, LANES), lambda i: (i, 0))
    x1, h2, route = pl.pallas_call(
        _merge_kernel,
        grid=(nt,),
        in_specs=[x_spec, mod_spec, row_spec, act_spec, act_spec, full(wts["w_gate"]), full(wts["w_conv_out"]),
                  full(wts["w_attn_out"]), full(wts["w_o"]), row_spec, full(wts["wr_hi"]), full(wts["wr_lo"]),
                  full(wts["b_router"])],
        out_specs=[x_spec, act_spec, lane_spec],
        out_shape=[jax.ShapeDtypeStruct((t, d), F32), jax.ShapeDtypeStruct((t, d), BF16),
                   jax.ShapeDtypeStruct((t, LANES), F32)],
        compiler_params=_params("parallel"),
        name="merge",
    )(x, mod, wts["norm1_g"], yc, oa, wts["w_gate"], wts["w_conv_out"], wts["w_attn_out"], wts["w_o"],
      wts["norm2_g"], wts["wr_hi"], wts["wr_lo"], wts["b_router"])

    outs = [x1, h2, route]
    if not latent:
        outs += [qkv_out[3], qkv_out[4]]
    return outs


MOE_ROWS = 256


def _routing_tables(route, n_tiles):
    t = route.shape[0]
    eid = route[:, :2].astype(jnp.int32).reshape(-1)
    onehot = (eid[:, None] == jnp.arange(N_EXPERTS, dtype=jnp.int32)[None, :]).astype(jnp.int32)
    csum = jnp.cumsum(onehot, axis=0)
    rank = jnp.sum(csum * onehot, axis=1) - 1
    counts = csum[-1]
    tiles_e = (counts + MOE_ROWS - 1) // MOE_ROWS
    tile_end = jnp.cumsum(tiles_e)
    row_start = (tile_end - tiles_e) * MOE_ROWS
    pos = jnp.sum(onehot * row_start[None, :], axis=1) + rank
    n_used = tile_end[-1]
    tile_ids = jnp.minimum(jnp.arange(n_tiles, dtype=jnp.int32), n_used - 1)
    tile_expert = jnp.sum((tile_ids[:, None] >= tile_end[None, :]).astype(jnp.int32), axis=1)
    src_tok = jnp.zeros((n_tiles * MOE_ROWS,), jnp.int32).at[pos].set(jnp.arange(2 * t, dtype=jnp.int32) // 2)
    return pos.reshape(t, 2), src_tok, tile_expert.astype(jnp.int32), n_used.astype(jnp.int32).reshape(1)


def _routed_moe(x1p, x1s, h2, route, mod, w1, w3, w2, per_seq_tiles, tm):
    t, d = h2.shape
    n_tiles = 2 * t // MOE_ROWS + N_EXPERTS
    pos, src_tok, tile_expert, n_used = _routing_tables(route, n_tiles)
    x_sorted = jnp.take(h2, src_tok, axis=0)

    def row_blk(j, te, nu):
        return (jnp.minimum(j, nu[0] - 1), 0)

    y_sorted = pl.pallas_call(
        _expert_kernel,
        grid_spec=pltpu.PrefetchScalarGridSpec(
            num_scalar_prefetch=2,
            grid=(n_tiles,),
            in_specs=[pl.BlockSpec((MOE_ROWS, d), row_blk),
                      pl.BlockSpec((1, d, D_EXPERT), lambda j, te, nu: (te[j], 0, 0)),
                      pl.BlockSpec((1, d, D_EXPERT), lambda j, te, nu: (te[j], 0, 0)),
                      pl.BlockSpec((1, D_EXPERT, d), lambda j, te, nu: (te[j], 0, 0))],
            out_specs=pl.BlockSpec((MOE_ROWS, d), row_blk),
            scratch_shapes=[pltpu.VMEM((d, D_EXPERT), BF16), pltpu.VMEM((d, D_EXPERT), BF16),
                            pltpu.VMEM((D_EXPERT, d), BF16)]),
        out_shape=jax.ShapeDtypeStruct((n_tiles * MOE_ROWS, d), F32),
        compiler_params=_params("arbitrary"),
        name="experts",
    )(tile_expert, n_used, x_sorted, w1, w3, w2)

    y0 = jnp.take(y_sorted, pos[:, 0], axis=0)
    y1 = jnp.take(y_sorted, pos[:, 1], axis=0)
    tp, ts = x1p.shape[0], x1s.shape[0]
    n_p = tp // tm
    blk = lambda i: (i, 0)
    return pl.pallas_call(
        functools.partial(_combine_kernel, n_p=n_p),
        grid=(t // tm,),
        in_specs=[pl.BlockSpec((tm, d), lambda i: (jnp.minimum(i, n_p - 1), 0)),
                  pl.BlockSpec((tm, d), lambda i: (jnp.maximum(i - n_p, 0), 0)),
                  pl.BlockSpec((tm, LANES), blk), pl.BlockSpec((tm, d), blk), pl.BlockSpec((tm, d), blk),
                  pl.BlockSpec((1, 6, d), lambda i: (jnp.where(i < n_p, 0, 1 + (i - n_p) // per_seq_tiles), 0, 0))],
        out_specs=[pl.BlockSpec((tm, d), lambda i: (jnp.minimum(i, n_p - 1), 0)),
                   pl.BlockSpec((tm, d), lambda i: (jnp.maximum(i - n_p, 0), 0))],
        out_shape=[jax.ShapeDtypeStruct((tp, d), F32), jax.ShapeDtypeStruct((ts, d), F32)],
        compiler_params=_params("arbitrary"),
        name="combine",
    )(x1p, x1s, route, y0, y1, mod)


def _rope_tables(n_tok):
    rows = n_tok // GRID_W
    pos_row = jnp.repeat(jnp.arange(rows, dtype=F32), GRID_W)
    pos_col = jnp.tile(jnp.arange(GRID_W, dtype=F32), rows)
    inv_freq = 1.0 / (ROPE_BASE ** (jnp.arange(0, AXIS_DIM, 2, dtype=F32) / AXIS_DIM))
    ang = jnp.concatenate([pos_row[:, None] * inv_freq[None, :]] * 2
                          + [pos_col[:, None] * inv_freq[None, :]] * 2, axis=1)
    sign = jnp.tile(jnp.concatenate([-jnp.ones((AXIS_DIM // 2,), F32), jnp.ones((AXIS_DIM // 2,), F32)]), 2)
    cos = jnp.cos(ang)
    sin = jnp.sin(ang) * sign[None, :]
    return jnp.tile(cos, (1, LANES // QK_DIM)), jnp.tile(sin, (1, LANES // QK_DIM))


def kernel(x_prompt, x_sample, cache_k, cache_v, c, c_ctx, w_ada, b_ada, norm1_g, w_in, conv_w, w_conv_out,
           q_norm_g, k_norm_g, lambda_q1, lambda_k1, lambda_q2, lambda_k2, subln_g, w_attn_out, w_o, norm2_g,
           w_grp, b_grp, w_exp, b_exp, w1, w3, w2):
    depth = w_in.shape[0]
    assert depth == 1
    d = D_MODEL
    dec_b, dec_l = x_sample.shape[0], x_sample.shape[1]
    cos, sin = _rope_tables(dec_l)
    bd = (jnp.arange(MXU_DIM)[:, None] // QK_DIM == jnp.arange(MXU_DIM)[None, :] // QK_DIM).astype(BF16)
    xp, xs = x_prompt, x_sample
    new_k, new_v = [], []
    for l in range(depth):
        lam_init = 0.8 - 0.6 * math.exp(-0.3 * l)
        mod_rows = 16
        cvec = jnp.concatenate([c_ctx[None, :], c, jnp.zeros((mod_rows - 1 - dec_b, d), F32)], axis=0)
        mod = _ada(cvec, w_ada[l], b_ada[l][None, :]).reshape(mod_rows, 6, d)

        wi = w_in[l].astype(BF16)
        w_router = jnp.concatenate([w_grp[l], w_exp[l]], axis=1)
        w_router = jnp.pad(w_router, ((0, 0), (0, LANES - w_router.shape[1])))
        wr_hi = w_router.astype(BF16)
        wr_lo = (w_router - wr_hi.astype(F32)).astype(BF16)
        b_router = jnp.pad(jnp.concatenate([b_grp[l], b_exp[l]]), (0, LANES - N_GROUPS - N_EXPERTS))[None, :]
        wts = dict(
            norm1_g=norm1_g[l][None, :], norm2_g=norm2_g[l][None, :],
            w_conv_in=wi[:, :3 * d].reshape(d, 3, d).transpose(1, 0, 2),
            w_qkv=wi[:, 3 * d:6 * d].reshape(d, 3, d).transpose(1, 0, 2),
            w_gate=wi[:, 6 * d:].reshape(d, 2, d).transpose(1, 0, 2),
            conv_w=conv_w[l], w_conv_out=w_conv_out[l].astype(BF16),
            gq=jnp.tile(q_norm_g[l], d // QK_DIM)[None, :], gk=jnp.tile(k_norm_g[l], d // QK_DIM)[None, :],
            bd=bd, cos=cos, sin=sin,
            lamv=jnp.stack([lambda_q1[l], lambda_k1[l], lambda_q2[l], lambda_k2[l]]),
            subln_g=subln_g[l][None, :], w_attn_out=w_attn_out[l].astype(BF16), w_o=w_o[l].astype(BF16),
            wr_hi=wr_hi, wr_lo=wr_lo, b_router=b_router,
        )
        x1p, h2p, route_p, k_ctx, v_ctx = _layer(xp, mod, 0, False, None, wts, lam_init)
        new_k.append(k_ctx.reshape(x_prompt.shape[0], x_prompt.shape[1], N_HEADS, 2, QK_DIM))
        new_v.append(v_ctx.reshape(x_prompt.shape[0], x_prompt.shape[1], N_HEADS, V_DIM))
        past = cache_k.shape[2]
        cache = (cache_k[:, l].reshape(dec_b, past, d), cache_v[:, l].reshape(dec_b, past, d))
        x1s, h2s, route_s = _layer(xs, mod, 1, True, cache, wts, lam_init)
        tm = 512
        yp, ys = _routed_moe(x1p, x1s, jnp.concatenate([h2p, h2s]), jnp.concatenate([route_p, route_s]), mod,
                             w1[l], w3[l], w2[l], dec_l // tm, tm)
        xp, xs = yp.reshape(x_prompt.shape), ys.reshape(x_sample.shape)
    return (xp, xs, jnp.stack(new_k, axis=1), jnp.stack(new_v, axis=1))
```

```python
import functools
import math

import jax
import jax.numpy as jnp
from jax import lax
from jax.experimental import pallas as pl
from jax.experimental.pallas import tpu as pltpu

D_MODEL = 1024
GRID_W = 64
N_HEADS = 8
QK_DIM = 64
V_DIM = 2 * QK_DIM
AXIS_DIM = QK_DIM // 2
ROPE_BASE = 10000.0
N_GROUPS = 4
EXPERTS_PER_GROUP = 4
N_EXPERTS = N_GROUPS * EXPERTS_PER_GROUP
D_EXPERT = 512
EPS = 1e-6

LANES = 128
MXU_DIM = 256
VMEM_LIMIT_BYTES = 56 * 1024 * 1024
ROUTER_LANE0 = N_GROUPS
MOE_TOKEN_TILE = 512
MOE_ROWS = 256
META_ROWS = 8
DMA_UNROLL = 8

F32 = jnp.float32
BF16 = jnp.bfloat16


def _dot(a, b):
    return jnp.dot(a, b, preferred_element_type=F32)


def _dot_nt(a, b):
    return lax.dot_general(a, b, (((1,), (1,)), ((), ())), preferred_element_type=F32)


def _split_bf16(x):
    hi = x.astype(BF16)
    lo = (x - hi.astype(F32)).astype(BF16)
    return hi, lo


def _sigmoid(x):
    return 1.0 / (1.0 + jnp.exp(-x))


def _rms(x):
    return x * lax.rsqrt(jnp.mean(x * x, axis=-1, keepdims=True) + EPS)


def _params(*sem):
    return pltpu.CompilerParams(dimension_semantics=sem, vmem_limit_bytes=VMEM_LIMIT_BYTES)


def _ada_kernel(c_ref, w_ref, b_ref, o_ref):
    c = c_ref[...]
    s = c * _sigmoid(c)
    s_hi, s_lo = _split_bf16(s)
    w_hi, w_lo = _split_bf16(w_ref[...])
    o_ref[...] = _dot(s_hi, w_hi) + _dot(s_lo, w_hi) + _dot(s_hi, w_lo) + b_ref[...]


def _ada(cvec, w_ada, b_ada):
    rows, d = cvec.shape
    n = w_ada.shape[1]
    tn = 1024
    return pl.pallas_call(
        _ada_kernel,
        grid=(n // tn,),
        in_specs=[pl.BlockSpec((rows, d), lambda j: (0, 0)),
                  pl.BlockSpec((d, tn), lambda j: (0, j)),
                  pl.BlockSpec((1, tn), lambda j: (0, j))],
        out_specs=pl.BlockSpec((rows, tn), lambda j: (0, j)),
        out_shape=jax.ShapeDtypeStruct((rows, n), F32),
        compiler_params=_params("arbitrary"),
        name="ada",
    )(cvec, w_ada, b_ada)


def _norm_mod(x, g, scale, shift):
    return (_rms(x) * g) * (1.0 + scale) + shift


def _conv_kernel(x_ref, mod_ref, g1_ref, w_ref, cw_ref, o_ref, *, seq_len):
    x = x_ref[...]
    tm = x.shape[0]
    hb = _norm_mod(x, g1_ref[...], mod_ref[0, 1:2, :], mod_ref[0, 0:1, :]).astype(BF16)
    pos = lax.broadcasted_iota(jnp.int32, (tm, 1), 0) % seq_len
    first = pos == 0
    last = pos == seq_len - 1
    for c in range(D_MODEL // MXU_DIM):
        sl = slice(c * MXU_DIM, (c + 1) * MXU_DIM)
        cb = _dot(hb, w_ref[0, :, sl])
        u = _dot(hb, w_ref[1, :, sl]) * _dot(hb, w_ref[2, :, sl])
        up = jnp.where(first, 0.0, pltpu.roll(u, 1, 0))
        dn = jnp.where(last, 0.0, pltpu.roll(u, tm - 1, 0))
        cw = cw_ref[:, sl]
        conv = cw[0:1] * up + cw[1:2] * u + cw[2:3] * dn
        o_ref[:, sl] = (cb * conv).astype(BF16)


def _chunk_norm(z, bd, g):
    sq = (z * z).astype(BF16)
    ss = jnp.concatenate(
        [_dot(sq[:, s * MXU_DIM:(s + 1) * MXU_DIM], bd) for s in range(z.shape[1] // MXU_DIM)], axis=1)
    return z * lax.rsqrt(ss * (1.0 / QK_DIM) + EPS) * g


def _rope(z, cos, sin, hi_half):
    n = z.shape[1]
    swap = jnp.where(hi_half, pltpu.roll(z, AXIS_DIM // 2, 1), pltpu.roll(z, n - AXIS_DIM // 2, 1))
    return z * cos + swap * sin


def _qkv_kernel(*refs, latent):
    if latent:
        x_ref, mod_ref, g1_ref, w_ref, bd_ref, gq_ref, gk_ref, cos_ref, sin_ref, q_ref, k_ref, v_ref = refs
    else:
        x_ref, mod_ref, g1_ref, w_ref, bd_ref, gq_ref, gk_ref, q_ref, k_ref, v_ref, kn_ref, vn_ref = refs
    hb = _norm_mod(x_ref[...], g1_ref[...], mod_ref[0, 1:2, :], mod_ref[0, 0:1, :]).astype(BF16)
    bd = bd_ref[...]
    q = _chunk_norm(_dot(hb, w_ref[0]), bd, gq_ref[...])
    k = _chunk_norm(_dot(hb, w_ref[1]), bd, gk_ref[...])
    v = _dot(hb, w_ref[2])
    if latent:
        reps = D_MODEL // LANES
        cos = jnp.concatenate([cos_ref[...]] * reps, axis=1)
        sin = jnp.concatenate([sin_ref[...]] * reps, axis=1)
        lane = lax.broadcasted_iota(jnp.int32, (1, D_MODEL), 1)
        hi_half = (lane & (AXIS_DIM // 2)) != 0
        q = _rope(q, cos, sin, hi_half)
        k_att = _rope(k, cos, sin, hi_half)
    else:
        k_att = k
        kn_ref[...] = k
        vn_ref[...] = v
    q_ref[...] = (q * (1.0 / math.sqrt(QK_DIM))).astype(BF16)
    k_ref[...] = k_att.astype(BF16)
    v_ref[...] = v.astype(BF16)


def _attn_kernel(*refs, latent, lam_init):
    if latent:
        lamv_ref, q_ref, k_ref, v_ref, ck_ref, cv_ref, sg_ref, o_ref = refs
    else:
        lamv_ref, q_ref, k_ref, v_ref, sg_ref, o_ref = refs
    lv = lamv_ref[...]
    lam = (jnp.exp(jnp.sum(lv[0:1] * lv[1:2], axis=1, keepdims=True))
           - jnp.exp(jnp.sum(lv[2:3] * lv[3:4], axis=1, keepdims=True)) + lam_init)
    q = q_ref[0]
    lane = lax.broadcasted_iota(jnp.int32, (1, V_DIM), 1)
    q1 = jnp.where(lane < QK_DIM, q, jnp.zeros_like(q))
    q2 = jnp.where(lane >= QK_DIM, q, jnp.zeros_like(q))
    k = k_ref[0]
    v = v_ref[0]
    keys = [k]
    vals = [v]
    if latent:
        keys.append(ck_ref[0].astype(BF16))
        vals.append(cv_ref[0].astype(BF16))

    def probs(qh):
        s = [_dot_nt(qh, kk) for kk in keys]
        m = s[0].max(axis=1, keepdims=True)
        for si in s[1:]:
            m = jnp.maximum(m, si.max(axis=1, keepdims=True))
        e = [jnp.exp(si - m) for si in s]
        l = e[0].sum(axis=1, keepdims=True)
        for ei in e[1:]:
            l = l + ei.sum(axis=1, keepdims=True)
        return e, 1.0 / l

    e1, r1 = probs(q1)
    e2, r2 = probs(q2)
    r2 = r2 * lam
    o = None
    for a1, a2, vv in zip(e1, e2, vals):
        part = _dot((a1 * r1 - a2 * r2).astype(BF16), vv)
        o = part if o is None else o + part
    o_ref[0] = (_rms(o) * sg_ref[...] * (1.0 - lam_init)).astype(BF16)


def _merge_kernel(x_ref, mod_ref, g1_ref, yc_ref, oa_ref, wg_ref, wco_ref, wao_ref, wo_ref, g2_ref,
                  wrh_ref, wrl_ref, br_ref, tri_ref, x1_ref, h2_ref, route_ref, meta_ref, cnt_ref):
    x = x_ref[...]
    hb = _norm_mod(x, g1_ref[...], mod_ref[0, 1:2, :], mod_ref[0, 0:1, :]).astype(BF16)
    y_conv = _dot(yc_ref[...], wco_ref[...])
    y_attn = _dot(oa_ref[...], wao_ref[...])
    mixed = _sigmoid(_dot(hb, wg_ref[0])) * y_conv + _sigmoid(_dot(hb, wg_ref[1])) * y_attn
    x1 = x + mod_ref[0, 2:3, :] * _dot(mixed.astype(BF16), wo_ref[...])
    x1_ref[...] = x1
    h2 = _norm_mod(x1, g2_ref[...], mod_ref[0, 4:5, :], mod_ref[0, 3:4, :])
    h2_ref[...] = h2

    h_hi, h_lo = _split_bf16(h2)
    logits = _dot(h_hi, wrh_ref[...]) + _dot(h_lo, wrh_ref[...]) + _dot(h_hi, wrl_ref[...]) + br_ref[...]
    tm = x.shape[0]
    lane = lax.broadcasted_iota(jnp.int32, (tm, LANES), 1).astype(F32)
    ninf = -jnp.inf
    big = float(LANES)

    def top(vals):
        vmax = vals.max(axis=1, keepdims=True)
        idx = jnp.where(vals == vmax, lane, big).min(axis=1, keepdims=True)
        return vmax, idx

    is_grp = lane < N_GROUPS
    gmax, gidx = top(jnp.where(is_grp, logits, ninf))
    g_p = 1.0 / jnp.where(is_grp, jnp.exp(logits - gmax), 0.0).sum(axis=1, keepdims=True)
    lo = ROUTER_LANE0 + EXPERTS_PER_GROUP * gidx
    el = jnp.where((lane >= lo) & (lane < lo + EXPERTS_PER_GROUP), logits, ninf)
    v1, i1 = top(el)
    v2, i2 = top(jnp.where(lane == i1, ninf, el))
    e2 = jnp.exp(v2 - v1)
    den = 1.0 / (1.0 + e2)
    e0 = i1 - ROUTER_LANE0
    e1 = i2 - ROUTER_LANE0
    onehot = jnp.where((lane == e0) | (lane == e1), 1.0, 0.0)
    before = _dot(tri_ref[...], onehot.astype(BF16))
    r0 = jnp.where(lane == e0, before, 0.0).sum(axis=1, keepdims=True)
    r1 = jnp.where(lane == e1, before, 0.0).sum(axis=1, keepdims=True)
    route = (jnp.where(lane == 0.0, e0, 0.0) + jnp.where(lane == 1.0, e1, 0.0)
             + jnp.where(lane == 2.0, den * g_p, 0.0) + jnp.where(lane == 3.0, e2 * den * g_p, 0.0)
             + jnp.where(lane == 4.0, r0, 0.0) + jnp.where(lane == 5.0, r1, 0.0))
    route_ref[...] = route
    meta_ref[...] = jnp.transpose(route)[:META_ROWS, :]
    cnt_ref[0] = jnp.broadcast_to(onehot.sum(axis=0, keepdims=True), (META_ROWS, LANES))


def _row_copy(src, dst, sem):
    return pltpu.make_async_copy(src, dst, sem)


def _dispatch_kernel(pos_ref, last_ref, nu_ref, h2p_ref, h2s_ref, xs_ref, zero_buf, sem, *, n_p, n_tok, n_tiles):
    i = pl.program_id(0)
    tm = h2p_ref.shape[0]

    @pl.when(i == 0)
    def _():
        zero_buf[...] = jnp.zeros_like(zero_buf)

        def zero_tile(row):
            return _row_copy(zero_buf, xs_ref.at[pl.ds(pl.multiple_of(row, MOE_ROWS), MOE_ROWS)], sem)

        for e in range(N_EXPERTS):
            @pl.when(last_ref[e] >= 0)
            def _():
                zero_tile(last_ref[e]).start()

        def start_tail(j, carry):
            zero_tile(j * MOE_ROWS).start()
            return carry

        lax.fori_loop(nu_ref[0], n_tiles, start_tail, 0)
        for e in range(N_EXPERTS):
            @pl.when(last_ref[e] >= 0)
            def _():
                zero_tile(0).wait()

        def wait_tail(j, carry):
            zero_tile(0).wait()
            return carry

        lax.fori_loop(nu_ref[0], n_tiles, wait_tail, 0)

    base = i * tm

    def scatter(h2_ref):
        def issue(g, carry):
            for u in range(DMA_UNROLL):
                r = g * DMA_UNROLL + u
                for slot in range(2):
                    p = pos_ref[slot * n_tok + base + r]
                    _row_copy(h2_ref.at[pl.ds(r, 1)], xs_ref.at[pl.ds(p, 1)], sem).start()
            return carry

        lax.fori_loop(0, tm // DMA_UNROLL, issue, 0)
        for slot in range(2):
            _row_copy(h2_ref, xs_ref.at[pl.ds(0, tm)], sem).wait()

    @pl.when(i < n_p)
    def _():
        scatter(h2p_ref)

    @pl.when(i >= n_p)
    def _():
        scatter(h2s_ref)


def _expert_kernel(te_ref, nu_ref, x_ref, w1_ref, w3_ref, w2_ref, y_ref, w1b, w3b, w2b):
    j = pl.program_id(0)

    @pl.when((j == 0) | (te_ref[j] != te_ref[jnp.maximum(j - 1, 0)]))
    def _():
        w1b[...] = w1_ref[0].astype(BF16)
        w3b[...] = w3_ref[0].astype(BF16)
        w2b[...] = w2_ref[0].astype(BF16)

    @pl.when(j < nu_ref[0])
    def _():
        t = x_ref[...].astype(BF16)
        a = _dot(t, w1b[...])
        he = (a * _sigmoid(a)) * _dot(t, w3b[...])
        y_ref[...] = _dot(he.astype(BF16), w2b[...])

    @pl.when(j >= nu_ref[0])
    def _():
        y_ref[...] = jnp.zeros_like(y_ref)


def _combine_kernel(pos_ref, x1p_ref, x1s_ref, rp_ref, rs_ref, mod_ref, ys_ref, op_ref, os_ref, ybuf, sem,
                    *, n_p, n_tok):
    i = pl.program_id(0)
    tm = ybuf.shape[1]
    base = i * tm

    def issue(g, carry):
        for u in range(DMA_UNROLL):
            r = g * DMA_UNROLL + u
            for slot in range(2):
                p = pos_ref[slot * n_tok + base + r]
                _row_copy(ys_ref.at[pl.ds(p, 1)], ybuf.at[slot, pl.ds(r, 1)], sem).start()
        return carry

    lax.fori_loop(0, tm // DMA_UNROLL, issue, 0)
    for slot in range(2):
        _row_copy(ybuf.at[slot], ybuf.at[slot], sem).wait()

    def out(x1_ref, r_ref, o_ref):
        r = r_ref[...]
        o_ref[...] = x1_ref[...] + mod_ref[0, 5:6, :] * (r[:, 2:3] * ybuf[0] + r[:, 3:4] * ybuf[1])

    @pl.when(i < n_p)
    def _():
        out(x1p_ref, rp_ref, op_ref)

    @pl.when(i >= n_p)
    def _():
        out(x1s_ref, rs_ref, os_ref)


def _layer(x3, mod, row0, latent, cache, wts, lam_init):
    b, l, d = x3.shape
    t = b * l
    x = x3.reshape(t, d)
    tm = 512
    nt = t // tm
    per_seq = l // tm if l >= tm else 1
    if latent:
        mod_row = lambda i: row0 + i // per_seq
    else:
        mod_row = lambda i: row0

    x_spec = pl.BlockSpec((tm, d), lambda i: (i, 0))
    mod_spec = pl.BlockSpec((1, 6, d), lambda i: (mod_row(i), 0, 0))
    row_spec = pl.BlockSpec((1, d), lambda i: (0, 0))
    act_spec = pl.BlockSpec((tm, d), lambda i: (i, 0))

    def full(a):
        nd = a.ndim
        return pl.BlockSpec(a.shape, lambda i: (0,) * nd)

    tmc = max(tm, l)
    assert tmc % l == 0 and t % tmc == 0
    seq_per_tile = tmc // l
    xc_spec = pl.BlockSpec((tmc, d), lambda i: (i, 0))
    modc_spec = pl.BlockSpec((1, 6, d), lambda i: (row0 + (i * seq_per_tile if latent else 0), 0, 0))
    assert not latent or seq_per_tile == 1
    yc = pl.pallas_call(
        functools.partial(_conv_kernel, seq_len=l),
        grid=(t // tmc,),
        in_specs=[xc_spec, modc_spec, row_spec, full(wts["w_conv_in"]), full(wts["conv_w"])],
        out_specs=xc_spec,
        out_shape=jax.ShapeDtypeStruct((t, d), BF16),
        compiler_params=_params("parallel"),
        name="conv_in",
    )(x, mod, wts["norm1_g"], wts["w_conv_in"], wts["conv_w"])

    qkv_in = [x, mod, wts["norm1_g"], wts["w_qkv"], wts["bd"], wts["gq"], wts["gk"]]
    qkv_specs = [x_spec, mod_spec, row_spec, full(wts["w_qkv"]), full(wts["bd"]), row_spec, row_spec]
    out_shapes = [jax.ShapeDtypeStruct((t, d), BF16)] * 3
    out_specs = [act_spec] * 3
    if latent:
        qkv_in += [wts["cos"], wts["sin"]]
        tab_spec = pl.BlockSpec((tm, LANES), lambda i: (i % per_seq, 0))
        qkv_specs += [tab_spec, tab_spec]
    else:
        out_shapes += [jax.ShapeDtypeStruct((t, d), F32)] * 2
        out_specs += [act_spec] * 2
    qkv_out = pl.pallas_call(
        functools.partial(_qkv_kernel, latent=latent),
        grid=(nt,),
        in_specs=qkv_specs,
        out_specs=out_specs,
        out_shape=out_shapes,
        compiler_params=_params("parallel"),
        name="qkv",
    )(*qkv_in)
    q, k, v = qkv_out[:3]

    tq = 256
    q3, k3, v3 = (a.reshape(b, l, d) for a in (q, k, v))
    q_spec = pl.BlockSpec((1, tq, V_DIM), lambda bi, h, qi: (bi, qi, h))
    kv_spec = pl.BlockSpec((1, l, V_DIM), lambda bi, h, qi: (bi, 0, h))
    sg_spec = pl.BlockSpec((1, V_DIM), lambda bi, h, qi: (0, 0))
    lam_spec = pl.BlockSpec(wts["lamv"].shape, lambda bi, h, qi: (0, 0))
    attn_in = [wts["lamv"], q3, k3, v3]
    attn_specs = [lam_spec, q_spec, kv_spec, kv_spec]
    if latent:
        ck, cv = cache
        c_spec = pl.BlockSpec((1, ck.shape[1], V_DIM), lambda bi, h, qi: (bi, 0, h))
        attn_in += [ck, cv]
        attn_specs += [c_spec, c_spec]
    attn_in.append(wts["subln_g"])
    attn_specs.append(sg_spec)
    oa = pl.pallas_call(
        functools.partial(_attn_kernel, latent=latent, lam_init=lam_init),
        grid=(b, N_HEADS, l // tq),
        in_specs=attn_specs,
        out_specs=q_spec,
        out_shape=jax.ShapeDtypeStruct((b, l, d), BF16),
        compiler_params=_params("parallel", "parallel", "arbitrary"),
        name="attn",
    )(*attn_in).reshape(t, d)

    lane_spec = pl.BlockSpec((tm, LANES), lambda i: (i, 0))
    assert tm == MOE_TOKEN_TILE
    x1, h2, route, meta, counts = pl.pallas_call(
        _merge_kernel,
        grid=(nt,),
        in_specs=[x_spec, mod_spec, row_spec, act_spec, act_spec, full(wts["w_gate"]), full(wts["w_conv_out"]),
                  full(wts["w_attn_out"]), full(wts["w_o"]), row_spec, full(wts["wr_hi"]), full(wts["wr_lo"]),
                  full(wts["b_router"]), full(wts["tri"])],
        out_specs=[x_spec, x_spec, lane_spec, pl.BlockSpec((META_ROWS, tm), lambda i: (0, i)),
                   pl.BlockSpec((1, META_ROWS, LANES), lambda i: (i, 0, 0))],
        out_shape=[jax.ShapeDtypeStruct((t, d), F32), jax.ShapeDtypeStruct((t, d), F32),
                   jax.ShapeDtypeStruct((t, LANES), F32), jax.ShapeDtypeStruct((META_ROWS, t), F32),
                   jax.ShapeDtypeStruct((nt, META_ROWS, LANES), F32)],
        compiler_params=_params("parallel"),
        name="merge",
    )(x, mod, wts["norm1_g"], yc, oa, wts["w_gate"], wts["w_conv_out"], wts["w_attn_out"], wts["w_o"],
      wts["norm2_g"], wts["wr_hi"], wts["wr_lo"], wts["b_router"], wts["tri"])

    outs = [x1, h2, route, meta, counts[:, 0, :N_EXPERTS]]
    if not latent:
        outs += [qkv_out[3], qkv_out[4]]
    return outs


def _routing_tables(meta, counts, tm, n_tiles):
    eid = meta[0:2].astype(jnp.int32)
    rank = meta[4:6].astype(jnp.int32)
    counts = counts.astype(jnp.int32)
    totals = counts.sum(axis=0)
    tiles_e = (totals + MOE_ROWS - 1) // MOE_ROWS
    tile_end = jnp.cumsum(tiles_e)
    row_start = (tile_end - tiles_e) * MOE_ROWS
    base = row_start[None, :] + jnp.cumsum(counts, axis=0) - counts
    base_tok = jnp.repeat(base.T, tm, axis=1)
    ex = jnp.arange(N_EXPERTS, dtype=jnp.int32)[None, :, None]
    pos = jnp.sum(jnp.where(eid[:, None, :] == ex, base_tok[None], 0), axis=1) + rank
    n_used = tile_end[-1]
    tile_ids = jnp.minimum(jnp.arange(n_tiles, dtype=jnp.int32), n_used - 1)
    tile_expert = jnp.sum((tile_ids[:, None] >= tile_end[None, :]).astype(jnp.int32), axis=1)
    last_tile_row = jnp.where(tiles_e > 0, (tile_end - 1) * MOE_ROWS, -1)
    return (pos.reshape(-1), tile_expert.astype(jnp.int32), n_used.astype(jnp.int32).reshape(1),
            last_tile_row.astype(jnp.int32))


def _routed_moe(streams, mod, w1, w3, w2, per_seq_tiles, tm):
    (x1p, h2p, rp, _, _), (x1s, h2s, rs, _, _) = streams
    d = x1p.shape[1]
    tp, ts = x1p.shape[0], x1s.shape[0]
    t = tp + ts
    n_tiles = 2 * t // MOE_ROWS + N_EXPERTS
    meta = jnp.concatenate([s[3] for s in streams], axis=1)
    counts = jnp.concatenate([s[4] for s in streams], axis=0)
    pos, tile_expert, n_used, last_tile_row = _routing_tables(meta, counts, tm, n_tiles)

    sorted_shape = jax.ShapeDtypeStruct((n_tiles * MOE_ROWS, d), F32)
    any_spec = pl.BlockSpec(memory_space=pl.ANY)
    n_p = tp // tm
    x_sorted = pl.pallas_call(
        functools.partial(_dispatch_kernel, n_p=n_p, n_tok=t, n_tiles=n_tiles),
        grid_spec=pltpu.PrefetchScalarGridSpec(
            num_scalar_prefetch=3, grid=(t // tm,),
            in_specs=[pl.BlockSpec((tm, d), lambda i, *_: (jnp.minimum(i, n_p - 1), 0)),
                      pl.BlockSpec((tm, d), lambda i, *_: (jnp.maximum(i - n_p, 0), 0))],
            out_specs=any_spec,
            scratch_shapes=[pltpu.VMEM((MOE_ROWS, d), F32), pltpu.SemaphoreType.DMA]),
        out_shape=sorted_shape,
        compiler_params=_params("arbitrary"),
        name="dispatch",
    )(pos, last_tile_row, n_used, h2p, h2s)

    def row_blk(j, te, nu):
        return (jnp.minimum(j, nu[0] - 1), 0)

    y_sorted = pl.pallas_call(
        _expert_kernel,
        grid_spec=pltpu.PrefetchScalarGridSpec(
            num_scalar_prefetch=2,
            grid=(n_tiles,),
            in_specs=[pl.BlockSpec((MOE_ROWS, d), row_blk),
                      pl.BlockSpec((1, d, D_EXPERT), lambda j, te, nu: (te[j], 0, 0)),
                      pl.BlockSpec((1, d, D_EXPERT), lambda j, te, nu: (te[j], 0, 0)),
                      pl.BlockSpec((1, D_EXPERT, d), lambda j, te, nu: (te[j], 0, 0))],
            out_specs=pl.BlockSpec((MOE_ROWS, d), lambda j, te, nu: (j, 0)),
            scratch_shapes=[pltpu.VMEM((d, D_EXPERT), BF16), pltpu.VMEM((d, D_EXPERT), BF16),
                            pltpu.VMEM((D_EXPERT, d), BF16)]),
        out_shape=jax.ShapeDtypeStruct((n_tiles * MOE_ROWS, d), F32),
        compiler_params=_params("arbitrary"),
        name="experts",
    )(tile_expert, n_used, x_sorted, w1, w3, w2)

    ctx_blk = lambda i, p: (jnp.minimum(i, n_p - 1), 0)
    lat_blk = lambda i, p: (jnp.maximum(i - n_p, 0), 0)
    return pl.pallas_call(
        functools.partial(_combine_kernel, n_p=n_p, n_tok=t),
        grid_spec=pltpu.PrefetchScalarGridSpec(
            num_scalar_prefetch=1, grid=(t // tm,),
            in_specs=[pl.BlockSpec((tm, d), ctx_blk), pl.BlockSpec((tm, d), lat_blk),
                      pl.BlockSpec((tm, LANES), ctx_blk), pl.BlockSpec((tm, LANES), lat_blk),
                      pl.BlockSpec((1, 6, d),
                                   lambda i, p: (jnp.where(i < n_p, 0, 1 + (i - n_p) // per_seq_tiles), 0, 0)),
                      any_spec],
            out_specs=[pl.BlockSpec((tm, d), ctx_blk), pl.BlockSpec((tm, d), lat_blk)],
            scratch_shapes=[pltpu.VMEM((2, tm, d), F32), pltpu.SemaphoreType.DMA]),
        out_shape=[jax.ShapeDtypeStruct((tp, d), F32), jax.ShapeDtypeStruct((ts, d), F32)],
        compiler_params=_params("arbitrary"),
        name="combine",
    )(pos, x1p, x1s, rp, rs, mod, y_sorted)


def _rope_tables(n_tok):
    rows = n_tok // GRID_W
    pos_row = jnp.repeat(jnp.arange(rows, dtype=F32), GRID_W)
    pos_col = jnp.tile(jnp.arange(GRID_W, dtype=F32), rows)
    inv_freq = 1.0 / (ROPE_BASE ** (jnp.arange(0, AXIS_DIM, 2, dtype=F32) / AXIS_DIM))
    ang = jnp.concatenate([pos_row[:, None] * inv_freq[None, :]] * 2
                          + [pos_col[:, None] * inv_freq[None, :]] * 2, axis=1)
    sign = jnp.tile(jnp.concatenate([-jnp.ones((AXIS_DIM // 2,), F32), jnp.ones((AXIS_DIM // 2,), F32)]), 2)
    cos = jnp.cos(ang)
    sin = jnp.sin(ang) * sign[None, :]
    return jnp.tile(cos, (1, LANES // QK_DIM)), jnp.tile(sin, (1, LANES // QK_DIM))


def kernel(x_prompt, x_sample, cache_k, cache_v, c, c_ctx, w_ada, b_ada, norm1_g, w_in, conv_w, w_conv_out,
           q_norm_g, k_norm_g, lambda_q1, lambda_k1, lambda_q2, lambda_k2, subln_g, w_attn_out, w_o, norm2_g,
           w_grp, b_grp, w_exp, b_exp, w1, w3, w2):
    depth = w_in.shape[0]
    assert depth == 1
    d = D_MODEL
    dec_b, dec_l = x_sample.shape[0], x_sample.shape[1]
    cos, sin = _rope_tables(dec_l)
    bd = (jnp.arange(MXU_DIM)[:, None] // QK_DIM == jnp.arange(MXU_DIM)[None, :] // QK_DIM).astype(BF16)
    tri = (jnp.arange(MOE_TOKEN_TILE)[:, None] > jnp.arange(MOE_TOKEN_TILE)[None, :]).astype(BF16)
    xp, xs = x_prompt, x_sample
    new_k, new_v = [], []
    for l in range(depth):
        lam_init = 0.8 - 0.6 * math.exp(-0.3 * l)
        mod_rows = 16
        cvec = jnp.concatenate([c_ctx[None, :], c, jnp.zeros((mod_rows - 1 - dec_b, d), F32)], axis=0)
        mod = _ada(cvec, w_ada[l], b_ada[l][None, :]).reshape(mod_rows, 6, d)

        wi = w_in[l].astype(BF16)
        w_router = jnp.concatenate([w_grp[l], w_exp[l]], axis=1)
        w_router = jnp.pad(w_router, ((0, 0), (0, LANES - w_router.shape[1])))
        wr_hi = w_router.astype(BF16)
        wr_lo = (w_router - wr_hi.astype(F32)).astype(BF16)
        b_router = jnp.pad(jnp.concatenate([b_grp[l], b_exp[l]]), (0, LANES - N_GROUPS - N_EXPERTS))[None, :]
        wts = dict(
            norm1_g=norm1_g[l][None, :], norm2_g=norm2_g[l][None, :],
            w_conv_in=wi[:, :3 * d].reshape(d, 3, d).transpose(1, 0, 2),
            w_qkv=wi[:, 3 * d:6 * d].reshape(d, 3, d).transpose(1, 0, 2),
            w_gate=wi[:, 6 * d:].reshape(d, 2, d).transpose(1, 0, 2),
            conv_w=conv_w[l], w_conv_out=w_conv_out[l].astype(BF16),
            gq=jnp.tile(q_norm_g[l], d // QK_DIM)[None, :], gk=jnp.tile(k_norm_g[l], d // QK_DIM)[None, :],
            bd=bd, cos=cos, sin=sin,
            lamv=jnp.stack([lambda_q1[l], lambda_k1[l], lambda_q2[l], lambda_k2[l]]),
            subln_g=subln_g[l][None, :], w_attn_out=w_attn_out[l].astype(BF16), w_o=w_o[l].astype(BF16),
            wr_hi=wr_hi, wr_lo=wr_lo, b_router=b_router, tri=tri,
        )
        *ctx, k_ctx, v_ctx = _layer(xp, mod, 0, False, None, wts, lam_init)
        new_k.append(k_ctx.reshape(x_prompt.shape[0], x_prompt.shape[1], N_HEADS, 2, QK_DIM))
        new_v.append(v_ctx.reshape(x_prompt.shape[0], x_prompt.shape[1], N_HEADS, V_DIM))
        past = cache_k.shape[2]
        cache = (cache_k[:, l].reshape(dec_b, past, d), cache_v[:, l].reshape(dec_b, past, d))
        lat = _layer(xs, mod, 1, True, cache, wts, lam_init)
        tm = MOE_TOKEN_TILE
        yp, ys = _routed_moe((ctx, lat), mod, w1[l], w3[l], w2[l], dec_l // tm, tm)
        xp, xs = yp.reshape(x_prompt.shape), ys.reshape(x_sample.shape)
    return (xp, xs, jnp.stack(new_k, axis=1), jnp.stack(new_v, axis=1))
```

```python
import functools
import math

import jax
import jax.numpy as jnp
from jax import lax
from jax.experimental import pallas as pl
from jax.experimental.pallas import tpu as pltpu

D_MODEL = 1024
GRID_W = 64
N_HEADS = 8
QK_DIM = 64
V_DIM = 2 * QK_DIM
AXIS_DIM = QK_DIM // 2
ROPE_BASE = 10000.0
N_GROUPS = 4
EXPERTS_PER_GROUP = 4
N_EXPERTS = N_GROUPS * EXPERTS_PER_GROUP
D_EXPERT = 512
EPS = 1e-6

LANES = 128
MXU_DIM = 256
VMEM_LIMIT_BYTES = 56 * 1024 * 1024
ROUTER_LANE0 = N_GROUPS
MOE_TOKEN_TILE = 512
MOE_ROWS = 256
META_ROWS = 8
DMA_UNROLL = 8
ATTN_Q_ROWS = 256
ATTN_UNITS_PER_STEP = 8
ATTN_SCORES_AHEAD = 1

F32 = jnp.float32
BF16 = jnp.bfloat16


def _dot(a, b):
    return jnp.dot(a, b, preferred_element_type=F32)


def _dot_nt(a, b):
    return lax.dot_general(a, b, (((1,), (1,)), ((), ())), preferred_element_type=F32)


def _split_bf16(x):
    hi = x.astype(BF16)
    lo = (x - hi.astype(F32)).astype(BF16)
    return hi, lo


def _sigmoid(x):
    return 1.0 / (1.0 + jnp.exp(-x))


def _rms(x):
    return x * lax.rsqrt(jnp.mean(x * x, axis=-1, keepdims=True) + EPS)


def _params(*sem):
    return pltpu.CompilerParams(dimension_semantics=sem, vmem_limit_bytes=VMEM_LIMIT_BYTES)


def _ada_kernel(c_ref, w_ref, b_ref, o_ref):
    c = c_ref[...]
    s = c * _sigmoid(c)
    s_hi, s_lo = _split_bf16(s)
    w_hi, w_lo = _split_bf16(w_ref[...])
    o_ref[...] = _dot(s_hi, w_hi) + _dot(s_lo, w_hi) + _dot(s_hi, w_lo) + b_ref[...]


def _ada(cvec, w_ada, b_ada):
    rows, d = cvec.shape
    n = w_ada.shape[1]
    tn = 1024
    return pl.pallas_call(
        _ada_kernel,
        grid=(n // tn,),
        in_specs=[pl.BlockSpec((rows, d), lambda j: (0, 0)),
                  pl.BlockSpec((d, tn), lambda j: (0, j)),
                  pl.BlockSpec((1, tn), lambda j: (0, j))],
        out_specs=pl.BlockSpec((rows, tn), lambda j: (0, j)),
        out_shape=jax.ShapeDtypeStruct((rows, n), F32),
        compiler_params=_params("arbitrary"),
        name="ada",
    )(cvec, w_ada, b_ada)


def _norm_mod(x, g, scale, shift):
    return (_rms(x) * g) * (1.0 + scale) + shift


def _conv_kernel(x_ref, mod_ref, g1_ref, w_ref, cw_ref, o_ref, *, seq_len):
    x = x_ref[...]
    tm = x.shape[0]
    hb = _norm_mod(x, g1_ref[...], mod_ref[0, 1:2, :], mod_ref[0, 0:1, :]).astype(BF16)
    pos = lax.broadcasted_iota(jnp.int32, (tm, 1), 0) % seq_len
    first = pos == 0
    last = pos == seq_len - 1
    for c in range(D_MODEL // MXU_DIM):
        sl = slice(c * MXU_DIM, (c + 1) * MXU_DIM)
        cb, cc, cx = (_dot(hb, w_ref[:, pl.ds(j * D_MODEL + c * MXU_DIM, MXU_DIM)]) for j in range(3))
        u = cc * cx
        up = jnp.where(first, 0.0, pltpu.roll(u, 1, 0))
        dn = jnp.where(last, 0.0, pltpu.roll(u, tm - 1, 0))
        cw = cw_ref[:, sl]
        conv = cw[0:1] * up + cw[1:2] * u + cw[2:3] * dn
        o_ref[:, sl] = (cb * conv).astype(BF16)


def _chunk_norm(z, bd, g):
    sq = (z * z).astype(BF16)
    ss = jnp.concatenate(
        [_dot(sq[:, s * MXU_DIM:(s + 1) * MXU_DIM], bd) for s in range(z.shape[1] // MXU_DIM)], axis=1)
    return z * lax.rsqrt(ss * (1.0 / QK_DIM) + EPS) * g


def _rope(z, cos, sin, hi_half):
    n = z.shape[1]
    swap = jnp.where(hi_half, pltpu.roll(z, AXIS_DIM // 2, 1), pltpu.roll(z, n - AXIS_DIM // 2, 1))
    return z * cos + swap * sin


def _qkv_kernel(*refs, latent):
    if latent:
        x_ref, mod_ref, g1_ref, w_ref, bd_ref, gq_ref, gk_ref, cos_ref, sin_ref, q_ref, k_ref, v_ref = refs
    else:
        x_ref, mod_ref, g1_ref, w_ref, bd_ref, gq_ref, gk_ref, q_ref, k_ref, v_ref, kn_ref, vn_ref = refs
    hb = _norm_mod(x_ref[...], g1_ref[...], mod_ref[0, 1:2, :], mod_ref[0, 0:1, :]).astype(BF16)
    bd = bd_ref[...]
    q = _chunk_norm(_dot(hb, w_ref[:, 0:D_MODEL]), bd, gq_ref[...])
    k = _chunk_norm(_dot(hb, w_ref[:, D_MODEL:2 * D_MODEL]), bd, gk_ref[...])
    v = _dot(hb, w_ref[:, 2 * D_MODEL:3 * D_MODEL])
    if latent:
        reps = D_MODEL // LANES
        cos = jnp.concatenate([cos_ref[...]] * reps, axis=1)
        sin = jnp.concatenate([sin_ref[...]] * reps, axis=1)
        lane = lax.broadcasted_iota(jnp.int32, (1, D_MODEL), 1)
        hi_half = (lane & (AXIS_DIM // 2)) != 0
        q = _rope(q, cos, sin, hi_half)
        k_att = _rope(k, cos, sin, hi_half)
    else:
        k_att = k
        kn_ref[...] = k
        vn_ref[...] = v
    q_ref[...] = (q * (math.log2(math.e) / math.sqrt(QK_DIM))).astype(BF16)
    k_ref[...] = k_att.astype(BF16)
    v_ref[...] = v.astype(BF16)


def _attn_kernel(*refs, latent, lam_init, tq):
    if latent:
        lamv_ref, q_ref, k_ref, v_ref, ck_ref, cv_ref, sg_ref, o_ref = refs
    else:
        lamv_ref, q_ref, k_ref, v_ref, sg_ref, o_ref = refs
    lv = lamv_ref[...]
    lam = (jnp.exp(jnp.sum(lv[0:1] * lv[1:2], axis=1, keepdims=True))
           - jnp.exp(jnp.sum(lv[2:3] * lv[3:4], axis=1, keepdims=True)) + lam_init)
    lane = lax.broadcasted_iota(jnp.int32, (1, V_DIM), 1)
    n_q = q_ref.shape[1]
    units = [(hh, c) for hh in range(q_ref.shape[2] // V_DIM) for c in range(n_q // tq)]

    def kv(hh):
        hs = slice(hh * V_DIM, (hh + 1) * V_DIM)
        keys = [k_ref[0, :, hs]]
        vals = [v_ref[0, :, hs]]
        if latent:
            keys.append(ck_ref[0, :, hs].astype(BF16))
            vals.append(cv_ref[0, :, hs].astype(BF16))
        return keys, vals

    def scores(u):
        hh, c = units[u]
        q = q_ref[0, c * tq:(c + 1) * tq, hh * V_DIM:(hh + 1) * V_DIM]
        keys, _ = kv(hh)
        halves = (jnp.where(lane < QK_DIM, q, jnp.zeros_like(q)), jnp.where(lane >= QK_DIM, q, jnp.zeros_like(q)))
        return [[_dot_nt(qh, kk) for kk in keys] for qh in halves]

    def probs(s):
        m = s[0].max(axis=1, keepdims=True)
        for si in s[1:]:
            m = jnp.maximum(m, si.max(axis=1, keepdims=True))
        e = [jnp.exp2(si - m) for si in s]
        l = e[0].sum(axis=1, keepdims=True)
        for ei in e[1:]:
            l = l + ei.sum(axis=1, keepdims=True)
        return e, 1.0 / l

    def finish(u, s):
        hh, c = units[u]
        _, vals = kv(hh)
        e1, r1 = probs(s[0])
        e2, r2 = probs(s[1])
        r2 = r2 * lam
        o = None
        for a1, a2, vv in zip(e1, e2, vals):
            part = _dot((a1 * r1 - a2 * r2).astype(BF16), vv)
            o = part if o is None else o + part
        o_ref[0, c * tq:(c + 1) * tq, hh * V_DIM:(hh + 1) * V_DIM] = (
            _rms(o) * sg_ref[...] * (1.0 - lam_init)).astype(BF16)

    ahead = ATTN_SCORES_AHEAD
    pending = [scores(u) for u in range(min(ahead, len(units)))]
    for u in range(len(units)):
        if u + ahead < len(units):
            pending.append(scores(u + ahead))
        finish(u, pending.pop(0))


def _merge_kernel(x_ref, mod_ref, g1_ref, yc_ref, oa_ref, wg_ref, wco_ref, wao_ref, wo_ref, g2_ref,
                  wrh_ref, wrl_ref, br_ref, tri_ref, x1_ref, h2_ref, route_ref, meta_ref, cnt_ref):
    x = x_ref[...]
    hb = _norm_mod(x, g1_ref[...], mod_ref[0, 1:2, :], mod_ref[0, 0:1, :]).astype(BF16)
    y_conv = _dot(yc_ref[...], wco_ref[...])
    y_attn = _dot(oa_ref[...], wao_ref[...])
    mixed = (_sigmoid(_dot(hb, wg_ref[:, 0:D_MODEL])) * y_conv
             + _sigmoid(_dot(hb, wg_ref[:, D_MODEL:2 * D_MODEL])) * y_attn)
    x1 = x + mod_ref[0, 2:3, :] * _dot(mixed.astype(BF16), wo_ref[...])
    x1_ref[...] = x1
    h2 = _norm_mod(x1, g2_ref[...], mod_ref[0, 4:5, :], mod_ref[0, 3:4, :])
    h2_ref[...] = h2

    h_hi, h_lo = _split_bf16(h2)
    logits = _dot(h_hi, wrh_ref[...]) + _dot(h_lo, wrh_ref[...]) + _dot(h_hi, wrl_ref[...]) + br_ref[...]
    tm = x.shape[0]
    lane = lax.broadcasted_iota(jnp.int32, (tm, LANES), 1).astype(F32)
    ninf = -jnp.inf
    big = float(LANES)

    def top(vals):
        vmax = vals.max(axis=1, keepdims=True)
        idx = jnp.where(vals == vmax, lane, big).min(axis=1, keepdims=True)
        return vmax, idx

    is_grp = lane < N_GROUPS
    gmax, gidx = top(jnp.where(is_grp, logits, ninf))
    g_p = 1.0 / jnp.where(is_grp, jnp.exp(logits - gmax), 0.0).sum(axis=1, keepdims=True)
    lo = ROUTER_LANE0 + EXPERTS_PER_GROUP * gidx
    el = jnp.where((lane >= lo) & (lane < lo + EXPERTS_PER_GROUP), logits, ninf)
    v1, i1 = top(el)
    v2, i2 = top(jnp.where(lane == i1, ninf, el))
    e2 = jnp.exp(v2 - v1)
    den = 1.0 / (1.0 + e2)
    e0 = i1 - ROUTER_LANE0
    e1 = i2 - ROUTER_LANE0
    onehot = jnp.where((lane == e0) | (lane == e1), 1.0, 0.0)
    before = _dot(tri_ref[...], onehot.astype(BF16))
    r0 = jnp.where(lane == e0, before, 0.0).sum(axis=1, keepdims=True)
    r1 = jnp.where(lane == e1, before, 0.0).sum(axis=1, keepdims=True)
    route = (jnp.where(lane == 0.0, e0, 0.0) + jnp.where(lane == 1.0, e1, 0.0)
             + jnp.where(lane == 2.0, den * g_p, 0.0) + jnp.where(lane == 3.0, e2 * den * g_p, 0.0)
             + jnp.where(lane == 4.0, r0, 0.0) + jnp.where(lane == 5.0, r1, 0.0))
    route_ref[...] = route
    meta_ref[...] = jnp.transpose(route)[:META_ROWS, :]
    cnt_ref[0] = jnp.broadcast_to(onehot.sum(axis=0, keepdims=True), (META_ROWS, LANES))


def _row_copy(src, dst, sem):
    return pltpu.make_async_copy(src, dst, sem)


def _dispatch_kernel(pos_ref, last_ref, nu_ref, h2p_ref, h2s_ref, xs_ref, zero_buf, sem, *, n_p, n_tok, n_tiles):
    i = pl.program_id(0)
    tm = h2p_ref.shape[0]

    @pl.when(i == 0)
    def _():
        zero_buf[...] = jnp.zeros_like(zero_buf)

        def zero_tile(row):
            return _row_copy(zero_buf, xs_ref.at[pl.ds(pl.multiple_of(row, MOE_ROWS), MOE_ROWS)], sem)

        for e in range(N_EXPERTS):
            @pl.when(last_ref[e] >= 0)
            def _():
                zero_tile(last_ref[e]).start()

        def start_tail(j, carry):
            zero_tile(j * MOE_ROWS).start()
            return carry

        lax.fori_loop(nu_ref[0], n_tiles, start_tail, 0)
        for e in range(N_EXPERTS):
            @pl.when(last_ref[e] >= 0)
            def _():
                zero_tile(0).wait()

        def wait_tail(j, carry):
            zero_tile(0).wait()
            return carry

        lax.fori_loop(nu_ref[0], n_tiles, wait_tail, 0)

    base = i * tm

    def scatter(h2_ref):
        def issue(g, carry):
            for u in range(DMA_UNROLL):
                r = g * DMA_UNROLL + u
                for slot in range(2):
                    p = pos_ref[slot * n_tok + base + r]
                    _row_copy(h2_ref.at[pl.ds(r, 1)], xs_ref.at[pl.ds(p, 1)], sem).start()
            return carry

        lax.fori_loop(0, tm // DMA_UNROLL, issue, 0)
        for slot in range(2):
            _row_copy(h2_ref, xs_ref.at[pl.ds(0, tm)], sem).wait()

    @pl.when(i < n_p)
    def _():
        scatter(h2p_ref)

    @pl.when(i >= n_p)
    def _():
        scatter(h2s_ref)


def _expert_kernel(te_ref, nu_ref, x_ref, w1_ref, w3_ref, w2_ref, y_ref, w1b, w3b, w2b):
    j = pl.program_id(0)

    @pl.when((j == 0) | (te_ref[j] != te_ref[jnp.maximum(j - 1, 0)]))
    def _():
        w1b[...] = w1_ref[0].astype(BF16)
        w3b[...] = w3_ref[0].astype(BF16)
        w2b[...] = w2_ref[0].astype(BF16)

    @pl.when(j < nu_ref[0])
    def _():
        t = x_ref[...].astype(BF16)
        a = _dot(t, w1b[...])
        he = (a * _sigmoid(a)) * _dot(t, w3b[...])
        y_ref[...] = _dot(he.astype(BF16), w2b[...])

    @pl.when(j >= nu_ref[0])
    def _():
        y_ref[...] = jnp.zeros_like(y_ref)


def _combine_kernel(pos_ref, x1p_ref, x1s_ref, rp_ref, rs_ref, mod_ref, ys_ref, op_ref, os_ref, ybuf, sem,
                    *, n_p, n_tok):
    i = pl.program_id(0)
    tm = ybuf.shape[1]
    base = i * tm

    def issue(g, carry):
        for u in range(DMA_UNROLL):
            r = g * DMA_UNROLL + u
            for slot in range(2):
                p = pos_ref[slot * n_tok + base + r]
                _row_copy(ys_ref.at[pl.ds(p, 1)], ybuf.at[slot, pl.ds(r, 1)], sem).start()
        return carry

    lax.fori_loop(0, tm // DMA_UNROLL, issue, 0)
    for slot in range(2):
        _row_copy(ybuf.at[slot], ybuf.at[slot], sem).wait()

    def out(x1_ref, r_ref, o_ref):
        r = r_ref[...]
        o_ref[...] = x1_ref[...] + mod_ref[0, 5:6, :] * (r[:, 2:3] * ybuf[0] + r[:, 3:4] * ybuf[1])

    @pl.when(i < n_p)
    def _():
        out(x1p_ref, rp_ref, op_ref)

    @pl.when(i >= n_p)
    def _():
        out(x1s_ref, rs_ref, os_ref)


def _layer(x3, mod, row0, latent, cache, wts, lam_init):
    b, l, d = x3.shape
    t = b * l
    x = x3.reshape(t, d)
    tm = 512
    nt = t // tm
    per_seq = l // tm if l >= tm else 1
    if latent:
        mod_row = lambda i: row0 + i // per_seq
    else:
        mod_row = lambda i: row0

    x_spec = pl.BlockSpec((tm, d), lambda i: (i, 0))
    mod_spec = pl.BlockSpec((1, 6, d), lambda i: (mod_row(i), 0, 0))
    row_spec = pl.BlockSpec((1, d), lambda i: (0, 0))
    act_spec = pl.BlockSpec((tm, d), lambda i: (i, 0))

    def full(a):
        nd = a.ndim
        return pl.BlockSpec(a.shape, lambda i: (0,) * nd)

    tmc = max(tm, l)
    assert tmc % l == 0 and t % tmc == 0
    seq_per_tile = tmc // l
    xc_spec = pl.BlockSpec((tmc, d), lambda i: (i, 0))
    modc_spec = pl.BlockSpec((1, 6, d), lambda i: (row0 + (i * seq_per_tile if latent else 0), 0, 0))
    assert not latent or seq_per_tile == 1
    yc = pl.pallas_call(
        functools.partial(_conv_kernel, seq_len=l),
        grid=(t // tmc,),
        in_specs=[xc_spec, modc_spec, row_spec, full(wts["w_conv_in"]), full(wts["conv_w"])],
        out_specs=xc_spec,
        out_shape=jax.ShapeDtypeStruct((t, d), BF16),
        compiler_params=_params("parallel"),
        name="conv_in",
    )(x, mod, wts["norm1_g"], wts["w_conv_in"], wts["conv_w"])

    qkv_in = [x, mod, wts["norm1_g"], wts["w_qkv"], wts["bd"], wts["gq"], wts["gk"]]
    qkv_specs = [x_spec, mod_spec, row_spec, full(wts["w_qkv"]), full(wts["bd"]), row_spec, row_spec]
    out_shapes = [jax.ShapeDtypeStruct((t, d), BF16)] * 3
    out_specs = [act_spec] * 3
    if latent:
        qkv_in += [wts["cos"], wts["sin"]]
        tab_spec = pl.BlockSpec((tm, LANES), lambda i: (i % per_seq, 0))
        qkv_specs += [tab_spec, tab_spec]
    else:
        out_shapes += [jax.ShapeDtypeStruct((t, d), F32)] * 2
        out_specs += [act_spec] * 2
    qkv_out = pl.pallas_call(
        functools.partial(_qkv_kernel, latent=latent),
        grid=(nt,),
        in_specs=qkv_specs,
        out_specs=out_specs,
        out_shape=out_shapes,
        compiler_params=_params("parallel"),
        name="qkv",
    )(*qkv_in)
    q, k, v = qkv_out[:3]

    tq = min(ATTN_Q_ROWS, l)
    heads_per_step = max(1, ATTN_UNITS_PER_STEP // (l // tq))
    hw = heads_per_step * V_DIM
    q3, k3, v3 = (a.reshape(b, l, d) for a in (q, k, v))
    qkv_spec = pl.BlockSpec((1, l, hw), lambda bi, h: (bi, 0, h))
    sg_spec = pl.BlockSpec((1, V_DIM), lambda bi, h: (0, 0))
    lam_spec = pl.BlockSpec(wts["lamv"].shape, lambda bi, h: (0, 0))
    attn_in = [wts["lamv"], q3, k3, v3]
    attn_specs = [lam_spec, qkv_spec, qkv_spec, qkv_spec]
    if latent:
        ck, cv = cache
        c_spec = pl.BlockSpec((1, ck.shape[1], hw), lambda bi, h: (bi, 0, h))
        attn_in += [ck, cv]
        attn_specs += [c_spec, c_spec]
    attn_in.append(wts["subln_g"])
    attn_specs.append(sg_spec)
    oa = pl.pallas_call(
        functools.partial(_attn_kernel, latent=latent, lam_init=lam_init, tq=tq),
        grid=(b, N_HEADS // heads_per_step),
        in_specs=attn_specs,
        out_specs=qkv_spec,
        out_shape=jax.ShapeDtypeStruct((b, l, d), BF16),
        compiler_params=_params("parallel", "parallel"),
        name="attn",
    )(*attn_in).reshape(t, d)

    lane_spec = pl.BlockSpec((tm, LANES), lambda i: (i, 0))
    assert tm == MOE_TOKEN_TILE
    x1, h2, route, meta, counts = pl.pallas_call(
        _merge_kernel,
        grid=(nt,),
        in_specs=[x_spec, mod_spec, row_spec, act_spec, act_spec, full(wts["w_gate"]), full(wts["w_conv_out"]),
                  full(wts["w_attn_out"]), full(wts["w_o"]), row_spec, full(wts["wr_hi"]), full(wts["wr_lo"]),
                  full(wts["b_router"]), full(wts["tri"])],
        out_specs=[x_spec, x_spec, lane_spec, pl.BlockSpec((META_ROWS, tm), lambda i: (0, i)),
                   pl.BlockSpec((1, META_ROWS, LANES), lambda i: (i, 0, 0))],
        out_shape=[jax.ShapeDtypeStruct((t, d), F32), jax.ShapeDtypeStruct((t, d), F32),
                   jax.ShapeDtypeStruct((t, LANES), F32), jax.ShapeDtypeStruct((META_ROWS, t), F32),
                   jax.ShapeDtypeStruct((nt, META_ROWS, LANES), F32)],
        compiler_params=_params("parallel"),
        name="merge",
    )(x, mod, wts["norm1_g"], yc, oa, wts["w_gate"], wts["w_conv_out"], wts["w_attn_out"], wts["w_o"],
      wts["norm2_g"], wts["wr_hi"], wts["wr_lo"], wts["b_router"], wts["tri"])

    outs = [x1, h2, route, meta, counts[:, 0, :N_EXPERTS]]
    if not latent:
        outs += [qkv_out[3], qkv_out[4]]
    return outs


def _routing_tables(meta, counts, tm, n_tiles):
    eid = meta[0:2].astype(jnp.int32)
    rank = meta[4:6].astype(jnp.int32)
    counts = counts.astype(jnp.int32)
    totals = counts.sum(axis=0)
    tiles_e = (totals + MOE_ROWS - 1) // MOE_ROWS
    tile_end = jnp.cumsum(tiles_e)
    row_start = (tile_end - tiles_e) * MOE_ROWS
    base = row_start[None, :] + jnp.cumsum(counts, axis=0) - counts
    base_tok = jnp.repeat(base.T, tm, axis=1)
    ex = jnp.arange(N_EXPERTS, dtype=jnp.int32)[None, :, None]
    pos = jnp.sum(jnp.where(eid[:, None, :] == ex, base_tok[None], 0), axis=1) + rank
    n_used = tile_end[-1]
    tile_ids = jnp.minimum(jnp.arange(n_tiles, dtype=jnp.int32), n_used - 1)
    tile_expert = jnp.sum((tile_ids[:, None] >= tile_end[None, :]).astype(jnp.int32), axis=1)
    last_tile_row = jnp.where(tiles_e > 0, (tile_end - 1) * MOE_ROWS, -1)
    return (pos.reshape(-1), tile_expert.astype(jnp.int32), n_used.astype(jnp.int32).reshape(1),
            last_tile_row.astype(jnp.int32))


def _routed_moe(streams, mod, w1, w3, w2, per_seq_tiles, tm):
    (x1p, h2p, rp, _, _), (x1s, h2s, rs, _, _) = streams
    d = x1p.shape[1]
    tp, ts = x1p.shape[0], x1s.shape[0]
    t = tp + ts
    n_tiles = 2 * t // MOE_ROWS + N_EXPERTS
    meta = jnp.concatenate([s[3] for s in streams], axis=1)
    counts = jnp.concatenate([s[4] for s in streams], axis=0)
    pos, tile_expert, n_used, last_tile_row = _routing_tables(meta, counts, tm, n_tiles)

    sorted_shape = jax.ShapeDtypeStruct((n_tiles * MOE_ROWS, d), F32)
    any_spec = pl.BlockSpec(memory_space=pl.ANY)
    n_p = tp // tm
    x_sorted = pl.pallas_call(
        functools.partial(_dispatch_kernel, n_p=n_p, n_tok=t, n_tiles=n_tiles),
        grid_spec=pltpu.PrefetchScalarGridSpec(
            num_scalar_prefetch=3, grid=(t // tm,),
            in_specs=[pl.BlockSpec((tm, d), lambda i, *_: (jnp.minimum(i, n_p - 1), 0)),
                      pl.BlockSpec((tm, d), lambda i, *_: (jnp.maximum(i - n_p, 0), 0))],
            out_specs=any_spec,
            scratch_shapes=[pltpu.VMEM((MOE_ROWS, d), F32), pltpu.SemaphoreType.DMA]),
        out_shape=sorted_shape,
        compiler_params=_params("arbitrary"),
        name="dispatch",
    )(pos, last_tile_row, n_used, h2p, h2s)

    def row_blk(j, te, nu):
        return (jnp.minimum(j, nu[0] - 1), 0)

    y_sorted = pl.pallas_call(
        _expert_kernel,
        grid_spec=pltpu.PrefetchScalarGridSpec(
            num_scalar_prefetch=2,
            grid=(n_tiles,),
            in_specs=[pl.BlockSpec((MOE_ROWS, d), row_blk),
                      pl.BlockSpec((1, d, D_EXPERT), lambda j, te, nu: (te[j], 0, 0)),
                      pl.BlockSpec((1, d, D_EXPERT), lambda j, te, nu: (te[j], 0, 0)),
                      pl.BlockSpec((1, D_EXPERT, d), lambda j, te, nu: (te[j], 0, 0))],
            out_specs=pl.BlockSpec((MOE_ROWS, d), lambda j, te, nu: (j, 0)),
            scratch_shapes=[pltpu.VMEM((d, D_EXPERT), BF16), pltpu.VMEM((d, D_EXPERT), BF16),
                            pltpu.VMEM((D_EXPERT, d), BF16)]),
        out_shape=jax.ShapeDtypeStruct((n_tiles * MOE_ROWS, d), F32),
        compiler_params=_params("arbitrary"),
        name="experts",
    )(tile_expert, n_used, x_sorted, w1, w3, w2)

    ctx_blk = lambda i, p: (jnp.minimum(i, n_p - 1), 0)
    lat_blk = lambda i, p: (jnp.maximum(i - n_p, 0), 0)
    return pl.pallas_call(
        functools.partial(_combine_kernel, n_p=n_p, n_tok=t),
        grid_spec=pltpu.PrefetchScalarGridSpec(
            num_scalar_prefetch=1, grid=(t // tm,),
            in_specs=[pl.BlockSpec((tm, d), ctx_blk), pl.BlockSpec((tm, d), lat_blk),
                      pl.BlockSpec((tm, LANES), ctx_blk), pl.BlockSpec((tm, LANES), lat_blk),
                      pl.BlockSpec((1, 6, d),
                                   lambda i, p: (jnp.where(i < n_p, 0, 1 + (i - n_p) // per_seq_tiles), 0, 0)),
                      any_spec],
            out_specs=[pl.BlockSpec((tm, d), ctx_blk), pl.BlockSpec((tm, d), lat_blk)],
            scratch_shapes=[pltpu.VMEM((2, tm, d), F32), pltpu.SemaphoreType.DMA]),
        out_shape=[jax.ShapeDtypeStruct((tp, d), F32), jax.ShapeDtypeStruct((ts, d), F32)],
        compiler_params=_params("arbitrary"),
        name="combine",
    )(pos, x1p, x1s, rp, rs, mod, y_sorted)


def _rope_tables(n_tok):
    rows = n_tok // GRID_W
    pos_row = jnp.repeat(jnp.arange(rows, dtype=F32), GRID_W)
    pos_col = jnp.tile(jnp.arange(GRID_W, dtype=F32), rows)
    inv_freq = 1.0 / (ROPE_BASE ** (jnp.arange(0, AXIS_DIM, 2, dtype=F32) / AXIS_DIM))
    ang = jnp.concatenate([pos_row[:, None] * inv_freq[None, :]] * 2
                          + [pos_col[:, None] * inv_freq[None, :]] * 2, axis=1)
    sign = jnp.tile(jnp.concatenate([-jnp.ones((AXIS_DIM // 2,), F32), jnp.ones((AXIS_DIM // 2,), F32)]), 2)
    cos = jnp.cos(ang)
    sin = jnp.sin(ang) * sign[None, :]
    return jnp.tile(cos, (1, LANES // QK_DIM)), jnp.tile(sin, (1, LANES // QK_DIM))


def kernel(x_prompt, x_sample, cache_k, cache_v, c, c_ctx, w_ada, b_ada, norm1_g, w_in, conv_w, w_conv_out,
           q_norm_g, k_norm_g, lambda_q1, lambda_k1, lambda_q2, lambda_k2, subln_g, w_attn_out, w_o, norm2_g,
           w_grp, b_grp, w_exp, b_exp, w1, w3, w2):
    depth = w_in.shape[0]
    assert depth == 1
    d = D_MODEL
    dec_b, dec_l = x_sample.shape[0], x_sample.shape[1]
    cos, sin = _rope_tables(dec_l)
    bd = (jnp.arange(MXU_DIM)[:, None] // QK_DIM == jnp.arange(MXU_DIM)[None, :] // QK_DIM).astype(BF16)
    tri = (jnp.arange(MOE_TOKEN_TILE)[:, None] > jnp.arange(MOE_TOKEN_TILE)[None, :]).astype(BF16)
    xp, xs = x_prompt, x_sample
    new_k, new_v = [], []
    for l in range(depth):
        lam_init = 0.8 - 0.6 * math.exp(-0.3 * l)
        mod_rows = 16
        cvec = jnp.concatenate([c_ctx[None, :], c, jnp.zeros((mod_rows - 1 - dec_b, d), F32)], axis=0)
        mod = _ada(cvec, w_ada[l], b_ada[l][None, :]).reshape(mod_rows, 6, d)

        wi = w_in[l].astype(BF16)
        w_router = jnp.concatenate([w_grp[l], w_exp[l]], axis=1)
        w_router = jnp.pad(w_router, ((0, 0), (0, LANES - w_router.shape[1])))
        wr_hi = w_router.astype(BF16)
        wr_lo = (w_router - wr_hi.astype(F32)).astype(BF16)
        b_router = jnp.pad(jnp.concatenate([b_grp[l], b_exp[l]]), (0, LANES - N_GROUPS - N_EXPERTS))[None, :]
        wts = dict(
            norm1_g=norm1_g[l][None, :], norm2_g=norm2_g[l][None, :],
            w_conv_in=wi[:, :3 * d], w_qkv=wi[:, 3 * d:6 * d], w_gate=wi[:, 6 * d:],
            conv_w=conv_w[l], w_conv_out=w_conv_out[l].astype(BF16),
            gq=jnp.tile(q_norm_g[l], d // QK_DIM)[None, :], gk=jnp.tile(k_norm_g[l], d // QK_DIM)[None, :],
            bd=bd, cos=cos, sin=sin,
            lamv=jnp.stack([lambda_q1[l], lambda_k1[l], lambda_q2[l], lambda_k2[l]]),
            subln_g=subln_g[l][None, :], w_attn_out=w_attn_out[l].astype(BF16), w_o=w_o[l].astype(BF16),
            wr_hi=wr_hi, wr_lo=wr_lo, b_router=b_router, tri=tri,
        )
        *ctx, k_ctx, v_ctx = _layer(xp, mod, 0, False, None, wts, lam_init)
        new_k.append(k_ctx.reshape(x_prompt.shape[0], x_prompt.shape[1], N_HEADS, 2, QK_DIM))
        new_v.append(v_ctx.reshape(x_prompt.shape[0], x_prompt.shape[1], N_HEADS, V_DIM))
        past = cache_k.shape[2]
        cache = (cache_k[:, l].reshape(dec_b, past, d), cache_v[:, l].reshape(dec_b, past, d))
        lat = _layer(xs, mod, 1, True, cache, wts, lam_init)
        tm = MOE_TOKEN_TILE
        yp, ys = _routed_moe((ctx, lat), mod, w1[l], w3[l], w2[l], dec_l // tm, tm)
        xp, xs = yp.reshape(x_prompt.shape), ys.reshape(x_sample.shape)
    return (xp, xs, jnp.stack(new_k, axis=1), jnp.stack(new_v, axis=1))
```

```python
import functools
import math

import jax
import jax.numpy as jnp
from jax import lax
from jax.experimental import pallas as pl
from jax.experimental.pallas import tpu as pltpu

D_MODEL = 1024
GRID_W = 64
N_HEADS = 8
QK_DIM = 64
V_DIM = 2 * QK_DIM
AXIS_DIM = QK_DIM // 2
ROPE_BASE = 10000.0
N_GROUPS = 4
EXPERTS_PER_GROUP = 4
N_EXPERTS = N_GROUPS * EXPERTS_PER_GROUP
D_EXPERT = 512
EPS = 1e-6

LANES = 128
MXU_DIM = 256
VMEM_LIMIT_BYTES = 56 * 1024 * 1024
ROUTER_LANE0 = N_GROUPS
MOE_TOKEN_TILE = 512
MOE_ROWS = 512
META_ROWS = 8
DMA_UNROLL = 512
ATTN_Q_ROWS = 256
ATTN_UNITS_PER_STEP = 8
ATTN_SCORES_AHEAD = 1

F32 = jnp.float32
BF16 = jnp.bfloat16


def _dot(a, b):
    return jnp.dot(a, b, preferred_element_type=F32)


def _dot_nt(a, b):
    return lax.dot_general(a, b, (((1,), (1,)), ((), ())), preferred_element_type=F32)


def _split_bf16(x):
    hi = x.astype(BF16)
    lo = (x - hi.astype(F32)).astype(BF16)
    return hi, lo


def _sigmoid(x):
    return 1.0 / (1.0 + jnp.exp(-x))


def _rms(x):
    return x * lax.rsqrt(jnp.mean(x * x, axis=-1, keepdims=True) + EPS)


def _params(*sem):
    return pltpu.CompilerParams(dimension_semantics=sem, vmem_limit_bytes=VMEM_LIMIT_BYTES)


def _ada_kernel(c_ref, w_ref, b_ref, o_ref):
    c = c_ref[...]
    s = c * _sigmoid(c)
    s_hi, s_lo = _split_bf16(s)
    w_hi, w_lo = _split_bf16(w_ref[...])
    o_ref[...] = _dot(s_hi, w_hi) + _dot(s_lo, w_hi) + _dot(s_hi, w_lo) + b_ref[...]


def _ada(cvec, w_ada, b_ada):
    rows, d = cvec.shape
    n = w_ada.shape[1]
    tn = 1024
    return pl.pallas_call(
        _ada_kernel,
        grid=(n // tn,),
        in_specs=[pl.BlockSpec((rows, d), lambda j: (0, 0)),
                  pl.BlockSpec((d, tn), lambda j: (0, j)),
                  pl.BlockSpec((1, tn), lambda j: (0, j))],
        out_specs=pl.BlockSpec((rows, tn), lambda j: (0, j)),
        out_shape=jax.ShapeDtypeStruct((rows, n), F32),
        compiler_params=_params("arbitrary"),
        name="ada",
    )(cvec, w_ada, b_ada)


def _norm_mod(x, g, scale, shift):
    return (_rms(x) * g) * (1.0 + scale) + shift


def _conv_kernel(x_ref, mod_ref, g1_ref, w_ref, cw_ref, o_ref, *, seq_len):
    x = x_ref[...]
    tm = x.shape[0]
    hb = _norm_mod(x, g1_ref[...], mod_ref[0, 1:2, :], mod_ref[0, 0:1, :]).astype(BF16)
    pos = lax.broadcasted_iota(jnp.int32, (tm, 1), 0) % seq_len
    first = pos == 0
    last = pos == seq_len - 1
    for c in range(D_MODEL // MXU_DIM):
        sl = slice(c * MXU_DIM, (c + 1) * MXU_DIM)
        cb, cc, cx = (_dot(hb, w_ref[:, pl.ds(j * D_MODEL + c * MXU_DIM, MXU_DIM)]) for j in range(3))
        u = cc * cx
        up = jnp.where(first, 0.0, pltpu.roll(u, 1, 0))
        dn = jnp.where(last, 0.0, pltpu.roll(u, tm - 1, 0))
        cw = cw_ref[:, sl]
        conv = cw[0:1] * up + cw[1:2] * u + cw[2:3] * dn
        o_ref[:, sl] = (cb * conv).astype(BF16)


def _chunk_norm(z, bd, g):
    sq = (z * z).astype(BF16)
    ss = jnp.concatenate(
        [_dot(sq[:, s * MXU_DIM:(s + 1) * MXU_DIM], bd) for s in range(z.shape[1] // MXU_DIM)], axis=1)
    return z * lax.rsqrt(ss * (1.0 / QK_DIM) + EPS) * g


def _rope(z, cos, sin, hi_half):
    n = z.shape[1]
    swap = jnp.where(hi_half, pltpu.roll(z, AXIS_DIM // 2, 1), pltpu.roll(z, n - AXIS_DIM // 2, 1))
    return z * cos + swap * sin


def _qkv_kernel(*refs, latent):
    if latent:
        x_ref, mod_ref, g1_ref, w_ref, bd_ref, gq_ref, gk_ref, cos_ref, sin_ref, q_ref, k_ref, v_ref = refs
    else:
        x_ref, mod_ref, g1_ref, w_ref, bd_ref, gq_ref, gk_ref, q_ref, k_ref, v_ref, kn_ref, vn_ref = refs
    hb = _norm_mod(x_ref[...], g1_ref[...], mod_ref[0, 1:2, :], mod_ref[0, 0:1, :]).astype(BF16)
    bd = bd_ref[...]
    q = _chunk_norm(_dot(hb, w_ref[:, 0:D_MODEL]), bd, gq_ref[...])
    k = _chunk_norm(_dot(hb, w_ref[:, D_MODEL:2 * D_MODEL]), bd, gk_ref[...])
    v = _dot(hb, w_ref[:, 2 * D_MODEL:3 * D_MODEL])
    if latent:
        reps = D_MODEL // LANES
        cos = jnp.concatenate([cos_ref[...]] * reps, axis=1)
        sin = jnp.concatenate([sin_ref[...]] * reps, axis=1)
        lane = lax.broadcasted_iota(jnp.int32, (1, D_MODEL), 1)
        hi_half = (lane & (AXIS_DIM // 2)) != 0
        q = _rope(q, cos, sin, hi_half)
        k_att = _rope(k, cos, sin, hi_half)
    else:
        k_att = k
        kn_ref[...] = k
        vn_ref[...] = v
    q_ref[...] = (q * (math.log2(math.e) / math.sqrt(QK_DIM))).astype(BF16)
    k_ref[...] = k_att.astype(BF16)
    v_ref[...] = v.astype(BF16)


def _attn_kernel(*refs, latent, lam_init, tq):
    if latent:
        lamv_ref, q_ref, k_ref, v_ref, ck_ref, cv_ref, sg_ref, o_ref = refs
    else:
        lamv_ref, q_ref, k_ref, v_ref, sg_ref, o_ref = refs
    lv = lamv_ref[...]
    lam = (jnp.exp(jnp.sum(lv[0:1] * lv[1:2], axis=1, keepdims=True))
           - jnp.exp(jnp.sum(lv[2:3] * lv[3:4], axis=1, keepdims=True)) + lam_init)
    lane = lax.broadcasted_iota(jnp.int32, (1, V_DIM), 1)
    n_q = q_ref.shape[1]
    units = [(hh, c) for hh in range(q_ref.shape[2] // V_DIM) for c in range(n_q // tq)]

    def kv(hh):
        hs = slice(hh * V_DIM, (hh + 1) * V_DIM)
        keys = [k_ref[0, :, hs]]
        vals = [v_ref[0, :, hs]]
        if latent:
            keys.append(ck_ref[0, :, hs].astype(BF16))
            vals.append(cv_ref[0, :, hs].astype(BF16))
        return keys, vals

    def scores(u):
        hh, c = units[u]
        q = q_ref[0, c * tq:(c + 1) * tq, hh * V_DIM:(hh + 1) * V_DIM]
        keys, _ = kv(hh)
        halves = (jnp.where(lane < QK_DIM, q, jnp.zeros_like(q)), jnp.where(lane >= QK_DIM, q, jnp.zeros_like(q)))
        return [[_dot_nt(qh, kk) for kk in keys] for qh in halves]

    def probs(s):
        m = s[0].max(axis=1, keepdims=True)
        for si in s[1:]:
            m = jnp.maximum(m, si.max(axis=1, keepdims=True))
        e = [jnp.exp2(si - m) for si in s]
        l = e[0].sum(axis=1, keepdims=True)
        for ei in e[1:]:
            l = l + ei.sum(axis=1, keepdims=True)
        return e, 1.0 / l

    def finish(u, s):
        hh, c = units[u]
        _, vals = kv(hh)
        e1, r1 = probs(s[0])
        e2, r2 = probs(s[1])
        r2 = r2 * lam
        o = None
        for a1, a2, vv in zip(e1, e2, vals):
            part = _dot((a1 * r1 - a2 * r2).astype(BF16), vv)
            o = part if o is None else o + part
        o_ref[0, c * tq:(c + 1) * tq, hh * V_DIM:(hh + 1) * V_DIM] = (
            _rms(o) * sg_ref[...] * (1.0 - lam_init)).astype(BF16)

    ahead = ATTN_SCORES_AHEAD
    pending = [scores(u) for u in range(min(ahead, len(units)))]
    for u in range(len(units)):
        if u + ahead < len(units):
            pending.append(scores(u + ahead))
        finish(u, pending.pop(0))


def _merge_kernel(x_ref, mod_ref, g1_ref, yc_ref, oa_ref, wg_ref, wco_ref, wao_ref, wo_ref, g2_ref,
                  wrh_ref, wrl_ref, br_ref, tri_ref, x1_ref, h2_ref, route_ref, meta_ref, cnt_ref):
    x = x_ref[...]
    hb = _norm_mod(x, g1_ref[...], mod_ref[0, 1:2, :], mod_ref[0, 0:1, :]).astype(BF16)
    y_conv = _dot(yc_ref[...], wco_ref[...])
    y_attn = _dot(oa_ref[...], wao_ref[...])
    mixed = (_sigmoid(_dot(hb, wg_ref[:, 0:D_MODEL])) * y_conv
             + _sigmoid(_dot(hb, wg_ref[:, D_MODEL:2 * D_MODEL])) * y_attn)
    x1 = x + mod_ref[0, 2:3, :] * _dot(mixed.astype(BF16), wo_ref[...])
    x1_ref[...] = x1
    h2 = _norm_mod(x1, g2_ref[...], mod_ref[0, 4:5, :], mod_ref[0, 3:4, :])
    h2_ref[...] = h2

    h_hi, h_lo = _split_bf16(h2)
    logits = _dot(h_hi, wrh_ref[...]) + _dot(h_lo, wrh_ref[...]) + _dot(h_hi, wrl_ref[...]) + br_ref[...]
    tm = x.shape[0]
    lane = lax.broadcasted_iota(jnp.int32, (tm, LANES), 1).astype(F32)
    ninf = -jnp.inf
    big = float(LANES)

    def top(vals):
        vmax = vals.max(axis=1, keepdims=True)
        idx = jnp.where(vals == vmax, lane, big).min(axis=1, keepdims=True)
        return vmax, idx

    is_grp = lane < N_GROUPS
    gmax, gidx = top(jnp.where(is_grp, logits, ninf))
    g_p = 1.0 / jnp.where(is_grp, jnp.exp(logits - gmax), 0.0).sum(axis=1, keepdims=True)
    lo = ROUTER_LANE0 + EXPERTS_PER_GROUP * gidx
    el = jnp.where((lane >= lo) & (lane < lo + EXPERTS_PER_GROUP), logits, ninf)
    v1, i1 = top(el)
    v2, i2 = top(jnp.where(lane == i1, ninf, el))
    e2 = jnp.exp(v2 - v1)
    den = 1.0 / (1.0 + e2)
    e0 = i1 - ROUTER_LANE0
    e1 = i2 - ROUTER_LANE0
    onehot = jnp.where((lane == e0) | (lane == e1), 1.0, 0.0)
    before = _dot(tri_ref[...], onehot.astype(BF16))
    r0 = jnp.where(lane == e0, before, 0.0).sum(axis=1, keepdims=True)
    r1 = jnp.where(lane == e1, before, 0.0).sum(axis=1, keepdims=True)
    route = (jnp.where(lane == 0.0, e0, 0.0) + jnp.where(lane == 1.0, e1, 0.0)
             + jnp.where(lane == 2.0, den * g_p, 0.0) + jnp.where(lane == 3.0, e2 * den * g_p, 0.0)
             + jnp.where(lane == 4.0, r0, 0.0) + jnp.where(lane == 5.0, r1, 0.0))
    route_ref[...] = route
    meta_ref[...] = jnp.transpose(route)[:META_ROWS, :]
    cnt_ref[0] = jnp.broadcast_to(onehot.sum(axis=0, keepdims=True), (META_ROWS, LANES))


def _row_copy(src, dst, sem):
    return pltpu.make_async_copy(src, dst, sem)


def _for_groups(n_groups, body):
    if n_groups == 1:
        body(0, 0)
    else:
        lax.fori_loop(0, n_groups, body, 0)


def _dispatch_kernel(pos_ref, last_ref, nu_ref, h2p_ref, h2s_ref, xs_ref, zero_buf, sem, *, n_p, n_tok, n_tiles):
    i = pl.program_id(0)
    tm = h2p_ref.shape[0]

    @pl.when(i == 0)
    def _():
        zero_buf[...] = jnp.zeros_like(zero_buf)

        def zero_tile(row):
            return _row_copy(zero_buf, xs_ref.at[pl.ds(pl.multiple_of(row, MOE_ROWS), MOE_ROWS)], sem)

        for e in range(N_EXPERTS):
            @pl.when(last_ref[e] >= 0)
            def _():
                zero_tile(last_ref[e]).start()

        def start_tail(j, carry):
            zero_tile(j * MOE_ROWS).start()
            return carry

        lax.fori_loop(nu_ref[0], n_tiles, start_tail, 0)
        for e in range(N_EXPERTS):
            @pl.when(last_ref[e] >= 0)
            def _():
                zero_tile(0).wait()

        def wait_tail(j, carry):
            zero_tile(0).wait()
            return carry

        lax.fori_loop(nu_ref[0], n_tiles, wait_tail, 0)

    base = i * tm

    def scatter(h2_ref):
        def issue(g, carry):
            for u in range(DMA_UNROLL):
                r = g * DMA_UNROLL + u
                for slot in range(2):
                    p = pos_ref[slot * n_tok + base + r]
                    _row_copy(h2_ref.at[pl.ds(r, 1)], xs_ref.at[pl.ds(p, 1)], sem).start()
            return carry

        _for_groups(tm // DMA_UNROLL, issue)
        for slot in range(2):
            _row_copy(h2_ref, xs_ref.at[pl.ds(0, tm)], sem).wait()

    @pl.when(i < n_p)
    def _():
        scatter(h2p_ref)

    @pl.when(i >= n_p)
    def _():
        scatter(h2s_ref)


def _expert_kernel(te_ref, nu_ref, x_ref, w1_ref, w3_ref, w2_ref, y_ref, w1b, w3b, w2b):
    j = pl.program_id(0)

    @pl.when((j == 0) | (te_ref[j] != te_ref[jnp.maximum(j - 1, 0)]))
    def _():
        w1b[...] = w1_ref[0].astype(BF16)
        w3b[...] = w3_ref[0].astype(BF16)
        w2b[...] = w2_ref[0].astype(BF16)

    @pl.when(j < nu_ref[0])
    def _():
        t = x_ref[...].astype(BF16)
        a = _dot(t, w1b[...])
        he = (a * _sigmoid(a)) * _dot(t, w3b[...])
        y_ref[...] = _dot(he.astype(BF16), w2b[...])

    @pl.when(j >= nu_ref[0])
    def _():
        y_ref[...] = jnp.zeros_like(y_ref)


def _combine_kernel(pos_ref, x1p_ref, x1s_ref, rp_ref, rs_ref, mod_ref, ys_ref, op_ref, os_ref, ybuf, sem,
                    *, n_p, n_tok):
    i = pl.program_id(0)
    tm = ybuf.shape[1]
    base = i * tm

    def issue(g, carry):
        for u in range(DMA_UNROLL):
            r = g * DMA_UNROLL + u
            for slot in range(2):
                p = pos_ref[slot * n_tok + base + r]
                _row_copy(ys_ref.at[pl.ds(p, 1)], ybuf.at[slot, pl.ds(r, 1)], sem).start()
        return carry

    _for_groups(tm // DMA_UNROLL, issue)
    for slot in range(2):
        _row_copy(ybuf.at[slot], ybuf.at[slot], sem).wait()

    def out(x1_ref, r_ref, o_ref):
        r = r_ref[...]
        o_ref[...] = x1_ref[...] + mod_ref[0, 5:6, :] * (r[:, 2:3] * ybuf[0] + r[:, 3:4] * ybuf[1])

    @pl.when(i < n_p)
    def _():
        out(x1p_ref, rp_ref, op_ref)

    @pl.when(i >= n_p)
    def _():
        out(x1s_ref, rs_ref, os_ref)


def _layer(x3, mod, row0, latent, cache, wts, lam_init):
    b, l, d = x3.shape
    t = b * l
    x = x3.reshape(t, d)
    tm = 512
    nt = t // tm
    per_seq = l // tm if l >= tm else 1
    if latent:
        mod_row = lambda i: row0 + i // per_seq
    else:
        mod_row = lambda i: row0

    x_spec = pl.BlockSpec((tm, d), lambda i: (i, 0))
    mod_spec = pl.BlockSpec((1, 6, d), lambda i: (mod_row(i), 0, 0))
    row_spec = pl.BlockSpec((1, d), lambda i: (0, 0))
    act_spec = pl.BlockSpec((tm, d), lambda i: (i, 0))

    def full(a):
        nd = a.ndim
        return pl.BlockSpec(a.shape, lambda i: (0,) * nd)

    tmc = max(tm, l)
    assert tmc % l == 0 and t % tmc == 0
    seq_per_tile = tmc // l
    xc_spec = pl.BlockSpec((tmc, d), lambda i: (i, 0))
    modc_spec = pl.BlockSpec((1, 6, d), lambda i: (row0 + (i * seq_per_tile if latent else 0), 0, 0))
    assert not latent or seq_per_tile == 1
    yc = pl.pallas_call(
        functools.partial(_conv_kernel, seq_len=l),
        grid=(t // tmc,),
        in_specs=[xc_spec, modc_spec, row_spec, full(wts["w_conv_in"]), full(wts["conv_w"])],
        out_specs=xc_spec,
        out_shape=jax.ShapeDtypeStruct((t, d), BF16),
        compiler_params=_params("parallel"),
        name="conv_in",
    )(x, mod, wts["norm1_g"], wts["w_conv_in"], wts["conv_w"])

    qkv_in = [x, mod, wts["norm1_g"], wts["w_qkv"], wts["bd"], wts["gq"], wts["gk"]]
    qkv_specs = [x_spec, mod_spec, row_spec, full(wts["w_qkv"]), full(wts["bd"]), row_spec, row_spec]
    out_shapes = [jax.ShapeDtypeStruct((t, d), BF16)] * 3
    out_specs = [act_spec] * 3
    if latent:
        qkv_in += [wts["cos"], wts["sin"]]
        tab_spec = pl.BlockSpec((tm, LANES), lambda i: (i % per_seq, 0))
        qkv_specs += [tab_spec, tab_spec]
    else:
        out_shapes += [jax.ShapeDtypeStruct((t, d), F32)] * 2
        out_specs += [act_spec] * 2
    qkv_out = pl.pallas_call(
        functools.partial(_qkv_kernel, latent=latent),
        grid=(nt,),
        in_specs=qkv_specs,
        out_specs=out_specs,
        out_shape=out_shapes,
        compiler_params=_params("parallel"),
        name="qkv",
    )(*qkv_in)
    q, k, v = qkv_out[:3]

    tq = min(ATTN_Q_ROWS, l)
    heads_per_step = max(1, ATTN_UNITS_PER_STEP // (l // tq))
    hw = heads_per_step * V_DIM
    q3, k3, v3 = (a.reshape(b, l, d) for a in (q, k, v))
    qkv_spec = pl.BlockSpec((1, l, hw), lambda bi, h: (bi, 0, h))
    sg_spec = pl.BlockSpec((1, V_DIM), lambda bi, h: (0, 0))
    lam_spec = pl.BlockSpec(wts["lamv"].shape, lambda bi, h: (0, 0))
    attn_in = [wts["lamv"], q3, k3, v3]
    attn_specs = [lam_spec, qkv_spec, qkv_spec, qkv_spec]
    if latent:
        ck, cv = cache
        c_spec = pl.BlockSpec((1, ck.shape[1], hw), lambda bi, h: (bi, 0, h))
        attn_in += [ck, cv]
        attn_specs += [c_spec, c_spec]
    attn_in.append(wts["subln_g"])
    attn_specs.append(sg_spec)
    oa = pl.pallas_call(
        functools.partial(_attn_kernel, latent=latent, lam_init=lam_init, tq=tq),
        grid=(b, N_HEADS // heads_per_step),
        in_specs=attn_specs,
        out_specs=qkv_spec,
        out_shape=jax.ShapeDtypeStruct((b, l, d), BF16),
        compiler_params=_params("parallel", "parallel"),
        name="attn",
    )(*attn_in).reshape(t, d)

    lane_spec = pl.BlockSpec((tm, LANES), lambda i: (i, 0))
    assert tm == MOE_TOKEN_TILE
    x1, h2, route, meta, counts = pl.pallas_call(
        _merge_kernel,
        grid=(nt,),
        in_specs=[x_spec, mod_spec, row_spec, act_spec, act_spec, full(wts["w_gate"]), full(wts["w_conv_out"]),
                  full(wts["w_attn_out"]), full(wts["w_o"]), row_spec, full(wts["wr_hi"]), full(wts["wr_lo"]),
                  full(wts["b_router"]), full(wts["tri"])],
        out_specs=[x_spec, x_spec, lane_spec, pl.BlockSpec((META_ROWS, tm), lambda i: (0, i)),
                   pl.BlockSpec((1, META_ROWS, LANES), lambda i: (i, 0, 0))],
        out_shape=[jax.ShapeDtypeStruct((t, d), F32), jax.ShapeDtypeStruct((t, d), F32),
                   jax.ShapeDtypeStruct((t, LANES), F32), jax.ShapeDtypeStruct((META_ROWS, t), F32),
                   jax.ShapeDtypeStruct((nt, META_ROWS, LANES), F32)],
        compiler_params=_params("parallel"),
        name="merge",
    )(x, mod, wts["norm1_g"], yc, oa, wts["w_gate"], wts["w_conv_out"], wts["w_attn_out"], wts["w_o"],
      wts["norm2_g"], wts["wr_hi"], wts["wr_lo"], wts["b_router"], wts["tri"])

    outs = [x1, h2, route, meta, counts[:, 0, :N_EXPERTS]]
    if not latent:
        outs += [qkv_out[3], qkv_out[4]]
    return outs


def _routing_tables(meta, counts, tm, n_tiles):
    eid = meta[0:2].astype(jnp.int32)
    rank = meta[4:6].astype(jnp.int32)
    counts = counts.astype(jnp.int32)
    totals = counts.sum(axis=0)
    tiles_e = (totals + MOE_ROWS - 1) // MOE_ROWS
    tile_end = jnp.cumsum(tiles_e)
    row_start = (tile_end - tiles_e) * MOE_ROWS
    base = row_start[None, :] + jnp.cumsum(counts, axis=0) - counts
    base_tok = jnp.repeat(base.T, tm, axis=1)
    ex = jnp.arange(N_EXPERTS, dtype=jnp.int32)[None, :, None]
    pos = jnp.sum(jnp.where(eid[:, None, :] == ex, base_tok[None], 0), axis=1) + rank
    n_used = tile_end[-1]
    tile_ids = jnp.minimum(jnp.arange(n_tiles, dtype=jnp.int32), n_used - 1)
    tile_expert = jnp.sum((tile_ids[:, None] >= tile_end[None, :]).astype(jnp.int32), axis=1)
    last_tile_row = jnp.where(tiles_e > 0, (tile_end - 1) * MOE_ROWS, -1)
    return (pos.reshape(-1), tile_expert.astype(jnp.int32), n_used.astype(jnp.int32).reshape(1),
            last_tile_row.astype(jnp.int32))


def _routed_moe(streams, mod, w1, w3, w2, per_seq_tiles, tm):
    (x1p, h2p, rp, _, _), (x1s, h2s, rs, _, _) = streams
    d = x1p.shape[1]
    tp, ts = x1p.shape[0], x1s.shape[0]
    t = tp + ts
    n_tiles = 2 * t // MOE_ROWS + N_EXPERTS
    meta = jnp.concatenate([s[3] for s in streams], axis=1)
    counts = jnp.concatenate([s[4] for s in streams], axis=0)
    pos, tile_expert, n_used, last_tile_row = _routing_tables(meta, counts, tm, n_tiles)

    sorted_shape = jax.ShapeDtypeStruct((n_tiles * MOE_ROWS, d), F32)
    any_spec = pl.BlockSpec(memory_space=pl.ANY)
    n_p = tp // tm
    x_sorted = pl.pallas_call(
        functools.partial(_dispatch_kernel, n_p=n_p, n_tok=t, n_tiles=n_tiles),
        grid_spec=pltpu.PrefetchScalarGridSpec(
            num_scalar_prefetch=3, grid=(t // tm,),
            in_specs=[pl.BlockSpec((tm, d), lambda i, *_: (jnp.minimum(i, n_p - 1), 0)),
                      pl.BlockSpec((tm, d), lambda i, *_: (jnp.maximum(i - n_p, 0), 0))],
            out_specs=any_spec,
            scratch_shapes=[pltpu.VMEM((MOE_ROWS, d), F32), pltpu.SemaphoreType.DMA]),
        out_shape=sorted_shape,
        compiler_params=_params("arbitrary"),
        name="dispatch",
    )(pos, last_tile_row, n_used, h2p, h2s)

    def row_blk(j, te, nu):
        return (jnp.minimum(j, nu[0] - 1), 0)

    y_sorted = pl.pallas_call(
        _expert_kernel,
        grid_spec=pltpu.PrefetchScalarGridSpec(
            num_scalar_prefetch=2,
            grid=(n_tiles,),
            in_specs=[pl.BlockSpec((MOE_ROWS, d), row_blk),
                      pl.BlockSpec((1, d, D_EXPERT), lambda j, te, nu: (te[j], 0, 0)),
                      pl.BlockSpec((1, d, D_EXPERT), lambda j, te, nu: (te[j], 0, 0)),
                      pl.BlockSpec((1, D_EXPERT, d), lambda j, te, nu: (te[j], 0, 0))],
            out_specs=pl.BlockSpec((MOE_ROWS, d), lambda j, te, nu: (j, 0)),
            scratch_shapes=[pltpu.VMEM((d, D_EXPERT), BF16), pltpu.VMEM((d, D_EXPERT), BF16),
                            pltpu.VMEM((D_EXPERT, d), BF16)]),
        out_shape=jax.ShapeDtypeStruct((n_tiles * MOE_ROWS, d), F32),
        compiler_params=_params("arbitrary"),
        name="experts",
    )(tile_expert, n_used, x_sorted, w1, w3, w2)

    ctx_blk = lambda i, p: (jnp.minimum(i, n_p - 1), 0)
    lat_blk = lambda i, p: (jnp.maximum(i - n_p, 0), 0)
    return pl.pallas_call(
        functools.partial(_combine_kernel, n_p=n_p, n_tok=t),
        grid_spec=pltpu.PrefetchScalarGridSpec(
            num_scalar_prefetch=1, grid=(t // tm,),
            in_specs=[pl.BlockSpec((tm, d), ctx_blk), pl.BlockSpec((tm, d), lat_blk),
                      pl.BlockSpec((tm, LANES), ctx_blk), pl.BlockSpec((tm, LANES), lat_blk),
                      pl.BlockSpec((1, 6, d),
                                   lambda i, p: (jnp.where(i < n_p, 0, 1 + (i - n_p) // per_seq_tiles), 0, 0)),
                      any_spec],
            out_specs=[pl.BlockSpec((tm, d), ctx_blk), pl.BlockSpec((tm, d), lat_blk)],
            scratch_shapes=[pltpu.VMEM((2, tm, d), F32), pltpu.SemaphoreType.DMA]),
        out_shape=[jax.ShapeDtypeStruct((tp, d), F32), jax.ShapeDtypeStruct((ts, d), F32)],
        compiler_params=_params("arbitrary"),
        name="combine",
    )(pos, x1p, x1s, rp, rs, mod, y_sorted)


def _rope_tables(n_tok):
    rows = n_tok // GRID_W
    pos_row = jnp.repeat(jnp.arange(rows, dtype=F32), GRID_W)
    pos_col = jnp.tile(jnp.arange(GRID_W, dtype=F32), rows)
    inv_freq = 1.0 / (ROPE_BASE ** (jnp.arange(0, AXIS_DIM, 2, dtype=F32) / AXIS_DIM))
    ang = jnp.concatenate([pos_row[:, None] * inv_freq[None, :]] * 2
                          + [pos_col[:, None] * inv_freq[None, :]] * 2, axis=1)
    sign = jnp.tile(jnp.concatenate([-jnp.ones((AXIS_DIM // 2,), F32), jnp.ones((AXIS_DIM // 2,), F32)]), 2)
    cos = jnp.cos(ang)
    sin = jnp.sin(ang) * sign[None, :]
    return jnp.tile(cos, (1, LANES // QK_DIM)), jnp.tile(sin, (1, LANES // QK_DIM))


def kernel(x_prompt, x_sample, cache_k, cache_v, c, c_ctx, w_ada, b_ada, norm1_g, w_in, conv_w, w_conv_out,
           q_norm_g, k_norm_g, lambda_q1, lambda_k1, lambda_q2, lambda_k2, subln_g, w_attn_out, w_o, norm2_g,
           w_grp, b_grp, w_exp, b_exp, w1, w3, w2):
    depth = w_in.shape[0]
    assert depth == 1
    d = D_MODEL
    dec_b, dec_l = x_sample.shape[0], x_sample.shape[1]
    cos, sin = _rope_tables(dec_l)
    bd = (jnp.arange(MXU_DIM)[:, None] // QK_DIM == jnp.arange(MXU_DIM)[None, :] // QK_DIM).astype(BF16)
    tri = (jnp.arange(MOE_TOKEN_TILE)[:, None] > jnp.arange(MOE_TOKEN_TILE)[None, :]).astype(BF16)
    xp, xs = x_prompt, x_sample
    new_k, new_v = [], []
    for l in range(depth):
        lam_init = 0.8 - 0.6 * math.exp(-0.3 * l)
        mod_rows = 16
        cvec = jnp.concatenate([c_ctx[None, :], c, jnp.zeros((mod_rows - 1 - dec_b, d), F32)], axis=0)
        mod = _ada(cvec, w_ada[l], b_ada[l][None, :]).reshape(mod_rows, 6, d)

        wi = w_in[l].astype(BF16)
        w_router = jnp.concatenate([w_grp[l], w_exp[l]], axis=1)
        w_router = jnp.pad(w_router, ((0, 0), (0, LANES - w_router.shape[1])))
        wr_hi = w_router.astype(BF16)
        wr_lo = (w_router - wr_hi.astype(F32)).astype(BF16)
        b_router = jnp.pad(jnp.concatenate([b_grp[l], b_exp[l]]), (0, LANES - N_GROUPS - N_EXPERTS))[None, :]
        wts = dict(
            norm1_g=norm1_g[l][None, :], norm2_g=norm2_g[l][None, :],
            w_conv_in=wi[:, :3 * d], w_qkv=wi[:, 3 * d:6 * d], w_gate=wi[:, 6 * d:],
            conv_w=conv_w[l], w_conv_out=w_conv_out[l].astype(BF16),
            gq=jnp.tile(q_norm_g[l], d // QK_DIM)[None, :], gk=jnp.tile(k_norm_g[l], d // QK_DIM)[None, :],
            bd=bd, cos=cos, sin=sin,
            lamv=jnp.stack([lambda_q1[l], lambda_k1[l], lambda_q2[l], lambda_k2[l]]),
            subln_g=subln_g[l][None, :], w_attn_out=w_attn_out[l].astype(BF16), w_o=w_o[l].astype(BF16),
            wr_hi=wr_hi, wr_lo=wr_lo, b_router=b_router, tri=tri,
        )
        *ctx, k_ctx, v_ctx = _layer(xp, mod, 0, False, None, wts, lam_init)
        new_k.append(k_ctx.reshape(x_prompt.shape[0], x_prompt.shape[1], N_HEADS, 2, QK_DIM))
        new_v.append(v_ctx.reshape(x_prompt.shape[0], x_prompt.shape[1], N_HEADS, V_DIM))
        past = cache_k.shape[2]
        cache = (cache_k[:, l].reshape(dec_b, past, d), cache_v[:, l].reshape(dec_b, past, d))
        lat = _layer(xs, mod, 1, True, cache, wts, lam_init)
        tm = MOE_TOKEN_TILE
        yp, ys = _routed_moe((ctx, lat), mod, w1[l], w3[l], w2[l], dec_l // tm, tm)
        xp, xs = yp.reshape(x_prompt.shape), ys.reshape(x_sample.shape)
    return (xp, xs, jnp.stack(new_k, axis=1), jnp.stack(new_v, axis=1))
```

```python
import functools
import math

import jax
import jax.numpy as jnp
from jax import lax
from jax.experimental import pallas as pl
from jax.experimental.pallas import tpu as pltpu

D_MODEL = 1024
GRID_W = 64
N_HEADS = 8
QK_DIM = 64
V_DIM = 2 * QK_DIM
AXIS_DIM = QK_DIM // 2
ROPE_BASE = 10000.0
N_GROUPS = 4
EXPERTS_PER_GROUP = 4
N_EXPERTS = N_GROUPS * EXPERTS_PER_GROUP
D_EXPERT = 512
EPS = 1e-6

LANES = 128
MXU_DIM = 256
VMEM_LIMIT_BYTES = 56 * 1024 * 1024
ROUTER_LANE0 = N_GROUPS
MOE_TOKEN_TILE = 512
MOE_ROWS = 512
META_ROWS = 8
DMA_UNROLL = 512
ATTN_Q_ROWS = 256
ATTN_UNITS_PER_STEP = 8
ATTN_SCORES_AHEAD = 1

F32 = jnp.float32
BF16 = jnp.bfloat16


def _dot(a, b):
    return jnp.dot(a, b, preferred_element_type=F32)


def _dot_nt(a, b):
    return lax.dot_general(a, b, (((1,), (1,)), ((), ())), preferred_element_type=F32)


def _split_bf16(x):
    hi = x.astype(BF16)
    lo = (x - hi.astype(F32)).astype(BF16)
    return hi, lo


def _sigmoid(x):
    return 1.0 / (1.0 + jnp.exp(-x))


def _rms(x):
    return x * lax.rsqrt(jnp.mean(x * x, axis=-1, keepdims=True) + EPS)


def _params(*sem):
    return pltpu.CompilerParams(dimension_semantics=sem, vmem_limit_bytes=VMEM_LIMIT_BYTES)


def _ada_kernel(c_ref, w_ref, b_ref, o_ref):
    c = c_ref[...]
    s = c * _sigmoid(c)
    s_hi, s_lo = _split_bf16(s)
    w_hi, w_lo = _split_bf16(w_ref[...])
    o_ref[...] = _dot(s_hi, w_hi) + _dot(s_lo, w_hi) + _dot(s_hi, w_lo) + b_ref[...]


def _ada(cvec, w_ada, b_ada):
    rows, d = cvec.shape
    n = w_ada.shape[1]
    tn = 1024
    return pl.pallas_call(
        _ada_kernel,
        grid=(n // tn,),
        in_specs=[pl.BlockSpec((rows, d), lambda j: (0, 0)),
                  pl.BlockSpec((d, tn), lambda j: (0, j)),
                  pl.BlockSpec((1, tn), lambda j: (0, j))],
        out_specs=pl.BlockSpec((rows, tn), lambda j: (0, j)),
        out_shape=jax.ShapeDtypeStruct((rows, n), F32),
        compiler_params=_params("arbitrary"),
        name="ada",
    )(cvec, w_ada, b_ada)


def _norm_mod(x, g, scale, shift):
    return (_rms(x) * g) * (1.0 + scale) + shift


def _conv_kernel(x_ref, mod_ref, g1_ref, w_ref, cw_ref, o_ref, *, seq_len):
    x = x_ref[...]
    tm = x.shape[0]
    hb = _norm_mod(x, g1_ref[...], mod_ref[0, 1:2, :], mod_ref[0, 0:1, :]).astype(BF16)
    pos = lax.broadcasted_iota(jnp.int32, (tm, 1), 0) % seq_len
    first = pos == 0
    last = pos == seq_len - 1
    for c in range(D_MODEL // MXU_DIM):
        sl = slice(c * MXU_DIM, (c + 1) * MXU_DIM)
        cb, cc, cx = (_dot(hb, w_ref[:, pl.ds(j * D_MODEL + c * MXU_DIM, MXU_DIM)]) for j in range(3))
        u = cc * cx
        up = jnp.where(first, 0.0, pltpu.roll(u, 1, 0))
        dn = jnp.where(last, 0.0, pltpu.roll(u, tm - 1, 0))
        cw = cw_ref[:, sl]
        conv = cw[0:1] * up + cw[1:2] * u + cw[2:3] * dn
        o_ref[:, sl] = (cb * conv).astype(BF16)


def _chunk_norm(z, bd, g):
    sq = (z * z).astype(BF16)
    ss = jnp.concatenate(
        [_dot(sq[:, s * MXU_DIM:(s + 1) * MXU_DIM], bd) for s in range(z.shape[1] // MXU_DIM)], axis=1)
    return z * lax.rsqrt(ss * (1.0 / QK_DIM) + EPS) * g


def _rope(z, cos, sin, hi_half):
    n = z.shape[1]
    swap = jnp.where(hi_half, pltpu.roll(z, AXIS_DIM // 2, 1), pltpu.roll(z, n - AXIS_DIM // 2, 1))
    return z * cos + swap * sin


def _qkv_kernel(*refs, latent):
    if latent:
        x_ref, mod_ref, g1_ref, w_ref, bd_ref, gq_ref, gk_ref, cos_ref, sin_ref, q_ref, k_ref, v_ref = refs
    else:
        x_ref, mod_ref, g1_ref, w_ref, bd_ref, gq_ref, gk_ref, q_ref, k_ref, v_ref, kn_ref, vn_ref = refs
    hb = _norm_mod(x_ref[...], g1_ref[...], mod_ref[0, 1:2, :], mod_ref[0, 0:1, :]).astype(BF16)
    bd = bd_ref[...]
    q = _chunk_norm(_dot(hb, w_ref[:, 0:D_MODEL]), bd, gq_ref[...])
    k = _chunk_norm(_dot(hb, w_ref[:, D_MODEL:2 * D_MODEL]), bd, gk_ref[...])
    v = _dot(hb, w_ref[:, 2 * D_MODEL:3 * D_MODEL])
    if latent:
        reps = D_MODEL // LANES
        cos = jnp.concatenate([cos_ref[...]] * reps, axis=1)
        sin = jnp.concatenate([sin_ref[...]] * reps, axis=1)
        lane = lax.broadcasted_iota(jnp.int32, (1, D_MODEL), 1)
        hi_half = (lane & (AXIS_DIM // 2)) != 0
        q = _rope(q, cos, sin, hi_half)
        k_att = _rope(k, cos, sin, hi_half)
    else:
        k_att = k
        kn_ref[...] = k
        vn_ref[...] = v
    q_ref[...] = (q * (math.log2(math.e) / math.sqrt(QK_DIM))).astype(BF16)
    k_ref[...] = k_att.astype(BF16)
    v_ref[...] = v.astype(BF16)


def _attn_kernel(*refs, latent, lam_init, tq):
    if latent:
        lamv_ref, q_ref, k_ref, v_ref, ck_ref, cv_ref, sg_ref, o_ref = refs
    else:
        lamv_ref, q_ref, k_ref, v_ref, sg_ref, o_ref = refs
    lv = lamv_ref[...]
    lam = (jnp.exp(jnp.sum(lv[0:1] * lv[1:2], axis=1, keepdims=True))
           - jnp.exp(jnp.sum(lv[2:3] * lv[3:4], axis=1, keepdims=True)) + lam_init)
    lane = lax.broadcasted_iota(jnp.int32, (1, V_DIM), 1)
    n_q = q_ref.shape[1]
    units = [(hh, c) for hh in range(q_ref.shape[2] // V_DIM) for c in range(n_q // tq)]

    def kv(hh):
        hs = slice(hh * V_DIM, (hh + 1) * V_DIM)
        keys = [k_ref[0, :, hs]]
        vals = [v_ref[0, :, hs]]
        if latent:
            keys.append(ck_ref[0, :, hs].astype(BF16))
            vals.append(cv_ref[0, :, hs].astype(BF16))
        return keys, vals

    def scores(u):
        hh, c = units[u]
        q = q_ref[0, c * tq:(c + 1) * tq, hh * V_DIM:(hh + 1) * V_DIM]
        keys, _ = kv(hh)
        halves = (jnp.where(lane < QK_DIM, q, jnp.zeros_like(q)), jnp.where(lane >= QK_DIM, q, jnp.zeros_like(q)))
        return [[_dot_nt(qh, kk) for kk in keys] for qh in halves]

    def probs(s):
        m = s[0].max(axis=1, keepdims=True)
        for si in s[1:]:
            m = jnp.maximum(m, si.max(axis=1, keepdims=True))
        e = [jnp.exp2(si - m) for si in s]
        l = e[0].sum(axis=1, keepdims=True)
        for ei in e[1:]:
            l = l + ei.sum(axis=1, keepdims=True)
        return e, 1.0 / l

    def finish(u, s):
        hh, c = units[u]
        _, vals = kv(hh)
        e1, r1 = probs(s[0])
        e2, r2 = probs(s[1])
        r2 = r2 * lam
        o = None
        for a1, a2, vv in zip(e1, e2, vals):
            part = _dot((a1 * r1 - a2 * r2).astype(BF16), vv)
            o = part if o is None else o + part
        o_ref[0, c * tq:(c + 1) * tq, hh * V_DIM:(hh + 1) * V_DIM] = (
            _rms(o) * sg_ref[...] * (1.0 - lam_init)).astype(BF16)

    ahead = ATTN_SCORES_AHEAD
    pending = [scores(u) for u in range(min(ahead, len(units)))]
    for u in range(len(units)):
        if u + ahead < len(units):
            pending.append(scores(u + ahead))
        finish(u, pending.pop(0))


def _merge_kernel(x_ref, mod_ref, g1_ref, yc_ref, oa_ref, wg_ref, wco_ref, wao_ref, wo_ref, g2_ref,
                  wrh_ref, wrl_ref, br_ref, tri_ref, x1_ref, h2_ref, route_ref, meta_ref, cnt_ref):
    x = x_ref[...]
    hb = _norm_mod(x, g1_ref[...], mod_ref[0, 1:2, :], mod_ref[0, 0:1, :]).astype(BF16)
    y_conv = _dot(yc_ref[...], wco_ref[...])
    y_attn = _dot(oa_ref[...], wao_ref[...])
    mixed = (_sigmoid(_dot(hb, wg_ref[:, 0:D_MODEL])) * y_conv
             + _sigmoid(_dot(hb, wg_ref[:, D_MODEL:2 * D_MODEL])) * y_attn)
    x1 = x + mod_ref[0, 2:3, :] * _dot(mixed.astype(BF16), wo_ref[...])
    x1_ref[...] = x1
    h2 = _norm_mod(x1, g2_ref[...], mod_ref[0, 4:5, :], mod_ref[0, 3:4, :])
    h2_ref[...] = h2

    h_hi, h_lo = _split_bf16(h2)
    logits = _dot(h_hi, wrh_ref[...]) + _dot(h_lo, wrh_ref[...]) + _dot(h_hi, wrl_ref[...]) + br_ref[...]
    tm = x.shape[0]
    lane = lax.broadcasted_iota(jnp.int32, (tm, LANES), 1).astype(F32)
    ninf = -jnp.inf
    big = float(LANES)

    def top(vals):
        vmax = vals.max(axis=1, keepdims=True)
        idx = jnp.where(vals == vmax, lane, big).min(axis=1, keepdims=True)
        return vmax, idx

    is_grp = lane < N_GROUPS
    gmax, gidx = top(jnp.where(is_grp, logits, ninf))
    g_p = 1.0 / jnp.where(is_grp, jnp.exp(logits - gmax), 0.0).sum(axis=1, keepdims=True)
    lo = ROUTER_LANE0 + EXPERTS_PER_GROUP * gidx
    el = jnp.where((lane >= lo) & (lane < lo + EXPERTS_PER_GROUP), logits, ninf)
    v1, i1 = top(el)
    v2, i2 = top(jnp.where(lane == i1, ninf, el))
    e2 = jnp.exp(v2 - v1)
    den = 1.0 / (1.0 + e2)
    e0 = i1 - ROUTER_LANE0
    e1 = i2 - ROUTER_LANE0
    onehot = jnp.where((lane == e0) | (lane == e1), 1.0, 0.0)
    before = _dot(tri_ref[...], onehot.astype(BF16))
    r0 = jnp.where(lane == e0, before, 0.0).sum(axis=1, keepdims=True)
    r1 = jnp.where(lane == e1, before, 0.0).sum(axis=1, keepdims=True)
    route = (jnp.where(lane == 0.0, e0, 0.0) + jnp.where(lane == 1.0, e1, 0.0)
             + jnp.where(lane == 2.0, den * g_p, 0.0) + jnp.where(lane == 3.0, e2 * den * g_p, 0.0)
             + jnp.where(lane == 4.0, r0, 0.0) + jnp.where(lane == 5.0, r1, 0.0))
    route_ref[...] = route
    meta_ref[...] = jnp.transpose(route)[:META_ROWS, :]
    cnt_ref[0] = jnp.broadcast_to(onehot.sum(axis=0, keepdims=True), (META_ROWS, LANES))


def _row_copy(src, dst, sem):
    return pltpu.make_async_copy(src, dst, sem)


def _for_groups(n_groups, body):
    if n_groups == 1:
        body(0, 0)
    else:
        lax.fori_loop(0, n_groups, body, 0)


def _dispatch_kernel(pos_ref, last_ref, nu_ref, h2p_ref, h2s_ref, xs_ref, zero_buf, sem, *, n_p, n_tok, n_tiles):
    i = pl.program_id(0)
    tm = h2p_ref.shape[0]

    @pl.when(i == 0)
    def _():
        zero_buf[...] = jnp.zeros_like(zero_buf)

        def zero_tile(row):
            return _row_copy(zero_buf, xs_ref.at[pl.ds(pl.multiple_of(row, MOE_ROWS), MOE_ROWS)], sem)

        for e in range(N_EXPERTS):
            @pl.when(last_ref[e] >= 0)
            def _():
                zero_tile(last_ref[e]).start()

        def start_tail(j, carry):
            zero_tile(j * MOE_ROWS).start()
            return carry

        lax.fori_loop(nu_ref[0], n_tiles, start_tail, 0)
        for e in range(N_EXPERTS):
            @pl.when(last_ref[e] >= 0)
            def _():
                zero_tile(0).wait()

        def wait_tail(j, carry):
            zero_tile(0).wait()
            return carry

        lax.fori_loop(nu_ref[0], n_tiles, wait_tail, 0)

    base = i * tm

    def scatter(h2_ref):
        def issue(g, carry):
            for u in range(DMA_UNROLL):
                r = g * DMA_UNROLL + u
                for slot in range(2):
                    p = pos_ref[slot * n_tok + base + r]
                    _row_copy(h2_ref.at[pl.ds(r, 1)], xs_ref.at[pl.ds(p, 1)], sem).start(priority=slot)
            return carry

        _for_groups(tm // DMA_UNROLL, issue)
        for slot in range(2):
            _row_copy(h2_ref, xs_ref.at[pl.ds(0, tm)], sem).wait()

    @pl.when(i < n_p)
    def _():
        scatter(h2p_ref)

    @pl.when(i >= n_p)
    def _():
        scatter(h2s_ref)


def _expert_kernel(te_ref, nu_ref, x_ref, w1_ref, w3_ref, w2_ref, y_ref, w1b, w3b, w2b):
    j = pl.program_id(0)

    @pl.when((j == 0) | (te_ref[j] != te_ref[jnp.maximum(j - 1, 0)]))
    def _():
        w1b[...] = w1_ref[0].astype(BF16)
        w3b[...] = w3_ref[0].astype(BF16)
        w2b[...] = w2_ref[0].astype(BF16)

    @pl.when(j < nu_ref[0])
    def _():
        t = x_ref[...].astype(BF16)
        a = _dot(t, w1b[...])
        he = (a * _sigmoid(a)) * _dot(t, w3b[...])
        y_ref[...] = _dot(he.astype(BF16), w2b[...])

    @pl.when(j >= nu_ref[0])
    def _():
        y_ref[...] = jnp.zeros_like(y_ref)


def _combine_kernel(pos_ref, x1p_ref, x1s_ref, rp_ref, rs_ref, mod_ref, ys_ref, op_ref, os_ref, ybuf, sem,
                    *, n_p, n_tok):
    i = pl.program_id(0)
    tm = ybuf.shape[1]
    base = i * tm

    def issue(g, carry):
        for u in range(DMA_UNROLL):
            r = g * DMA_UNROLL + u
            for slot in range(2):
                p = pos_ref[slot * n_tok + base + r]
                _row_copy(ys_ref.at[pl.ds(p, 1)], ybuf.at[slot, pl.ds(r, 1)], sem).start(priority=slot)
        return carry

    _for_groups(tm // DMA_UNROLL, issue)
    for slot in range(2):
        _row_copy(ybuf.at[slot], ybuf.at[slot], sem).wait()

    def out(x1_ref, r_ref, o_ref):
        r = r_ref[...]
        o_ref[...] = x1_ref[...] + mod_ref[0, 5:6, :] * (r[:, 2:3] * ybuf[0] + r[:, 3:4] * ybuf[1])

    @pl.when(i < n_p)
    def _():
        out(x1p_ref, rp_ref, op_ref)

    @pl.when(i >= n_p)
    def _():
        out(x1s_ref, rs_ref, os_ref)


def _layer(x3, mod, row0, latent, cache, wts, lam_init):
    b, l, d = x3.shape
    t = b * l
    x = x3.reshape(t, d)
    tm = 512
    nt = t // tm
    per_seq = l // tm if l >= tm else 1
    if latent:
        mod_row = lambda i: row0 + i // per_seq
    else:
        mod_row = lambda i: row0

    x_spec = pl.BlockSpec((tm, d), lambda i: (i, 0))
    mod_spec = pl.BlockSpec((1, 6, d), lambda i: (mod_row(i), 0, 0))
    row_spec = pl.BlockSpec((1, d), lambda i: (0, 0))
    act_spec = pl.BlockSpec((tm, d), lambda i: (i, 0))

    def full(a):
        nd = a.ndim
        return pl.BlockSpec(a.shape, lambda i: (0,) * nd)

    tmc = max(tm, l)
    assert tmc % l == 0 and t % tmc == 0
    seq_per_tile = tmc // l
    xc_spec = pl.BlockSpec((tmc, d), lambda i: (i, 0))
    modc_spec = pl.BlockSpec((1, 6, d), lambda i: (row0 + (i * seq_per_tile if latent else 0), 0, 0))
    assert not latent or seq_per_tile == 1
    yc = pl.pallas_call(
        functools.partial(_conv_kernel, seq_len=l),
        grid=(t // tmc,),
        in_specs=[xc_spec, modc_spec, row_spec, full(wts["w_conv_in"]), full(wts["conv_w"])],
        out_specs=xc_spec,
        out_shape=jax.ShapeDtypeStruct((t, d), BF16),
        compiler_params=_params("parallel"),
        name="conv_in",
    )(x, mod, wts["norm1_g"], wts["w_conv_in"], wts["conv_w"])

    qkv_in = [x, mod, wts["norm1_g"], wts["w_qkv"], wts["bd"], wts["gq"], wts["gk"]]
    qkv_specs = [x_spec, mod_spec, row_spec, full(wts["w_qkv"]), full(wts["bd"]), row_spec, row_spec]
    out_shapes = [jax.ShapeDtypeStruct((t, d), BF16)] * 3
    out_specs = [act_spec] * 3
    if latent:
        qkv_in += [wts["cos"], wts["sin"]]
        tab_spec = pl.BlockSpec((tm, LANES), lambda i: (i % per_seq, 0))
        qkv_specs += [tab_spec, tab_spec]
    else:
        out_shapes += [jax.ShapeDtypeStruct((t, d), F32)] * 2
        out_specs += [act_spec] * 2
    qkv_out = pl.pallas_call(
        functools.partial(_qkv_kernel, latent=latent),
        grid=(nt,),
        in_specs=qkv_specs,
        out_specs=out_specs,
        out_shape=out_shapes,
        compiler_params=_params("parallel"),
        name="qkv",
    )(*qkv_in)
    q, k, v = qkv_out[:3]

    tq = min(ATTN_Q_ROWS, l)
    heads_per_step = max(1, ATTN_UNITS_PER_STEP // (l // tq))
    hw = heads_per_step * V_DIM
    q3, k3, v3 = (a.reshape(b, l, d) for a in (q, k, v))
    qkv_spec = pl.BlockSpec((1, l, hw), lambda bi, h: (bi, 0, h))
    sg_spec = pl.BlockSpec((1, V_DIM), lambda bi, h: (0, 0))
    lam_spec = pl.BlockSpec(wts["lamv"].shape, lambda bi, h: (0, 0))
    attn_in = [wts["lamv"], q3, k3, v3]
    attn_specs = [lam_spec, qkv_spec, qkv_spec, qkv_spec]
    if latent:
        ck, cv = cache
        c_spec = pl.BlockSpec((1, ck.shape[1], hw), lambda bi, h: (bi, 0, h))
        attn_in += [ck, cv]
        attn_specs += [c_spec, c_spec]
    attn_in.append(wts["subln_g"])
    attn_specs.append(sg_spec)
    oa = pl.pallas_call(
        functools.partial(_attn_kernel, latent=latent, lam_init=lam_init, tq=tq),
        grid=(b, N_HEADS // heads_per_step),
        in_specs=attn_specs,
        out_specs=qkv_spec,
        out_shape=jax.ShapeDtypeStruct((b, l, d), BF16),
        compiler_params=_params("parallel", "parallel"),
        name="attn",
    )(*attn_in).reshape(t, d)

    lane_spec = pl.BlockSpec((tm, LANES), lambda i: (i, 0))
    assert tm == MOE_TOKEN_TILE
    x1, h2, route, meta, counts = pl.pallas_call(
        _merge_kernel,
        grid=(nt,),
        in_specs=[x_spec, mod_spec, row_spec, act_spec, act_spec, full(wts["w_gate"]), full(wts["w_conv_out"]),
                  full(wts["w_attn_out"]), full(wts["w_o"]), row_spec, full(wts["wr_hi"]), full(wts["wr_lo"]),
                  full(wts["b_router"]), full(wts["tri"])],
        out_specs=[x_spec, x_spec, lane_spec, pl.BlockSpec((META_ROWS, tm), lambda i: (0, i)),
                   pl.BlockSpec((1, META_ROWS, LANES), lambda i: (i, 0, 0))],
        out_shape=[jax.ShapeDtypeStruct((t, d), F32), jax.ShapeDtypeStruct((t, d), F32),
                   jax.ShapeDtypeStruct((t, LANES), F32), jax.ShapeDtypeStruct((META_ROWS, t), F32),
                   jax.ShapeDtypeStruct((nt, META_ROWS, LANES), F32)],
        compiler_params=_params("parallel"),
        name="merge",
    )(x, mod, wts["norm1_g"], yc, oa, wts["w_gate"], wts["w_conv_out"], wts["w_attn_out"], wts["w_o"],
      wts["norm2_g"], wts["wr_hi"], wts["wr_lo"], wts["b_router"], wts["tri"])

    outs = [x1, h2, route, meta, counts[:, 0, :N_EXPERTS]]
    if not latent:
        outs += [qkv_out[3], qkv_out[4]]
    return outs


def _routing_tables(meta, counts, tm, n_tiles):
    eid = meta[0:2].astype(jnp.int32)
    rank = meta[4:6].astype(jnp.int32)
    counts = counts.astype(jnp.int32)
    totals = counts.sum(axis=0)
    tiles_e = (totals + MOE_ROWS - 1) // MOE_ROWS
    tile_end = jnp.cumsum(tiles_e)
    row_start = (tile_end - tiles_e) * MOE_ROWS
    base = row_start[None, :] + jnp.cumsum(counts, axis=0) - counts
    base_tok = jnp.repeat(base.T, tm, axis=1)
    ex = jnp.arange(N_EXPERTS, dtype=jnp.int32)[None, :, None]
    pos = jnp.sum(jnp.where(eid[:, None, :] == ex, base_tok[None], 0), axis=1) + rank
    n_used = tile_end[-1]
    tile_ids = jnp.minimum(jnp.arange(n_tiles, dtype=jnp.int32), n_used - 1)
    tile_expert = jnp.sum((tile_ids[:, None] >= tile_end[None, :]).astype(jnp.int32), axis=1)
    last_tile_row = jnp.where(tiles_e > 0, (tile_end - 1) * MOE_ROWS, -1)
    return (pos.reshape(-1), tile_expert.astype(jnp.int32), n_used.astype(jnp.int32).reshape(1),
            last_tile_row.astype(jnp.int32))


def _routed_moe(streams, mod, w1, w3, w2, per_seq_tiles, tm):
    (x1p, h2p, rp, _, _), (x1s, h2s, rs, _, _) = streams
    d = x1p.shape[1]
    tp, ts = x1p.shape[0], x1s.shape[0]
    t = tp + ts
    n_tiles = 2 * t // MOE_ROWS + N_EXPERTS
    meta = jnp.concatenate([s[3] for s in streams], axis=1)
    counts = jnp.concatenate([s[4] for s in streams], axis=0)
    pos, tile_expert, n_used, last_tile_row = _routing_tables(meta, counts, tm, n_tiles)

    sorted_shape = jax.ShapeDtypeStruct((n_tiles * MOE_ROWS, d), F32)
    any_spec = pl.BlockSpec(memory_space=pl.ANY)
    n_p = tp // tm
    x_sorted = pl.pallas_call(
        functools.partial(_dispatch_kernel, n_p=n_p, n_tok=t, n_tiles=n_tiles),
        grid_spec=pltpu.PrefetchScalarGridSpec(
            num_scalar_prefetch=3, grid=(t // tm,),
            in_specs=[pl.BlockSpec((tm, d), lambda i, *_: (jnp.minimum(i, n_p - 1), 0)),
                      pl.BlockSpec((tm, d), lambda i, *_: (jnp.maximum(i - n_p, 0), 0))],
            out_specs=any_spec,
            scratch_shapes=[pltpu.VMEM((MOE_ROWS, d), F32), pltpu.SemaphoreType.DMA]),
        out_shape=sorted_shape,
        compiler_params=_params("arbitrary"),
        name="dispatch",
    )(pos, last_tile_row, n_used, h2p, h2s)

    def row_blk(j, te, nu):
        return (jnp.minimum(j, nu[0] - 1), 0)

    y_sorted = pl.pallas_call(
        _expert_kernel,
        grid_spec=pltpu.PrefetchScalarGridSpec(
            num_scalar_prefetch=2,
            grid=(n_tiles,),
            in_specs=[pl.BlockSpec((MOE_ROWS, d), row_blk),
                      pl.BlockSpec((1, d, D_EXPERT), lambda j, te, nu: (te[j], 0, 0)),
                      pl.BlockSpec((1, d, D_EXPERT), lambda j, te, nu: (te[j], 0, 0)),
                      pl.BlockSpec((1, D_EXPERT, d), lambda j, te, nu: (te[j], 0, 0))],
            out_specs=pl.BlockSpec((MOE_ROWS, d), lambda j, te, nu: (j, 0)),
            scratch_shapes=[pltpu.VMEM((d, D_EXPERT), BF16), pltpu.VMEM((d, D_EXPERT), BF16),
                            pltpu.VMEM((D_EXPERT, d), BF16)]),
        out_shape=jax.ShapeDtypeStruct((n_tiles * MOE_ROWS, d), F32),
        compiler_params=_params("arbitrary"),
        name="experts",
    )(tile_expert, n_used, x_sorted, w1, w3, w2)

    ctx_blk = lambda i, p: (jnp.minimum(i, n_p - 1), 0)
    lat_blk = lambda i, p: (jnp.maximum(i - n_p, 0), 0)
    return pl.pallas_call(
        functools.partial(_combine_kernel, n_p=n_p, n_tok=t),
        grid_spec=pltpu.PrefetchScalarGridSpec(
            num_scalar_prefetch=1, grid=(t // tm,),
            in_specs=[pl.BlockSpec((tm, d), ctx_blk), pl.BlockSpec((tm, d), lat_blk),
                      pl.BlockSpec((tm, LANES), ctx_blk), pl.BlockSpec((tm, LANES), lat_blk),
                      pl.BlockSpec((1, 6, d),
                                   lambda i, p: (jnp.where(i < n_p, 0, 1 + (i - n_p) // per_seq_tiles), 0, 0)),
                      any_spec],
            out_specs=[pl.BlockSpec((tm, d), ctx_blk), pl.BlockSpec((tm, d), lat_blk)],
            scratch_shapes=[pltpu.VMEM((2, tm, d), F32), pltpu.SemaphoreType.DMA]),
        out_shape=[jax.ShapeDtypeStruct((tp, d), F32), jax.ShapeDtypeStruct((ts, d), F32)],
        compiler_params=_params("arbitrary"),
        name="combine",
    )(pos, x1p, x1s, rp, rs, mod, y_sorted)


def _rope_tables(n_tok):
    rows = n_tok // GRID_W
    pos_row = jnp.repeat(jnp.arange(rows, dtype=F32), GRID_W)
    pos_col = jnp.tile(jnp.arange(GRID_W, dtype=F32), rows)
    inv_freq = 1.0 / (ROPE_BASE ** (jnp.arange(0, AXIS_DIM, 2, dtype=F32) / AXIS_DIM))
    ang = jnp.concatenate([pos_row[:, None] * inv_freq[None, :]] * 2
                          + [pos_col[:, None] * inv_freq[None, :]] * 2, axis=1)
    sign = jnp.tile(jnp.concatenate([-jnp.ones((AXIS_DIM // 2,), F32), jnp.ones((AXIS_DIM // 2,), F32)]), 2)
    cos = jnp.cos(ang)
    sin = jnp.sin(ang) * sign[None, :]
    return jnp.tile(cos, (1, LANES // QK_DIM)), jnp.tile(sin, (1, LANES // QK_DIM))


def kernel(x_prompt, x_sample, cache_k, cache_v, c, c_ctx, w_ada, b_ada, norm1_g, w_in, conv_w, w_conv_out,
           q_norm_g, k_norm_g, lambda_q1, lambda_k1, lambda_q2, lambda_k2, subln_g, w_attn_out, w_o, norm2_g,
           w_grp, b_grp, w_exp, b_exp, w1, w3, w2):
    depth = w_in.shape[0]
    assert depth == 1
    d = D_MODEL
    dec_b, dec_l = x_sample.shape[0], x_sample.shape[1]
    cos, sin = _rope_tables(dec_l)
    bd = (jnp.arange(MXU_DIM)[:, None] // QK_DIM == jnp.arange(MXU_DIM)[None, :] // QK_DIM).astype(BF16)
    tri = (jnp.arange(MOE_TOKEN_TILE)[:, None] > jnp.arange(MOE_TOKEN_TILE)[None, :]).astype(BF16)
    xp, xs = x_prompt, x_sample
    new_k, new_v = [], []
    for l in range(depth):
        lam_init = 0.8 - 0.6 * math.exp(-0.3 * l)
        mod_rows = 16
        cvec = jnp.concatenate([c_ctx[None, :], c, jnp.zeros((mod_rows - 1 - dec_b, d), F32)], axis=0)
        mod = _ada(cvec, w_ada[l], b_ada[l][None, :]).reshape(mod_rows, 6, d)

        wi = w_in[l].astype(BF16)
        w_router = jnp.concatenate([w_grp[l], w_exp[l]], axis=1)
        w_router = jnp.pad(w_router, ((0, 0), (0, LANES - w_router.shape[1])))
        wr_hi = w_router.astype(BF16)
        wr_lo = (w_router - wr_hi.astype(F32)).astype(BF16)
        b_router = jnp.pad(jnp.concatenate([b_grp[l], b_exp[l]]), (0, LANES - N_GROUPS - N_EXPERTS))[None, :]
        wts = dict(
            norm1_g=norm1_g[l][None, :], norm2_g=norm2_g[l][None, :],
            w_conv_in=wi[:, :3 * d], w_qkv=wi[:, 3 * d:6 * d], w_gate=wi[:, 6 * d:],
            conv_w=conv_w[l], w_conv_out=w_conv_out[l].astype(BF16),
            gq=jnp.tile(q_norm_g[l], d // QK_DIM)[None, :], gk=jnp.tile(k_norm_g[l], d // QK_DIM)[None, :],
            bd=bd, cos=cos, sin=sin,
            lamv=jnp.stack([lambda_q1[l], lambda_k1[l], lambda_q2[l], lambda_k2[l]]),
            subln_g=subln_g[l][None, :], w_attn_out=w_attn_out[l].astype(BF16), w_o=w_o[l].astype(BF16),
            wr_hi=wr_hi, wr_lo=wr_lo, b_router=b_router, tri=tri,
        )
        *ctx, k_ctx, v_ctx = _layer(xp, mod, 0, False, None, wts, lam_init)
        new_k.append(k_ctx.reshape(x_prompt.shape[0], x_prompt.shape[1], N_HEADS, 2, QK_DIM))
        new_v.append(v_ctx.reshape(x_prompt.shape[0], x_prompt.shape[1], N_HEADS, V_DIM))
        past = cache_k.shape[2]
        cache = (cache_k[:, l].reshape(dec_b, past, d), cache_v[:, l].reshape(dec_b, past, d))
        lat = _layer(xs, mod, 1, True, cache, wts, lam_init)
        tm = MOE_TOKEN_TILE
        yp, ys = _routed_moe((ctx, lat), mod, w1[l], w3[l], w2[l], dec_l // tm, tm)
        xp, xs = yp.reshape(x_prompt.shape), ys.reshape(x_sample.shape)
    return (xp, xs, jnp.stack(new_k, axis=1), jnp.stack(new_v, axis=1))
```

```python
import functools
import math

import jax
import jax.numpy as jnp
from jax import lax
from jax.experimental import pallas as pl
from jax.experimental.pallas import tpu as pltpu

D_MODEL = 1024
GRID_W = 64
N_HEADS = 8
QK_DIM = 64
V_DIM = 2 * QK_DIM
AXIS_DIM = QK_DIM // 2
ROPE_BASE = 10000.0
N_GROUPS = 4
EXPERTS_PER_GROUP = 4
N_EXPERTS = N_GROUPS * EXPERTS_PER_GROUP
D_EXPERT = 512
EPS = 1e-6

LANES = 128
MXU_DIM = 256
VMEM_LIMIT_BYTES = 56 * 1024 * 1024
ROUTER_LANE0 = N_GROUPS
MOE_TOKEN_TILE = 512
MOE_ROWS = 512
META_ROWS = 8
DMA_UNROLL = 512
ATTN_Q_ROWS = 256
ATTN_UNITS_PER_STEP = 8
ATTN_SCORES_AHEAD = 1

F32 = jnp.float32
BF16 = jnp.bfloat16


def _dot(a, b):
    return jnp.dot(a, b, preferred_element_type=F32)


def _dot_nt(a, b):
    return lax.dot_general(a, b, (((1,), (1,)), ((), ())), preferred_element_type=F32)


def _split_bf16(x):
    hi = x.astype(BF16)
    lo = (x - hi.astype(F32)).astype(BF16)
    return hi, lo


def _sigmoid(x):
    return 1.0 / (1.0 + jnp.exp(-x))


def _rms(x):
    return x * lax.rsqrt(jnp.mean(x * x, axis=-1, keepdims=True) + EPS)


def _params(*sem):
    return pltpu.CompilerParams(dimension_semantics=sem, vmem_limit_bytes=VMEM_LIMIT_BYTES)


def _ada_kernel(c_ref, w_ref, b_ref, o_ref):
    c = c_ref[...]
    s = c * _sigmoid(c)
    s_hi, s_lo = _split_bf16(s)
    w_hi, w_lo = _split_bf16(w_ref[...])
    o_ref[...] = _dot(s_hi, w_hi) + _dot(s_lo, w_hi) + _dot(s_hi, w_lo) + b_ref[...]


def _ada(cvec, w_ada, b_ada):
    rows, d = cvec.shape
    n = w_ada.shape[1]
    tn = 1024
    return pl.pallas_call(
        _ada_kernel,
        grid=(n // tn,),
        in_specs=[pl.BlockSpec((rows, d), lambda j: (0, 0)),
                  pl.BlockSpec((d, tn), lambda j: (0, j)),
                  pl.BlockSpec((1, tn), lambda j: (0, j))],
        out_specs=pl.BlockSpec((rows, tn), lambda j: (0, j)),
        out_shape=jax.ShapeDtypeStruct((rows, n), F32),
        compiler_params=_params("arbitrary"),
        name="ada",
    )(cvec, w_ada, b_ada)


def _norm_mod(x, g, scale, shift):
    return (_rms(x) * g) * (1.0 + scale) + shift


def _conv_kernel(x_ref, mod_ref, g1_ref, w_ref, cw_ref, o_ref, *, seq_len):
    x = x_ref[...]
    tm = x.shape[0]
    hb = _norm_mod(x, g1_ref[...], mod_ref[0, 1:2, :], mod_ref[0, 0:1, :]).astype(BF16)
    pos = lax.broadcasted_iota(jnp.int32, (tm, 1), 0) % seq_len
    first = pos == 0
    last = pos == seq_len - 1
    for c in range(D_MODEL // MXU_DIM):
        sl = slice(c * MXU_DIM, (c + 1) * MXU_DIM)
        cb, cc, cx = (_dot(hb, w_ref[:, pl.ds(j * D_MODEL + c * MXU_DIM, MXU_DIM)]) for j in range(3))
        u = cc * cx
        up = jnp.where(first, 0.0, pltpu.roll(u, 1, 0))
        dn = jnp.where(last, 0.0, pltpu.roll(u, tm - 1, 0))
        cw = cw_ref[:, sl]
        conv = cw[0:1] * up + cw[1:2] * u + cw[2:3] * dn
        o_ref[:, sl] = (cb * conv).astype(BF16)


def _chunk_norm(z, bd, g):
    sq = (z * z).astype(BF16)
    ss = jnp.concatenate(
        [_dot(sq[:, s * MXU_DIM:(s + 1) * MXU_DIM], bd) for s in range(z.shape[1] // MXU_DIM)], axis=1)
    return z * lax.rsqrt(ss * (1.0 / QK_DIM) + EPS) * g


def _rope(z, cos, sin, hi_half):
    n = z.shape[1]
    swap = jnp.where(hi_half, pltpu.roll(z, AXIS_DIM // 2, 1), pltpu.roll(z, n - AXIS_DIM // 2, 1))
    return z * cos + swap * sin


def _qkv_kernel(*refs, latent):
    if latent:
        x_ref, mod_ref, g1_ref, w_ref, bd_ref, gq_ref, gk_ref, cos_ref, sin_ref, q_ref, k_ref, v_ref = refs
    else:
        x_ref, mod_ref, g1_ref, w_ref, wkt_ref, bd_ref, gq_ref, gkt_ref, q_ref, kt_ref, v_ref, knt_ref, vn_ref = refs
    hb = _norm_mod(x_ref[...], g1_ref[...], mod_ref[0, 1:2, :], mod_ref[0, 0:1, :]).astype(BF16)
    tm = hb.shape[0]
    bd = bd_ref[...]
    q = _chunk_norm(_dot(hb, w_ref[:, 0:D_MODEL]), bd, gq_ref[...])
    v = _dot(hb, w_ref[:, 2 * D_MODEL:3 * D_MODEL])
    if latent:
        k = _chunk_norm(_dot(hb, w_ref[:, D_MODEL:2 * D_MODEL]), bd, gk_ref[...])
        reps = D_MODEL // LANES
        cos = jnp.concatenate([cos_ref[...]] * reps, axis=1)
        sin = jnp.concatenate([sin_ref[...]] * reps, axis=1)
        lane = lax.broadcasted_iota(jnp.int32, (1, D_MODEL), 1)
        hi_half = (lane & (AXIS_DIM // 2)) != 0
        q = _rope(q, cos, sin, hi_half)
        k_ref[...] = _rope(k, cos, sin, hi_half).astype(BF16)
    else:
        kt = _dot_nt(wkt_ref[...], hb)
        gkt = jnp.concatenate([gkt_ref[...]] * (tm // LANES), axis=1)
        slabs = []
        for s in range(D_MODEL // MXU_DIM):
            z = kt[s * MXU_DIM:(s + 1) * MXU_DIM, :]
            ss = _dot(bd, (z * z).astype(BF16))
            slabs.append(z * lax.rsqrt(ss * (1.0 / QK_DIM) + EPS))
        knt = jnp.concatenate(slabs, axis=0) * gkt
        seq = knt_ref.shape[2]
        for j in range(tm // seq):
            knt_ref[j] = knt[:, j * seq:(j + 1) * seq]
            kt_ref[j] = knt[:, j * seq:(j + 1) * seq].astype(BF16)
        vn_ref[...] = v
    q_ref[...] = (q * (math.log2(math.e) / math.sqrt(QK_DIM))).astype(BF16)
    v_ref[...] = v.astype(BF16)


def _attn_kernel(*refs, latent, lam_init, tq):
    if latent:
        lamv_ref, q_ref, k_ref, v_ref, ck_ref, cv_ref, sg_ref, o_ref = refs
    else:
        lamv_ref, q_ref, k_ref, v_ref, sg_ref, o_ref = refs
    lv = lamv_ref[...]
    lam = (jnp.exp(jnp.sum(lv[0:1] * lv[1:2], axis=1, keepdims=True))
           - jnp.exp(jnp.sum(lv[2:3] * lv[3:4], axis=1, keepdims=True)) + lam_init)
    lane = lax.broadcasted_iota(jnp.int32, (1, V_DIM), 1)
    n_q = q_ref.shape[1]
    units = [(hh, c) for hh in range(q_ref.shape[2] // V_DIM) for c in range(n_q // tq)]

    def kv(hh):
        hs = slice(hh * V_DIM, (hh + 1) * V_DIM)
        vals = [v_ref[0, :, hs]]
        if latent:
            keys = [(k_ref[0, :, hs], False), (ck_ref[0, hs, :].astype(BF16), True)]
            vals.append(cv_ref[0, :, hs].astype(BF16))
        else:
            keys = [(k_ref[0, hs, :], True)]
        return keys, vals

    def scores(u):
        hh, c = units[u]
        q = q_ref[0, c * tq:(c + 1) * tq, hh * V_DIM:(hh + 1) * V_DIM]
        keys, _ = kv(hh)
        halves = (jnp.where(lane < QK_DIM, q, jnp.zeros_like(q)), jnp.where(lane >= QK_DIM, q, jnp.zeros_like(q)))
        return [[_dot(qh, kk) if feature_major else _dot_nt(qh, kk) for kk, feature_major in keys] for qh in halves]

    def probs(s):
        m = s[0].max(axis=1, keepdims=True)
        for si in s[1:]:
            m = jnp.maximum(m, si.max(axis=1, keepdims=True))
        e = [jnp.exp2(si - m) for si in s]
        l = e[0].sum(axis=1, keepdims=True)
        for ei in e[1:]:
            l = l + ei.sum(axis=1, keepdims=True)
        return e, 1.0 / l

    def finish(u, s):
        hh, c = units[u]
        _, vals = kv(hh)
        e1, r1 = probs(s[0])
        e2, r2 = probs(s[1])
        r2 = r2 * lam
        o = None
        for a1, a2, vv in zip(e1, e2, vals):
            part = _dot((a1 * r1 - a2 * r2).astype(BF16), vv)
            o = part if o is None else o + part
        o_ref[0, c * tq:(c + 1) * tq, hh * V_DIM:(hh + 1) * V_DIM] = (
            _rms(o) * sg_ref[...] * (1.0 - lam_init)).astype(BF16)

    ahead = ATTN_SCORES_AHEAD
    pending = [scores(u) for u in range(min(ahead, len(units)))]
    for u in range(len(units)):
        if u + ahead < len(units):
            pending.append(scores(u + ahead))
        finish(u, pending.pop(0))


def _merge_kernel(x_ref, mod_ref, g1_ref, yc_ref, oa_ref, wg_ref, wco_ref, wao_ref, wo_ref, g2_ref,
                  wrh_ref, wrl_ref, br_ref, tri_ref, x1_ref, h2_ref, route_ref, meta_ref, cnt_ref):
    x = x_ref[...]
    hb = _norm_mod(x, g1_ref[...], mod_ref[0, 1:2, :], mod_ref[0, 0:1, :]).astype(BF16)
    y_conv = _dot(yc_ref[...], wco_ref[...])
    y_attn = _dot(oa_ref[...], wao_ref[...])
    mixed = (_sigmoid(_dot(hb, wg_ref[:, 0:D_MODEL])) * y_conv
             + _sigmoid(_dot(hb, wg_ref[:, D_MODEL:2 * D_MODEL])) * y_attn)
    x1 = x + mod_ref[0, 2:3, :] * _dot(mixed.astype(BF16), wo_ref[...])
    x1_ref[...] = x1
    h2 = _norm_mod(x1, g2_ref[...], mod_ref[0, 4:5, :], mod_ref[0, 3:4, :])
    h2_ref[...] = h2

    h_hi, h_lo = _split_bf16(h2)
    logits = _dot(h_hi, wrh_ref[...]) + _dot(h_lo, wrh_ref[...]) + _dot(h_hi, wrl_ref[...]) + br_ref[...]
    tm = x.shape[0]
    lane = lax.broadcasted_iota(jnp.int32, (tm, LANES), 1).astype(F32)
    ninf = -jnp.inf
    big = float(LANES)

    def top(vals):
        vmax = vals.max(axis=1, keepdims=True)
        idx = jnp.where(vals == vmax, lane, big).min(axis=1, keepdims=True)
        return vmax, idx

    is_grp = lane < N_GROUPS
    gmax, gidx = top(jnp.where(is_grp, logits, ninf))
    g_p = 1.0 / jnp.where(is_grp, jnp.exp(logits - gmax), 0.0).sum(axis=1, keepdims=True)
    lo = ROUTER_LANE0 + EXPERTS_PER_GROUP * gidx
    el = jnp.where((lane >= lo) & (lane < lo + EXPERTS_PER_GROUP), logits, ninf)
    v1, i1 = top(el)
    v2, i2 = top(jnp.where(lane == i1, ninf, el))
    e2 = jnp.exp(v2 - v1)
    den = 1.0 / (1.0 + e2)
    e0 = i1 - ROUTER_LANE0
    e1 = i2 - ROUTER_LANE0
    onehot = jnp.where((lane == e0) | (lane == e1), 1.0, 0.0)
    before = _dot(tri_ref[...], onehot.astype(BF16))
    r0 = jnp.where(lane == e0, before, 0.0).sum(axis=1, keepdims=True)
    r1 = jnp.where(lane == e1, before, 0.0).sum(axis=1, keepdims=True)
    route = (jnp.where(lane == 0.0, e0, 0.0) + jnp.where(lane == 1.0, e1, 0.0)
             + jnp.where(lane == 2.0, den * g_p, 0.0) + jnp.where(lane == 3.0, e2 * den * g_p, 0.0)
             + jnp.where(lane == 4.0, r0, 0.0) + jnp.where(lane == 5.0, r1, 0.0))
    route_ref[...] = route
    meta_ref[...] = jnp.transpose(route)[:META_ROWS, :]
    cnt_ref[0] = jnp.broadcast_to(onehot.sum(axis=0, keepdims=True), (META_ROWS, LANES))


def _row_copy(src, dst, sem):
    return pltpu.make_async_copy(src, dst, sem)


def _for_groups(n_groups, body):
    if n_groups == 1:
        body(0, 0)
    else:
        lax.fori_loop(0, n_groups, body, 0)


def _dispatch_kernel(pos_ref, last_ref, nu_ref, h2p_ref, h2s_ref, xs_ref, zero_buf, sem, *, n_p, n_tok, n_tiles):
    i = pl.program_id(0)
    tm = h2p_ref.shape[0]

    @pl.when(i == 0)
    def _():
        zero_buf[...] = jnp.zeros_like(zero_buf)

        def zero_tile(row):
            return _row_copy(zero_buf, xs_ref.at[pl.ds(pl.multiple_of(row, MOE_ROWS), MOE_ROWS)], sem)

        for e in range(N_EXPERTS):
            @pl.when(last_ref[e] >= 0)
            def _():
                zero_tile(last_ref[e]).start()

        def start_tail(j, carry):
            zero_tile(j * MOE_ROWS).start()
            return carry

        lax.fori_loop(nu_ref[0], n_tiles, start_tail, 0)
        for e in range(N_EXPERTS):
            @pl.when(last_ref[e] >= 0)
            def _():
                zero_tile(0).wait()

        def wait_tail(j, carry):
            zero_tile(0).wait()
            return carry

        lax.fori_loop(nu_ref[0], n_tiles, wait_tail, 0)

    base = i * tm

    def scatter(h2_ref):
        def issue(g, carry):
            for u in range(DMA_UNROLL):
                r = g * DMA_UNROLL + u
                for slot in range(2):
                    p = pos_ref[slot * n_tok + base + r]
                    _row_copy(h2_ref.at[pl.ds(r, 1)], xs_ref.at[pl.ds(p, 1)], sem).start(priority=slot)
            return carry

        _for_groups(tm // DMA_UNROLL, issue)
        for slot in range(2):
            _row_copy(h2_ref, xs_ref.at[pl.ds(0, tm)], sem).wait()

    @pl.when(i < n_p)
    def _():
        scatter(h2p_ref)

    @pl.when(i >= n_p)
    def _():
        scatter(h2s_ref)


def _expert_kernel(te_ref, nu_ref, x_ref, w1_ref, w3_ref, w2_ref, y_ref, w1b, w3b, w2b):
    j = pl.program_id(0)

    @pl.when((j == 0) | (te_ref[j] != te_ref[jnp.maximum(j - 1, 0)]))
    def _():
        w1b[...] = w1_ref[0].astype(BF16)
        w3b[...] = w3_ref[0].astype(BF16)
        w2b[...] = w2_ref[0].astype(BF16)

    @pl.when(j < nu_ref[0])
    def _():
        t = x_ref[...].astype(BF16)
        a = _dot(t, w1b[...])
        he = (a * _sigmoid(a)) * _dot(t, w3b[...])
        y_ref[...] = _dot(he.astype(BF16), w2b[...])

    @pl.when(j >= nu_ref[0])
    def _():
        y_ref[...] = jnp.zeros_like(y_ref)


def _combine_kernel(pos_ref, x1p_ref, x1s_ref, rp_ref, rs_ref, mod_ref, ys_ref, op_ref, os_ref, ybuf, sem,
                    *, n_p, n_tok):
    i = pl.program_id(0)
    tm = ybuf.shape[1]
    base = i * tm

    def issue(g, carry):
        for u in range(DMA_UNROLL):
            r = g * DMA_UNROLL + u
            for slot in range(2):
                p = pos_ref[slot * n_tok + base + r]
                _row_copy(ys_ref.at[pl.ds(p, 1)], ybuf.at[slot, pl.ds(r, 1)], sem).start(priority=slot)
        return carry

    _for_groups(tm // DMA_UNROLL, issue)
    for slot in range(2):
        _row_copy(ybuf.at[slot], ybuf.at[slot], sem).wait()

    def out(x1_ref, r_ref, o_ref):
        r = r_ref[...]
        o_ref[...] = x1_ref[...] + mod_ref[0, 5:6, :] * (r[:, 2:3] * ybuf[0] + r[:, 3:4] * ybuf[1])

    @pl.when(i < n_p)
    def _():
        out(x1p_ref, rp_ref, op_ref)

    @pl.when(i >= n_p)
    def _():
        out(x1s_ref, rs_ref, os_ref)


def _layer(x3, mod, row0, latent, cache, wts, lam_init):
    b, l, d = x3.shape
    t = b * l
    x = x3.reshape(t, d)
    tm = 512
    nt = t // tm
    per_seq = l // tm if l >= tm else 1
    if latent:
        mod_row = lambda i: row0 + i // per_seq
    else:
        mod_row = lambda i: row0

    x_spec = pl.BlockSpec((tm, d), lambda i: (i, 0))
    mod_spec = pl.BlockSpec((1, 6, d), lambda i: (mod_row(i), 0, 0))
    row_spec = pl.BlockSpec((1, d), lambda i: (0, 0))
    act_spec = pl.BlockSpec((tm, d), lambda i: (i, 0))

    def full(a):
        nd = a.ndim
        return pl.BlockSpec(a.shape, lambda i: (0,) * nd)

    tmc = max(tm, l)
    assert tmc % l == 0 and t % tmc == 0
    seq_per_tile = tmc // l
    xc_spec = pl.BlockSpec((tmc, d), lambda i: (i, 0))
    modc_spec = pl.BlockSpec((1, 6, d), lambda i: (row0 + (i * seq_per_tile if latent else 0), 0, 0))
    assert not latent or seq_per_tile == 1
    yc = pl.pallas_call(
        functools.partial(_conv_kernel, seq_len=l),
        grid=(t // tmc,),
        in_specs=[xc_spec, modc_spec, row_spec, full(wts["w_conv_in"]), full(wts["conv_w"])],
        out_specs=xc_spec,
        out_shape=jax.ShapeDtypeStruct((t, d), BF16),
        compiler_params=_params("parallel"),
        name="conv_in",
    )(x, mod, wts["norm1_g"], wts["w_conv_in"], wts["conv_w"])

    if latent:
        qkv_in = [x, mod, wts["norm1_g"], wts["w_qkv"], wts["bd"], wts["gq"], wts["gk"], wts["cos"], wts["sin"]]
        tab_spec = pl.BlockSpec((tm, LANES), lambda i: (i % per_seq, 0))
        qkv_specs = [x_spec, mod_spec, row_spec, full(wts["w_qkv"]), full(wts["bd"]), row_spec, row_spec,
                     tab_spec, tab_spec]
        out_shapes = [jax.ShapeDtypeStruct((t, d), BF16)] * 3
        out_specs = [act_spec] * 3
    else:
        assert tm % l == 0
        kt_spec = pl.BlockSpec((tm // l, d, l), lambda i: (i, 0, 0))
        qkv_in = [x, mod, wts["norm1_g"], wts["w_qkv"], wts["w_kt"], wts["bd"], wts["gq"], wts["gkt"]]
        qkv_specs = [x_spec, mod_spec, row_spec, full(wts["w_qkv"]), full(wts["w_kt"]), full(wts["bd"]), row_spec,
                     full(wts["gkt"])]
        out_shapes = [jax.ShapeDtypeStruct((t, d), BF16), jax.ShapeDtypeStruct((b, d, l), BF16),
                      jax.ShapeDtypeStruct((t, d), BF16), jax.ShapeDtypeStruct((b, d, l), F32),
                      jax.ShapeDtypeStruct((t, d), F32)]
        out_specs = [act_spec, kt_spec, act_spec, kt_spec, act_spec]
    qkv_out = pl.pallas_call(
        functools.partial(_qkv_kernel, latent=latent),
        grid=(nt,),
        in_specs=qkv_specs,
        out_specs=out_specs,
        out_shape=out_shapes,
        compiler_params=_params("parallel"),
        name="qkv",
    )(*qkv_in)
    q, k, v = qkv_out[:3]

    tq = min(ATTN_Q_ROWS, l)
    heads_per_step = max(1, ATTN_UNITS_PER_STEP // (l // tq))
    hw = heads_per_step * V_DIM
    q3, v3 = q.reshape(b, l, d), v.reshape(b, l, d)
    qkv_spec = pl.BlockSpec((1, l, hw), lambda bi, h: (bi, 0, h))
    kt_blk = lambda n_tok: pl.BlockSpec((1, hw, n_tok), lambda bi, h: (bi, h, 0))
    sg_spec = pl.BlockSpec((1, V_DIM), lambda bi, h: (0, 0))
    lam_spec = pl.BlockSpec(wts["lamv"].shape, lambda bi, h: (0, 0))
    if latent:
        ckt, cv = cache
        attn_in = [wts["lamv"], q3, k.reshape(b, l, d), v3, ckt, cv]
        attn_specs = [lam_spec, qkv_spec, qkv_spec, qkv_spec, kt_blk(ckt.shape[2]),
                      pl.BlockSpec((1, cv.shape[1], hw), lambda bi, h: (bi, 0, h))]
    else:
        attn_in = [wts["lamv"], q3, k, v3]
        attn_specs = [lam_spec, qkv_spec, kt_blk(l), qkv_spec]
    attn_in.append(wts["subln_g"])
    attn_specs.append(sg_spec)
    oa = pl.pallas_call(
        functools.partial(_attn_kernel, latent=latent, lam_init=lam_init, tq=tq),
        grid=(b, N_HEADS // heads_per_step),
        in_specs=attn_specs,
        out_specs=qkv_spec,
        out_shape=jax.ShapeDtypeStruct((b, l, d), BF16),
        compiler_params=_params("parallel", "parallel"),
        name="attn",
    )(*attn_in).reshape(t, d)

    lane_spec = pl.BlockSpec((tm, LANES), lambda i: (i, 0))
    assert tm == MOE_TOKEN_TILE
    x1, h2, route, meta, counts = pl.pallas_call(
        _merge_kernel,
        grid=(nt,),
        in_specs=[x_spec, mod_spec, row_spec, act_spec, act_spec, full(wts["w_gate"]), full(wts["w_conv_out"]),
                  full(wts["w_attn_out"]), full(wts["w_o"]), row_spec, full(wts["wr_hi"]), full(wts["wr_lo"]),
                  full(wts["b_router"]), full(wts["tri"])],
        out_specs=[x_spec, x_spec, lane_spec, pl.BlockSpec((META_ROWS, tm), lambda i: (0, i)),
                   pl.BlockSpec((1, META_ROWS, LANES), lambda i: (i, 0, 0))],
        out_shape=[jax.ShapeDtypeStruct((t, d), F32), jax.ShapeDtypeStruct((t, d), F32),
                   jax.ShapeDtypeStruct((t, LANES), F32), jax.ShapeDtypeStruct((META_ROWS, t), F32),
                   jax.ShapeDtypeStruct((nt, META_ROWS, LANES), F32)],
        compiler_params=_params("parallel"),
        name="merge",
    )(x, mod, wts["norm1_g"], yc, oa, wts["w_gate"], wts["w_conv_out"], wts["w_attn_out"], wts["w_o"],
      wts["norm2_g"], wts["wr_hi"], wts["wr_lo"], wts["b_router"], wts["tri"])

    outs = [x1, h2, route, meta, counts[:, 0, :N_EXPERTS]]
    if not latent:
        outs += [qkv_out[3], qkv_out[4]]
    return outs


def _routing_tables(meta, counts, tm, n_tiles):
    eid = meta[0:2].astype(jnp.int32)
    rank = meta[4:6].astype(jnp.int32)
    counts = counts.astype(jnp.int32)
    totals = counts.sum(axis=0)
    tiles_e = (totals + MOE_ROWS - 1) // MOE_ROWS
    tile_end = jnp.cumsum(tiles_e)
    row_start = (tile_end - tiles_e) * MOE_ROWS
    base = row_start[None, :] + jnp.cumsum(counts, axis=0) - counts
    base_tok = jnp.repeat(base.T, tm, axis=1)
    ex = jnp.arange(N_EXPERTS, dtype=jnp.int32)[None, :, None]
    pos = jnp.sum(jnp.where(eid[:, None, :] == ex, base_tok[None], 0), axis=1) + rank
    n_used = tile_end[-1]
    tile_ids = jnp.minimum(jnp.arange(n_tiles, dtype=jnp.int32), n_used - 1)
    tile_expert = jnp.sum((tile_ids[:, None] >= tile_end[None, :]).astype(jnp.int32), axis=1)
    last_tile_row = jnp.where(tiles_e > 0, (tile_end - 1) * MOE_ROWS, -1)
    return (pos.reshape(-1), tile_expert.astype(jnp.int32), n_used.astype(jnp.int32).reshape(1),
            last_tile_row.astype(jnp.int32))


def _routed_moe(streams, mod, w1, w3, w2, per_seq_tiles, tm):
    (x1p, h2p, rp, _, _), (x1s, h2s, rs, _, _) = streams
    d = x1p.shape[1]
    tp, ts = x1p.shape[0], x1s.shape[0]
    t = tp + ts
    n_tiles = 2 * t // MOE_ROWS + N_EXPERTS
    meta = jnp.concatenate([s[3] for s in streams], axis=1)
    counts = jnp.concatenate([s[4] for s in streams], axis=0)
    pos, tile_expert, n_used, last_tile_row = _routing_tables(meta, counts, tm, n_tiles)

    sorted_shape = jax.ShapeDtypeStruct((n_tiles * MOE_ROWS, d), F32)
    any_spec = pl.BlockSpec(memory_space=pl.ANY)
    n_p = tp // tm
    x_sorted = pl.pallas_call(
        functools.partial(_dispatch_kernel, n_p=n_p, n_tok=t, n_tiles=n_tiles),
        grid_spec=pltpu.PrefetchScalarGridSpec(
            num_scalar_prefetch=3, grid=(t // tm,),
            in_specs=[pl.BlockSpec((tm, d), lambda i, *_: (jnp.minimum(i, n_p - 1), 0)),
                      pl.BlockSpec((tm, d), lambda i, *_: (jnp.maximum(i - n_p, 0), 0))],
            out_specs=any_spec,
            scratch_shapes=[pltpu.VMEM((MOE_ROWS, d), F32), pltpu.SemaphoreType.DMA]),
        out_shape=sorted_shape,
        compiler_params=_params("arbitrary"),
        name="dispatch",
    )(pos, last_tile_row, n_used, h2p, h2s)

    def row_blk(j, te, nu):
        return (jnp.minimum(j, nu[0] - 1), 0)

    y_sorted = pl.pallas_call(
        _expert_kernel,
        grid_spec=pltpu.PrefetchScalarGridSpec(
            num_scalar_prefetch=2,
            grid=(n_tiles,),
            in_specs=[pl.BlockSpec((MOE_ROWS, d), row_blk),
                      pl.BlockSpec((1, d, D_EXPERT), lambda j, te, nu: (te[j], 0, 0)),
                      pl.BlockSpec((1, d, D_EXPERT), lambda j, te, nu: (te[j], 0, 0)),
                      pl.BlockSpec((1, D_EXPERT, d), lambda j, te, nu: (te[j], 0, 0))],
            out_specs=pl.BlockSpec((MOE_ROWS, d), lambda j, te, nu: (j, 0)),
            scratch_shapes=[pltpu.VMEM((d, D_EXPERT), BF16), pltpu.VMEM((d, D_EXPERT), BF16),
                            pltpu.VMEM((D_EXPERT, d), BF16)]),
        out_shape=jax.ShapeDtypeStruct((n_tiles * MOE_ROWS, d), F32),
        compiler_params=_params("arbitrary"),
        name="experts",
    )(tile_expert, n_used, x_sorted, w1, w3, w2)

    ctx_blk = lambda i, p: (jnp.minimum(i, n_p - 1), 0)
    lat_blk = lambda i, p: (jnp.maximum(i - n_p, 0), 0)
    return pl.pallas_call(
        functools.partial(_combine_kernel, n_p=n_p, n_tok=t),
        grid_spec=pltpu.PrefetchScalarGridSpec(
            num_scalar_prefetch=1, grid=(t // tm,),
            in_specs=[pl.BlockSpec((tm, d), ctx_blk), pl.BlockSpec((tm, d), lat_blk),
                      pl.BlockSpec((tm, LANES), ctx_blk), pl.BlockSpec((tm, LANES), lat_blk),
                      pl.BlockSpec((1, 6, d),
                                   lambda i, p: (jnp.where(i < n_p, 0, 1 + (i - n_p) // per_seq_tiles), 0, 0)),
                      any_spec],
            out_specs=[pl.BlockSpec((tm, d), ctx_blk), pl.BlockSpec((tm, d), lat_blk)],
            scratch_shapes=[pltpu.VMEM((2, tm, d), F32), pltpu.SemaphoreType.DMA]),
        out_shape=[jax.ShapeDtypeStruct((tp, d), F32), jax.ShapeDtypeStruct((ts, d), F32)],
        compiler_params=_params("arbitrary"),
        name="combine",
    )(pos, x1p, x1s, rp, rs, mod, y_sorted)


def _rope_tables(n_tok):
    rows = n_tok // GRID_W
    pos_row = jnp.repeat(jnp.arange(rows, dtype=F32), GRID_W)
    pos_col = jnp.tile(jnp.arange(GRID_W, dtype=F32), rows)
    inv_freq = 1.0 / (ROPE_BASE ** (jnp.arange(0, AXIS_DIM, 2, dtype=F32) / AXIS_DIM))
    ang = jnp.concatenate([pos_row[:, None] * inv_freq[None, :]] * 2
                          + [pos_col[:, None] * inv_freq[None, :]] * 2, axis=1)
    sign = jnp.tile(jnp.concatenate([-jnp.ones((AXIS_DIM // 2,), F32), jnp.ones((AXIS_DIM // 2,), F32)]), 2)
    cos = jnp.cos(ang)
    sin = jnp.sin(ang) * sign[None, :]
    return jnp.tile(cos, (1, LANES // QK_DIM)), jnp.tile(sin, (1, LANES // QK_DIM))


def kernel(x_prompt, x_sample, cache_k, cache_v, c, c_ctx, w_ada, b_ada, norm1_g, w_in, conv_w, w_conv_out,
           q_norm_g, k_norm_g, lambda_q1, lambda_k1, lambda_q2, lambda_k2, subln_g, w_attn_out, w_o, norm2_g,
           w_grp, b_grp, w_exp, b_exp, w1, w3, w2):
    depth = w_in.shape[0]
    assert depth == 1
    d = D_MODEL
    dec_b, dec_l = x_sample.shape[0], x_sample.shape[1]
    cos, sin = _rope_tables(dec_l)
    bd = (jnp.arange(MXU_DIM)[:, None] // QK_DIM == jnp.arange(MXU_DIM)[None, :] // QK_DIM).astype(BF16)
    tri = (jnp.arange(MOE_TOKEN_TILE)[:, None] > jnp.arange(MOE_TOKEN_TILE)[None, :]).astype(BF16)
    xp, xs = x_prompt, x_sample
    new_k, new_v = [], []
    for l in range(depth):
        lam_init = 0.8 - 0.6 * math.exp(-0.3 * l)
        mod_rows = 16
        cvec = jnp.concatenate([c_ctx[None, :], c, jnp.zeros((mod_rows - 1 - dec_b, d), F32)], axis=0)
        mod = _ada(cvec, w_ada[l], b_ada[l][None, :]).reshape(mod_rows, 6, d)

        wi = w_in[l].astype(BF16)
        w_router = jnp.concatenate([w_grp[l], w_exp[l]], axis=1)
        w_router = jnp.pad(w_router, ((0, 0), (0, LANES - w_router.shape[1])))
        wr_hi = w_router.astype(BF16)
        wr_lo = (w_router - wr_hi.astype(F32)).astype(BF16)
        b_router = jnp.pad(jnp.concatenate([b_grp[l], b_exp[l]]), (0, LANES - N_GROUPS - N_EXPERTS))[None, :]
        wts = dict(
            norm1_g=norm1_g[l][None, :], norm2_g=norm2_g[l][None, :],
            w_conv_in=wi[:, :3 * d], w_qkv=wi[:, 3 * d:6 * d], w_gate=wi[:, 6 * d:],
            conv_w=conv_w[l], w_conv_out=w_conv_out[l].astype(BF16),
            gq=jnp.tile(q_norm_g[l], d // QK_DIM)[None, :], gk=jnp.tile(k_norm_g[l], d // QK_DIM)[None, :],
            w_kt=wi[:, 4 * d:5 * d].T, gkt=jnp.broadcast_to(jnp.tile(k_norm_g[l], d // QK_DIM)[:, None], (d, LANES)),
            bd=bd, cos=cos, sin=sin,
            lamv=jnp.stack([lambda_q1[l], lambda_k1[l], lambda_q2[l], lambda_k2[l]]),
            subln_g=subln_g[l][None, :], w_attn_out=w_attn_out[l].astype(BF16), w_o=w_o[l].astype(BF16),
            wr_hi=wr_hi, wr_lo=wr_lo, b_router=b_router, tri=tri,
        )
        *ctx, k_ctx, v_ctx = _layer(xp, mod, 0, False, None, wts, lam_init)
        new_k.append(k_ctx.reshape(x_prompt.shape[0], N_HEADS, 2, QK_DIM, x_prompt.shape[1]).transpose(0, 4, 1, 2, 3))
        new_v.append(v_ctx.reshape(x_prompt.shape[0], x_prompt.shape[1], N_HEADS, V_DIM))
        past = cache_k.shape[2]
        cache = (cache_k[:, l].transpose(0, 2, 3, 4, 1).reshape(dec_b, d, past), cache_v[:, l].reshape(dec_b, past, d))
        lat = _layer(xs, mod, 1, True, cache, wts, lam_init)
        tm = MOE_TOKEN_TILE
        yp, ys = _routed_moe((ctx, lat), mod, w1[l], w3[l], w2[l], dec_l // tm, tm)
        xp, xs = yp.reshape(x_prompt.shape), ys.reshape(x_sample.shape)
    return (xp, xs, jnp.stack(new_k, axis=1), jnp.stack(new_v, axis=1))
```

```python
import functools
import math

import jax
import jax.numpy as jnp
from jax import lax
from jax.experimental import pallas as pl
from jax.experimental.pallas import tpu as pltpu

D_MODEL = 1024
GRID_W = 64
N_HEADS = 8
QK_DIM = 64
V_DIM = 2 * QK_DIM
AXIS_DIM = QK_DIM // 2
ROPE_BASE = 10000.0
N_GROUPS = 4
EXPERTS_PER_GROUP = 4
N_EXPERTS = N_GROUPS * EXPERTS_PER_GROUP
D_EXPERT = 512
EPS = 1e-6

LANES = 128
MXU_DIM = 256
VMEM_LIMIT_BYTES = 56 * 1024 * 1024
ROUTER_LANE0 = N_GROUPS
MOE_TOKEN_TILE = 512
MOE_ROWS = 512
META_ROWS = 8
DMA_UNROLL = 512
ATTN_Q_ROWS = 256
ATTN_UNITS_PER_STEP = 8
ATTN_SCORES_AHEAD = 1

F32 = jnp.float32
BF16 = jnp.bfloat16


def _dot(a, b):
    return jnp.dot(a, b, preferred_element_type=F32)


def _dot_nt(a, b):
    return lax.dot_general(a, b, (((1,), (1,)), ((), ())), preferred_element_type=F32)


def _split_bf16(x):
    hi = x.astype(BF16)
    lo = (x - hi.astype(F32)).astype(BF16)
    return hi, lo


def _sigmoid(x):
    return 1.0 / (1.0 + jnp.exp(-x))


def _rms(x):
    return x * lax.rsqrt(jnp.mean(x * x, axis=-1, keepdims=True) + EPS)


def _params(*sem):
    return pltpu.CompilerParams(dimension_semantics=sem, vmem_limit_bytes=VMEM_LIMIT_BYTES)


def _ada_kernel(c_ref, w_ref, b_ref, o_ref):
    c = c_ref[...]
    s = c * _sigmoid(c)
    s_hi, s_lo = _split_bf16(s)
    w_hi, w_lo = _split_bf16(w_ref[...])
    o_ref[...] = _dot(s_hi, w_hi) + _dot(s_lo, w_hi) + _dot(s_hi, w_lo) + b_ref[...]


def _ada(cvec, w_ada, b_ada):
    rows, d = cvec.shape
    n = w_ada.shape[1]
    tn = 1024
    return pl.pallas_call(
        _ada_kernel,
        grid=(n // tn,),
        in_specs=[pl.BlockSpec((rows, d), lambda j: (0, 0)),
                  pl.BlockSpec((d, tn), lambda j: (0, j)),
                  pl.BlockSpec((1, tn), lambda j: (0, j))],
        out_specs=pl.BlockSpec((rows, tn), lambda j: (0, j)),
        out_shape=jax.ShapeDtypeStruct((rows, n), F32),
        compiler_params=_params("arbitrary"),
        name="ada",
    )(cvec, w_ada, b_ada)


def _norm_mod(x, g, scale, shift):
    return (_rms(x) * g) * (1.0 + scale) + shift


def _conv_kernel(x_ref, mod_ref, g1_ref, w_ref, cw_ref, o_ref, *, seq_len):
    x = x_ref[...]
    tm = x.shape[0]
    hb = _norm_mod(x, g1_ref[...], mod_ref[0, 1:2, :], mod_ref[0, 0:1, :]).astype(BF16)
    pos = lax.broadcasted_iota(jnp.int32, (tm, 1), 0) % seq_len
    first = pos == 0
    last = pos == seq_len - 1
    for c in range(D_MODEL // MXU_DIM):
        sl = slice(c * MXU_DIM, (c + 1) * MXU_DIM)
        cb, cc, cx = (_dot(hb, w_ref[:, pl.ds(j * D_MODEL + c * MXU_DIM, MXU_DIM)]) for j in range(3))
        u = cc * cx
        up = jnp.where(first, 0.0, pltpu.roll(u, 1, 0))
        dn = jnp.where(last, 0.0, pltpu.roll(u, tm - 1, 0))
        cw = cw_ref[:, sl]
        conv = cw[0:1] * up + cw[1:2] * u + cw[2:3] * dn
        o_ref[:, sl] = (cb * conv).astype(BF16)


def _chunk_norm(z, bd, g):
    sq = (z * z).astype(BF16)
    ss = jnp.concatenate(
        [_dot(sq[:, s * MXU_DIM:(s + 1) * MXU_DIM], bd) for s in range(z.shape[1] // MXU_DIM)], axis=1)
    return z * lax.rsqrt(ss * (1.0 / QK_DIM) + EPS) * g


def _rope(z, cos, sin, hi_half):
    n = z.shape[1]
    swap = jnp.where(hi_half, pltpu.roll(z, AXIS_DIM // 2, 1), pltpu.roll(z, n - AXIS_DIM // 2, 1))
    return z * cos + swap * sin


def _qkv_kernel(*refs, latent):
    if latent:
        x_ref, mod_ref, g1_ref, w_ref, bd_ref, gq_ref, gk_ref, cos_ref, sin_ref, q_ref, k_ref, v_ref = refs
    else:
        x_ref, mod_ref, g1_ref, w_ref, wkt_ref, bd_ref, gq_ref, gkt_ref, q_ref, kt_ref, v_ref, knt_ref, vn_ref = refs
    hb = _norm_mod(x_ref[...], g1_ref[...], mod_ref[0, 1:2, :], mod_ref[0, 0:1, :]).astype(BF16)
    tm = hb.shape[0]
    bd = bd_ref[...]
    q = _chunk_norm(_dot(hb, w_ref[:, 0:D_MODEL]), bd, gq_ref[...])
    v = _dot(hb, w_ref[:, 2 * D_MODEL:3 * D_MODEL])
    if latent:
        k = _chunk_norm(_dot(hb, w_ref[:, D_MODEL:2 * D_MODEL]), bd, gk_ref[...])
        reps = D_MODEL // LANES
        cos = jnp.concatenate([cos_ref[...]] * reps, axis=1)
        sin = jnp.concatenate([sin_ref[...]] * reps, axis=1)
        lane = lax.broadcasted_iota(jnp.int32, (1, D_MODEL), 1)
        hi_half = (lane & (AXIS_DIM // 2)) != 0
        q = _rope(q, cos, sin, hi_half)
        k_att = _rope(k, cos, sin, hi_half)
    else:
        kt = _dot_nt(wkt_ref[...], hb)
        gkt = jnp.concatenate([gkt_ref[...]] * (tm // LANES), axis=1)
        slabs = []
        for s in range(D_MODEL // MXU_DIM):
            z = kt[s * MXU_DIM:(s + 1) * MXU_DIM, :]
            ss = _dot(bd, (z * z).astype(BF16))
            slabs.append(z * lax.rsqrt(ss * (1.0 / QK_DIM) + EPS))
        knt = jnp.concatenate(slabs, axis=0) * gkt
        seq = knt_ref.shape[2]
        for j in range(tm // seq):
            knt_ref[j] = knt[:, j * seq:(j + 1) * seq]
            kt_ref[j] = knt[:, j * seq:(j + 1) * seq].astype(BF16)
        vn_ref[...] = v
    q_ref[...] = (q * (math.log2(math.e) / math.sqrt(QK_DIM))).astype(BF16)
    if latent:
        k_ref[...] = k_att.astype(BF16)
    v_ref[...] = v.astype(BF16)


def _attn_kernel(*refs, latent, lam_init, tq):
    if latent:
        lamv_ref, q_ref, k_ref, v_ref, ck_ref, cv_ref, sg_ref, o_ref = refs
    else:
        lamv_ref, q_ref, k_ref, v_ref, sg_ref, o_ref = refs
    lv = lamv_ref[...]
    lam = (jnp.exp(jnp.sum(lv[0:1] * lv[1:2], axis=1, keepdims=True))
           - jnp.exp(jnp.sum(lv[2:3] * lv[3:4], axis=1, keepdims=True)) + lam_init)
    lane = lax.broadcasted_iota(jnp.int32, (1, V_DIM), 1)
    n_q = q_ref.shape[1]
    units = [(hh, c) for hh in range(q_ref.shape[2] // V_DIM) for c in range(n_q // tq)]

    def kv(hh):
        hs = slice(hh * V_DIM, (hh + 1) * V_DIM)
        vals = [v_ref[0, :, hs]]
        if latent:
            keys = [(k_ref[0, :, hs], False), (ck_ref[0, hs, :].astype(BF16), True)]
            vals.append(cv_ref[0, :, hs].astype(BF16))
        else:
            keys = [(k_ref[0, hs, :], True)]
        return keys, vals

    def scores(u):
        hh, c = units[u]
        q = q_ref[0, c * tq:(c + 1) * tq, hh * V_DIM:(hh + 1) * V_DIM]
        keys, _ = kv(hh)
        halves = (jnp.where(lane < QK_DIM, q, jnp.zeros_like(q)), jnp.where(lane >= QK_DIM, q, jnp.zeros_like(q)))
        return [[_dot(qh, kk) if feature_major else _dot_nt(qh, kk) for kk, feature_major in keys] for qh in halves]

    ones_col = jnp.where(lane == 0, 1.0, 0.0).astype(BF16)

    def softmax_pv(s, vals):
        m = s[0].max(axis=1, keepdims=True)
        for si in s[1:]:
            m = jnp.maximum(m, si.max(axis=1, keepdims=True))
        acc = None
        for si, vv in zip(s, vals):
            v_aug = jnp.concatenate([vv, jnp.broadcast_to(ones_col, vv.shape)], axis=1)
            part = _dot(jnp.exp2(si - m).astype(BF16), v_aug)
            acc = part if acc is None else acc + part
        return acc[:, :V_DIM] * (1.0 / acc[:, V_DIM:V_DIM + 1])

    def finish(u, s):
        hh, c = units[u]
        _, vals = kv(hh)
        o = softmax_pv(s[0], vals) - lam * softmax_pv(s[1], vals)
        o_ref[0, c * tq:(c + 1) * tq, hh * V_DIM:(hh + 1) * V_DIM] = (
            _rms(o) * sg_ref[...] * (1.0 - lam_init)).astype(BF16)

    ahead = ATTN_SCORES_AHEAD
    pending = [scores(u) for u in range(min(ahead, len(units)))]
    for u in range(len(units)):
        if u + ahead < len(units):
            pending.append(scores(u + ahead))
        finish(u, pending.pop(0))


def _merge_kernel(x_ref, mod_ref, g1_ref, yc_ref, oa_ref, wg_ref, wco_ref, wao_ref, wo_ref, g2_ref,
                  wrh_ref, wrl_ref, br_ref, tri_ref, x1_ref, h2_ref, route_ref, meta_ref, cnt_ref):
    x = x_ref[...]
    hb = _norm_mod(x, g1_ref[...], mod_ref[0, 1:2, :], mod_ref[0, 0:1, :]).astype(BF16)
    y_conv = _dot(yc_ref[...], wco_ref[...])
    y_attn = _dot(oa_ref[...], wao_ref[...])
    mixed = (_sigmoid(_dot(hb, wg_ref[:, 0:D_MODEL])) * y_conv
             + _sigmoid(_dot(hb, wg_ref[:, D_MODEL:2 * D_MODEL])) * y_attn)
    x1 = x + mod_ref[0, 2:3, :] * _dot(mixed.astype(BF16), wo_ref[...])
    x1_ref[...] = x1
    h2 = _norm_mod(x1, g2_ref[...], mod_ref[0, 4:5, :], mod_ref[0, 3:4, :])
    h2_ref[...] = h2

    h_hi, h_lo = _split_bf16(h2)
    logits = _dot(h_hi, wrh_ref[...]) + _dot(h_lo, wrh_ref[...]) + _dot(h_hi, wrl_ref[...]) + br_ref[...]
    tm = x.shape[0]
    lane = lax.broadcasted_iota(jnp.int32, (tm, LANES), 1).astype(F32)
    ninf = -jnp.inf
    big = float(LANES)

    def top(vals):
        vmax = vals.max(axis=1, keepdims=True)
        idx = jnp.where(vals == vmax, lane, big).min(axis=1, keepdims=True)
        return vmax, idx

    is_grp = lane < N_GROUPS
    gmax, gidx = top(jnp.where(is_grp, logits, ninf))
    g_p = 1.0 / jnp.where(is_grp, jnp.exp(logits - gmax), 0.0).sum(axis=1, keepdims=True)
    lo = ROUTER_LANE0 + EXPERTS_PER_GROUP * gidx
    el = jnp.where((lane >= lo) & (lane < lo + EXPERTS_PER_GROUP), logits, ninf)
    v1, i1 = top(el)
    v2, i2 = top(jnp.where(lane == i1, ninf, el))
    e2 = jnp.exp(v2 - v1)
    den = 1.0 / (1.0 + e2)
    e0 = i1 - ROUTER_LANE0
    e1 = i2 - ROUTER_LANE0
    onehot = jnp.where((lane == e0) | (lane == e1), 1.0, 0.0)
    before = _dot(tri_ref[...], onehot.astype(BF16))
    r0 = jnp.where(lane == e0, before, 0.0).sum(axis=1, keepdims=True)
    r1 = jnp.where(lane == e1, before, 0.0).sum(axis=1, keepdims=True)
    route = (jnp.where(lane == 0.0, e0, 0.0) + jnp.where(lane == 1.0, e1, 0.0)
             + jnp.where(lane == 2.0, den * g_p, 0.0) + jnp.where(lane == 3.0, e2 * den * g_p, 0.0)
             + jnp.where(lane == 4.0, r0, 0.0) + jnp.where(lane == 5.0, r1, 0.0))
    route_ref[...] = route
    meta_ref[...] = jnp.transpose(route)[:META_ROWS, :]
    cnt_ref[0] = jnp.broadcast_to(onehot.sum(axis=0, keepdims=True), (META_ROWS, LANES))


def _row_copy(src, dst, sem):
    return pltpu.make_async_copy(src, dst, sem)


def _for_groups(n_groups, body):
    if n_groups == 1:
        body(0, 0)
    else:
        lax.fori_loop(0, n_groups, body, 0)


def _dispatch_kernel(pos_ref, last_ref, nu_ref, h2p_ref, h2s_ref, xs_ref, zero_buf, sem, *, n_p, n_tok, n_tiles):
    i = pl.program_id(0)
    tm = h2p_ref.shape[0]

    @pl.when(i == 0)
    def _():
        zero_buf[...] = jnp.zeros_like(zero_buf)

        def zero_tile(row):
            return _row_copy(zero_buf, xs_ref.at[pl.ds(pl.multiple_of(row, MOE_ROWS), MOE_ROWS)], sem)

        for e in range(N_EXPERTS):
            @pl.when(last_ref[e] >= 0)
            def _():
                zero_tile(last_ref[e]).start()

        def start_tail(j, carry):
            zero_tile(j * MOE_ROWS).start()
            return carry

        lax.fori_loop(nu_ref[0], n_tiles, start_tail, 0)
        for e in range(N_EXPERTS):
            @pl.when(last_ref[e] >= 0)
            def _():
                zero_tile(0).wait()

        def wait_tail(j, carry):
            zero_tile(0).wait()
            return carry

        lax.fori_loop(nu_ref[0], n_tiles, wait_tail, 0)

    base = i * tm

    def scatter(h2_ref):
        def issue(g, carry):
            for u in range(DMA_UNROLL):
                r = g * DMA_UNROLL + u
                for slot in range(2):
                    p = pos_ref[slot * n_tok + base + r]
                    _row_copy(h2_ref.at[pl.ds(r, 1)], xs_ref.at[pl.ds(p, 1)], sem).start(priority=slot)
            return carry

        _for_groups(tm // DMA_UNROLL, issue)
        for slot in range(2):
            _row_copy(h2_ref, xs_ref.at[pl.ds(0, tm)], sem).wait()

    @pl.when(i < n_p)
    def _():
        scatter(h2p_ref)

    @pl.when(i >= n_p)
    def _():
        scatter(h2s_ref)


def _expert_kernel(te_ref, nu_ref, x_ref, w1_ref, w3_ref, w2_ref, y_ref, w1b, w3b, w2b):
    j = pl.program_id(0)

    @pl.when((j == 0) | (te_ref[j] != te_ref[jnp.maximum(j - 1, 0)]))
    def _():
        w1b[...] = w1_ref[0].astype(BF16)
        w3b[...] = w3_ref[0].astype(BF16)
        w2b[...] = w2_ref[0].astype(BF16)

    @pl.when(j < nu_ref[0])
    def _():
        t = x_ref[...].astype(BF16)
        a = _dot(t, w1b[...])
        he = (a * _sigmoid(a)) * _dot(t, w3b[...])
        y_ref[...] = _dot(he.astype(BF16), w2b[...])

    @pl.when(j >= nu_ref[0])
    def _():
        y_ref[...] = jnp.zeros_like(y_ref)


def _combine_kernel(pos_ref, x1p_ref, x1s_ref, rp_ref, rs_ref, mod_ref, ys_ref, op_ref, os_ref, ybuf, sem,
                    *, n_p, n_tok):
    i = pl.program_id(0)
    tm = ybuf.shape[1]
    base = i * tm

    def issue(g, carry):
        for u in range(DMA_UNROLL):
            r = g * DMA_UNROLL + u
            for slot in range(2):
                p = pos_ref[slot * n_tok + base + r]
                _row_copy(ys_ref.at[pl.ds(p, 1)], ybuf.at[slot, pl.ds(r, 1)], sem).start(priority=slot)
        return carry

    _for_groups(tm // DMA_UNROLL, issue)
    for slot in range(2):
        _row_copy(ybuf.at[slot], ybuf.at[slot], sem).wait()

    def out(x1_ref, r_ref, o_ref):
        r = r_ref[...]
        o_ref[...] = x1_ref[...] + mod_ref[0, 5:6, :] * (r[:, 2:3] * ybuf[0] + r[:, 3:4] * ybuf[1])

    @pl.when(i < n_p)
    def _():
        out(x1p_ref, rp_ref, op_ref)

    @pl.when(i >= n_p)
    def _():
        out(x1s_ref, rs_ref, os_ref)


def _layer(x3, mod, row0, latent, cache, wts, lam_init):
    b, l, d = x3.shape
    t = b * l
    x = x3.reshape(t, d)
    tm = 512
    nt = t // tm
    per_seq = l // tm if l >= tm else 1
    if latent:
        mod_row = lambda i: row0 + i // per_seq
    else:
        mod_row = lambda i: row0

    x_spec = pl.BlockSpec((tm, d), lambda i: (i, 0))
    mod_spec = pl.BlockSpec((1, 6, d), lambda i: (mod_row(i), 0, 0))
    row_spec = pl.BlockSpec((1, d), lambda i: (0, 0))
    act_spec = pl.BlockSpec((tm, d), lambda i: (i, 0))

    def full(a):
        nd = a.ndim
        return pl.BlockSpec(a.shape, lambda i: (0,) * nd)

    tmc = max(tm, l)
    assert tmc % l == 0 and t % tmc == 0
    seq_per_tile = tmc // l
    xc_spec = pl.BlockSpec((tmc, d), lambda i: (i, 0))
    modc_spec = pl.BlockSpec((1, 6, d), lambda i: (row0 + (i * seq_per_tile if latent else 0), 0, 0))
    assert not latent or seq_per_tile == 1
    yc = pl.pallas_call(
        functools.partial(_conv_kernel, seq_len=l),
        grid=(t // tmc,),
        in_specs=[xc_spec, modc_spec, row_spec, full(wts["w_conv_in"]), full(wts["conv_w"])],
        out_specs=xc_spec,
        out_shape=jax.ShapeDtypeStruct((t, d), BF16),
        compiler_params=_params("parallel"),
        name="conv_in",
    )(x, mod, wts["norm1_g"], wts["w_conv_in"], wts["conv_w"])

    if latent:
        qkv_in = [x, mod, wts["norm1_g"], wts["w_qkv"], wts["bd"], wts["gq"], wts["gk"], wts["cos"], wts["sin"]]
        tab_spec = pl.BlockSpec((tm, LANES), lambda i: (i % per_seq, 0))
        qkv_specs = [x_spec, mod_spec, row_spec, full(wts["w_qkv"]), full(wts["bd"]), row_spec, row_spec,
                     tab_spec, tab_spec]
        out_shapes = [jax.ShapeDtypeStruct((t, d), BF16)] * 3
        out_specs = [act_spec] * 3
    else:
        assert tm % l == 0
        kt_spec = pl.BlockSpec((tm // l, d, l), lambda i: (i, 0, 0))
        qkv_in = [x, mod, wts["norm1_g"], wts["w_qkv"], wts["w_kt"], wts["bd"], wts["gq"], wts["gkt"]]
        qkv_specs = [x_spec, mod_spec, row_spec, full(wts["w_qkv"]), full(wts["w_kt"]), full(wts["bd"]), row_spec,
                     full(wts["gkt"])]
        out_shapes = [jax.ShapeDtypeStruct((t, d), BF16), jax.ShapeDtypeStruct((b, d, l), BF16),
                      jax.ShapeDtypeStruct((t, d), BF16), jax.ShapeDtypeStruct((b, d, l), F32),
                      jax.ShapeDtypeStruct((t, d), F32)]
        out_specs = [act_spec, kt_spec, act_spec, kt_spec, act_spec]
    qkv_out = pl.pallas_call(
        functools.partial(_qkv_kernel, latent=latent),
        grid=(nt,),
        in_specs=qkv_specs,
        out_specs=out_specs,
        out_shape=out_shapes,
        compiler_params=_params("parallel"),
        name="qkv",
    )(*qkv_in)
    q, k, v = qkv_out[:3]

    tq = min(ATTN_Q_ROWS, l)
    heads_per_step = max(1, ATTN_UNITS_PER_STEP // (l // tq))
    hw = heads_per_step * V_DIM
    q3, v3 = q.reshape(b, l, d), v.reshape(b, l, d)
    qkv_spec = pl.BlockSpec((1, l, hw), lambda bi, h: (bi, 0, h))
    kt_blk = lambda n_tok: pl.BlockSpec((1, hw, n_tok), lambda bi, h: (bi, h, 0))
    sg_spec = pl.BlockSpec((1, V_DIM), lambda bi, h: (0, 0))
    lam_spec = pl.BlockSpec(wts["lamv"].shape, lambda bi, h: (0, 0))
    if latent:
        ckt, cv = cache
        attn_in = [wts["lamv"], q3, k.reshape(b, l, d), v3, ckt, cv]
        attn_specs = [lam_spec, qkv_spec, qkv_spec, qkv_spec, kt_blk(ckt.shape[2]),
                      pl.BlockSpec((1, cv.shape[1], hw), lambda bi, h: (bi, 0, h))]
    else:
        attn_in = [wts["lamv"], q3, k, v3]
        attn_specs = [lam_spec, qkv_spec, kt_blk(l), qkv_spec]
    attn_in.append(wts["subln_g"])
    attn_specs.append(sg_spec)
    oa = pl.pallas_call(
        functools.partial(_attn_kernel, latent=latent, lam_init=lam_init, tq=tq),
        grid=(b, N_HEADS // heads_per_step),
        in_specs=attn_specs,
        out_specs=qkv_spec,
        out_shape=jax.ShapeDtypeStruct((b, l, d), BF16),
        compiler_params=_params("parallel", "parallel"),
        name="attn",
    )(*attn_in).reshape(t, d)

    lane_spec = pl.BlockSpec((tm, LANES), lambda i: (i, 0))
    assert tm == MOE_TOKEN_TILE
    x1, h2, route, meta, counts = pl.pallas_call(
        _merge_kernel,
        grid=(nt,),
        in_specs=[x_spec, mod_spec, row_spec, act_spec, act_spec, full(wts["w_gate"]), full(wts["w_conv_out"]),
                  full(wts["w_attn_out"]), full(wts["w_o"]), row_spec, full(wts["wr_hi"]), full(wts["wr_lo"]),
                  full(wts["b_router"]), full(wts["tri"])],
        out_specs=[x_spec, x_spec, lane_spec, pl.BlockSpec((META_ROWS, tm), lambda i: (0, i)),
                   pl.BlockSpec((1, META_ROWS, LANES), lambda i: (i, 0, 0))],
        out_shape=[jax.ShapeDtypeStruct((t, d), F32), jax.ShapeDtypeStruct((t, d), F32),
                   jax.ShapeDtypeStruct((t, LANES), F32), jax.ShapeDtypeStruct((META_ROWS, t), F32),
                   jax.ShapeDtypeStruct((nt, META_ROWS, LANES), F32)],
        compiler_params=_params("parallel"),
        name="merge",
    )(x, mod, wts["norm1_g"], yc, oa, wts["w_gate"], wts["w_conv_out"], wts["w_attn_out"], wts["w_o"],
      wts["norm2_g"], wts["wr_hi"], wts["wr_lo"], wts["b_router"], wts["tri"])

    outs = [x1, h2, route, meta, counts[:, 0, :N_EXPERTS]]
    if not latent:
        outs += [qkv_out[3], qkv_out[4]]
    return outs


def _routing_tables(meta, counts, tm, n_tiles):
    eid = meta[0:2].astype(jnp.int32)
    rank = meta[4:6].astype(jnp.int32)
    counts = counts.astype(jnp.int32)
    totals = counts.sum(axis=0)
    tiles_e = (totals + MOE_ROWS - 1) // MOE_ROWS
    tile_end = jnp.cumsum(tiles_e)
    row_start = (tile_end - tiles_e) * MOE_ROWS
    base = row_start[None, :] + jnp.cumsum(counts, axis=0) - counts
    base_tok = jnp.repeat(base.T, tm, axis=1)
    ex = jnp.arange(N_EXPERTS, dtype=jnp.int32)[None, :, None]
    pos = jnp.sum(jnp.where(eid[:, None, :] == ex, base_tok[None], 0), axis=1) + rank
    n_used = tile_end[-1]
    tile_ids = jnp.minimum(jnp.arange(n_tiles, dtype=jnp.int32), n_used - 1)
    tile_expert = jnp.sum((tile_ids[:, None] >= tile_end[None, :]).astype(jnp.int32), axis=1)
    last_tile_row = jnp.where(tiles_e > 0, (tile_end - 1) * MOE_ROWS, -1)
    return (pos.reshape(-1), tile_expert.astype(jnp.int32), n_used.astype(jnp.int32).reshape(1),
            last_tile_row.astype(jnp.int32))


def _routed_moe(streams, mod, w1, w3, w2, per_seq_tiles, tm):
    (x1p, h2p, rp, _, _), (x1s, h2s, rs, _, _) = streams
    d = x1p.shape[1]
    tp, ts = x1p.shape[0], x1s.shape[0]
    t = tp + ts
    n_tiles = 2 * t // MOE_ROWS + N_EXPERTS
    meta = jnp.concatenate([s[3] for s in streams], axis=1)
    counts = jnp.concatenate([s[4] for s in streams], axis=0)
    pos, tile_expert, n_used, last_tile_row = _routing_tables(meta, counts, tm, n_tiles)

    sorted_shape = jax.ShapeDtypeStruct((n_tiles * MOE_ROWS, d), F32)
    any_spec = pl.BlockSpec(memory_space=pl.ANY)
    n_p = tp // tm
    x_sorted = pl.pallas_call(
        functools.partial(_dispatch_kernel, n_p=n_p, n_tok=t, n_tiles=n_tiles),
        grid_spec=pltpu.PrefetchScalarGridSpec(
            num_scalar_prefetch=3, grid=(t // tm,),
            in_specs=[pl.BlockSpec((tm, d), lambda i, *_: (jnp.minimum(i, n_p - 1), 0)),
                      pl.BlockSpec((tm, d), lambda i, *_: (jnp.maximum(i - n_p, 0), 0))],
            out_specs=any_spec,
            scratch_shapes=[pltpu.VMEM((MOE_ROWS, d), F32), pltpu.SemaphoreType.DMA]),
        out_shape=sorted_shape,
        compiler_params=_params("arbitrary"),
        name="dispatch",
    )(pos, last_tile_row, n_used, h2p, h2s)

    def row_blk(j, te, nu):
        return (jnp.minimum(j, nu[0] - 1), 0)

    y_sorted = pl.pallas_call(
        _expert_kernel,
        grid_spec=pltpu.PrefetchScalarGridSpec(
            num_scalar_prefetch=2,
            grid=(n_tiles,),
            in_specs=[pl.BlockSpec((MOE_ROWS, d), row_blk),
                      pl.BlockSpec((1, d, D_EXPERT), lambda j, te, nu: (te[j], 0, 0)),
                      pl.BlockSpec((1, d, D_EXPERT), lambda j, te, nu: (te[j], 0, 0)),
                      pl.BlockSpec((1, D_EXPERT, d), lambda j, te, nu: (te[j], 0, 0))],
            out_specs=pl.BlockSpec((MOE_ROWS, d), lambda j, te, nu: (j, 0)),
            scratch_shapes=[pltpu.VMEM((d, D_EXPERT), BF16), pltpu.VMEM((d, D_EXPERT), BF16),
                            pltpu.VMEM((D_EXPERT, d), BF16)]),
        out_shape=jax.ShapeDtypeStruct((n_tiles * MOE_ROWS, d), F32),
        compiler_params=_params("arbitrary"),
        name="experts",
    )(tile_expert, n_used, x_sorted, w1, w3, w2)

    ctx_blk = lambda i, p: (jnp.minimum(i, n_p - 1), 0)
    lat_blk = lambda i, p: (jnp.maximum(i - n_p, 0), 0)
    return pl.pallas_call(
        functools.partial(_combine_kernel, n_p=n_p, n_tok=t),
        grid_spec=pltpu.PrefetchScalarGridSpec(
            num_scalar_prefetch=1, grid=(t // tm,),
            in_specs=[pl.BlockSpec((tm, d), ctx_blk), pl.BlockSpec((tm, d), lat_blk),
                      pl.BlockSpec((tm, LANES), ctx_blk), pl.BlockSpec((tm, LANES), lat_blk),
                      pl.BlockSpec((1, 6, d),
                                   lambda i, p: (jnp.where(i < n_p, 0, 1 + (i - n_p) // per_seq_tiles), 0, 0)),
                      any_spec],
            out_specs=[pl.BlockSpec((tm, d), ctx_blk), pl.BlockSpec((tm, d), lat_blk)],
            scratch_shapes=[pltpu.VMEM((2, tm, d), F32), pltpu.SemaphoreType.DMA]),
        out_shape=[jax.ShapeDtypeStruct((tp, d), F32), jax.ShapeDtypeStruct((ts, d), F32)],
        compiler_params=_params("arbitrary"),
        name="combine",
    )(pos, x1p, x1s, rp, rs, mod, y_sorted)


def _rope_tables(n_tok):
    rows = n_tok // GRID_W
    pos_row = jnp.repeat(jnp.arange(rows, dtype=F32), GRID_W)
    pos_col = jnp.tile(jnp.arange(GRID_W, dtype=F32), rows)
    inv_freq = 1.0 / (ROPE_BASE ** (jnp.arange(0, AXIS_DIM, 2, dtype=F32) / AXIS_DIM))
    ang = jnp.concatenate([pos_row[:, None] * inv_freq[None, :]] * 2
                          + [pos_col[:, None] * inv_freq[None, :]] * 2, axis=1)
    sign = jnp.tile(jnp.concatenate([-jnp.ones((AXIS_DIM // 2,), F32), jnp.ones((AXIS_DIM // 2,), F32)]), 2)
    cos = jnp.cos(ang)
    sin = jnp.sin(ang) * sign[None, :]
    return jnp.tile(cos, (1, LANES // QK_DIM)), jnp.tile(sin, (1, LANES // QK_DIM))


def kernel(x_prompt, x_sample, cache_k, cache_v, c, c_ctx, w_ada, b_ada, norm1_g, w_in, conv_w, w_conv_out,
           q_norm_g, k_norm_g, lambda_q1, lambda_k1, lambda_q2, lambda_k2, subln_g, w_attn_out, w_o, norm2_g,
           w_grp, b_grp, w_exp, b_exp, w1, w3, w2):
    depth = w_in.shape[0]
    assert depth == 1
    d = D_MODEL
    dec_b, dec_l = x_sample.shape[0], x_sample.shape[1]
    cos, sin = _rope_tables(dec_l)
    bd = (jnp.arange(MXU_DIM)[:, None] // QK_DIM == jnp.arange(MXU_DIM)[None, :] // QK_DIM).astype(BF16)
    tri = (jnp.arange(MOE_TOKEN_TILE)[:, None] > jnp.arange(MOE_TOKEN_TILE)[None, :]).astype(BF16)
    xp, xs = x_prompt, x_sample
    new_k, new_v = [], []
    for l in range(depth):
        lam_init = 0.8 - 0.6 * math.exp(-0.3 * l)
        mod_rows = 16
        cvec = jnp.concatenate([c_ctx[None, :], c, jnp.zeros((mod_rows - 1 - dec_b, d), F32)], axis=0)
        mod = _ada(cvec, w_ada[l], b_ada[l][None, :]).reshape(mod_rows, 6, d)

        wi = w_in[l].astype(BF16)
        w_router = jnp.concatenate([w_grp[l], w_exp[l]], axis=1)
        w_router = jnp.pad(w_router, ((0, 0), (0, LANES - w_router.shape[1])))
        wr_hi = w_router.astype(BF16)
        wr_lo = (w_router - wr_hi.astype(F32)).astype(BF16)
        b_router = jnp.pad(jnp.concatenate([b_grp[l], b_exp[l]]), (0, LANES - N_GROUPS - N_EXPERTS))[None, :]
        wts = dict(
            norm1_g=norm1_g[l][None, :], norm2_g=norm2_g[l][None, :],
            w_conv_in=wi[:, :3 * d], w_qkv=wi[:, 3 * d:6 * d], w_gate=wi[:, 6 * d:],
            conv_w=conv_w[l], w_conv_out=w_conv_out[l].astype(BF16),
            gq=jnp.tile(q_norm_g[l], d // QK_DIM)[None, :], gk=jnp.tile(k_norm_g[l], d // QK_DIM)[None, :],
            w_kt=wi[:, 4 * d:5 * d].T, gkt=jnp.broadcast_to(jnp.tile(k_norm_g[l], d // QK_DIM)[:, None], (d, LANES)),
            bd=bd, cos=cos, sin=sin,
            lamv=jnp.stack([lambda_q1[l], lambda_k1[l], lambda_q2[l], lambda_k2[l]]),
            subln_g=subln_g[l][None, :], w_attn_out=w_attn_out[l].astype(BF16), w_o=w_o[l].astype(BF16),
            wr_hi=wr_hi, wr_lo=wr_lo, b_router=b_router, tri=tri,
        )
        *ctx, k_ctx, v_ctx = _layer(xp, mod, 0, False, None, wts, lam_init)
        new_k.append(k_ctx.reshape(x_prompt.shape[0], N_HEADS, 2, QK_DIM, x_prompt.shape[1]).transpose(0, 4, 1, 2, 3))
        new_v.append(v_ctx.reshape(x_prompt.shape[0], x_prompt.shape[1], N_HEADS, V_DIM))
        past = cache_k.shape[2]
        cache = (cache_k[:, l].transpose(0, 2, 3, 4, 1).reshape(dec_b, d, past), cache_v[:, l].reshape(dec_b, past, d))
        lat = _layer(xs, mod, 1, True, cache, wts, lam_init)
        tm = MOE_TOKEN_TILE
        yp, ys = _routed_moe((ctx, lat), mod, w1[l], w3[l], w2[l], dec_l // tm, tm)
        xp, xs = yp.reshape(x_prompt.shape), ys.reshape(x_sample.shape)
    return (xp, xs, jnp.stack(new_k, axis=1), jnp.stack(new_v, axis=1))
```

```python
import functools
import math

import jax
import jax.numpy as jnp
from jax import lax
from jax.experimental import pallas as pl
from jax.experimental.pallas import tpu as pltpu

D_MODEL = 1024
GRID_W = 64
N_HEADS = 8
QK_DIM = 64
V_DIM = 2 * QK_DIM
AXIS_DIM = QK_DIM // 2
ROPE_BASE = 10000.0
N_GROUPS = 4
EXPERTS_PER_GROUP = 4
N_EXPERTS = N_GROUPS * EXPERTS_PER_GROUP
D_EXPERT = 512
EPS = 1e-6

LANES = 128
MXU_DIM = 256
VMEM_LIMIT_BYTES = 56 * 1024 * 1024
ROUTER_LANE0 = N_GROUPS
MOE_TOKEN_TILE = 512
MOE_ROWS = 512
META_ROWS = 8
DMA_UNROLL = 512
SCALAR_UNROLL = 16
ATTN_Q_ROWS = 256
ATTN_UNITS_PER_STEP = 8
ATTN_SCORES_AHEAD = 1

F32 = jnp.float32
BF16 = jnp.bfloat16


def _dot(a, b):
    return jnp.dot(a, b, preferred_element_type=F32)


def _dot_nt(a, b):
    return lax.dot_general(a, b, (((1,), (1,)), ((), ())), preferred_element_type=F32)


def _split_bf16(x):
    hi = x.astype(BF16)
    lo = (x - hi.astype(F32)).astype(BF16)
    return hi, lo


def _sigmoid(x):
    return 1.0 / (1.0 + jnp.exp(-x))


def _rms(x):
    return x * lax.rsqrt(jnp.mean(x * x, axis=-1, keepdims=True) + EPS)


def _params(*sem):
    return pltpu.CompilerParams(dimension_semantics=sem, vmem_limit_bytes=VMEM_LIMIT_BYTES)


def _ada_kernel(c_ref, w_ref, b_ref, o_ref):
    c = c_ref[...]
    s = c * _sigmoid(c)
    s_hi, s_lo = _split_bf16(s)
    w_hi, w_lo = _split_bf16(w_ref[...])
    o_ref[...] = _dot(s_hi, w_hi) + _dot(s_lo, w_hi) + _dot(s_hi, w_lo) + b_ref[...]


def _ada(cvec, w_ada, b_ada):
    rows, d = cvec.shape
    n = w_ada.shape[1]
    tn = 1024
    return pl.pallas_call(
        _ada_kernel,
        grid=(n // tn,),
        in_specs=[pl.BlockSpec((rows, d), lambda j: (0, 0)),
                  pl.BlockSpec((d, tn), lambda j: (0, j)),
                  pl.BlockSpec((1, tn), lambda j: (0, j))],
        out_specs=pl.BlockSpec((rows, tn), lambda j: (0, j)),
        out_shape=jax.ShapeDtypeStruct((rows, n), F32),
        compiler_params=_params("arbitrary"),
        name="ada",
    )(cvec, w_ada, b_ada)


def _norm_mod(x, g, scale, shift):
    return (_rms(x) * g) * (1.0 + scale) + shift


def _conv_kernel(x_ref, mod_ref, g1_ref, w_ref, cw_ref, o_ref, *, seq_len):
    x = x_ref[...]
    tm = x.shape[0]
    hb = _norm_mod(x, g1_ref[...], mod_ref[0, 1:2, :], mod_ref[0, 0:1, :]).astype(BF16)
    pos = lax.broadcasted_iota(jnp.int32, (tm, 1), 0) % seq_len
    first = pos == 0
    last = pos == seq_len - 1
    for c in range(D_MODEL // MXU_DIM):
        sl = slice(c * MXU_DIM, (c + 1) * MXU_DIM)
        cb, cc, cx = (_dot(hb, w_ref[:, pl.ds(j * D_MODEL + c * MXU_DIM, MXU_DIM)]) for j in range(3))
        u = cc * cx
        up = jnp.where(first, 0.0, pltpu.roll(u, 1, 0))
        dn = jnp.where(last, 0.0, pltpu.roll(u, tm - 1, 0))
        cw = cw_ref[:, sl]
        conv = cw[0:1] * up + cw[1:2] * u + cw[2:3] * dn
        o_ref[:, sl] = (cb * conv).astype(BF16)


def _chunk_norm(z, bd, g):
    sq = (z * z).astype(BF16)
    ss = jnp.concatenate(
        [_dot(sq[:, s * MXU_DIM:(s + 1) * MXU_DIM], bd) for s in range(z.shape[1] // MXU_DIM)], axis=1)
    return z * lax.rsqrt(ss * (1.0 / QK_DIM) + EPS) * g


def _rope(z, cos, sin, hi_half):
    n = z.shape[1]
    swap = jnp.where(hi_half, pltpu.roll(z, AXIS_DIM // 2, 1), pltpu.roll(z, n - AXIS_DIM // 2, 1))
    return z * cos + swap * sin


def _qkv_kernel(*refs, latent):
    if latent:
        x_ref, mod_ref, g1_ref, w_ref, bd_ref, gq_ref, gk_ref, cos_ref, sin_ref, q_ref, k_ref, v_ref = refs
    else:
        x_ref, mod_ref, g1_ref, w_ref, wkt_ref, bd_ref, gq_ref, gkt_ref, q_ref, kt_ref, v_ref, knt_ref, vn_ref = refs
    hb = _norm_mod(x_ref[...], g1_ref[...], mod_ref[0, 1:2, :], mod_ref[0, 0:1, :]).astype(BF16)
    tm = hb.shape[0]
    bd = bd_ref[...]
    q = _chunk_norm(_dot(hb, w_ref[:, 0:D_MODEL]), bd, gq_ref[...])
    v = _dot(hb, w_ref[:, 2 * D_MODEL:3 * D_MODEL])
    if latent:
        k = _chunk_norm(_dot(hb, w_ref[:, D_MODEL:2 * D_MODEL]), bd, gk_ref[...])
        reps = D_MODEL // LANES
        cos = jnp.concatenate([cos_ref[...]] * reps, axis=1)
        sin = jnp.concatenate([sin_ref[...]] * reps, axis=1)
        lane = lax.broadcasted_iota(jnp.int32, (1, D_MODEL), 1)
        hi_half = (lane & (AXIS_DIM // 2)) != 0
        q = _rope(q, cos, sin, hi_half)
        k_att = _rope(k, cos, sin, hi_half)
    else:
        kt = _dot_nt(wkt_ref[...], hb)
        gkt = jnp.concatenate([gkt_ref[...]] * (tm // LANES), axis=1)
        slabs = []
        for s in range(D_MODEL // MXU_DIM):
            z = kt[s * MXU_DIM:(s + 1) * MXU_DIM, :]
            ss = _dot(bd, (z * z).astype(BF16))
            slabs.append(z * lax.rsqrt(ss * (1.0 / QK_DIM) + EPS))
        knt = jnp.concatenate(slabs, axis=0) * gkt
        seq = knt_ref.shape[2]
        for j in range(tm // seq):
            knt_ref[j] = knt[:, j * seq:(j + 1) * seq]
            kt_ref[j] = knt[:, j * seq:(j + 1) * seq].astype(BF16)
        vn_ref[...] = v
    q_ref[...] = (q * (math.log2(math.e) / math.sqrt(QK_DIM))).astype(BF16)
    if latent:
        k_ref[...] = k_att.astype(BF16)
    v_ref[...] = v.astype(BF16)


def _attn_kernel(*refs, latent, lam_init, tq):
    if latent:
        lamv_ref, q_ref, k_ref, v_ref, ck_ref, cv_ref, sg_ref, o_ref = refs
    else:
        lamv_ref, q_ref, k_ref, v_ref, sg_ref, o_ref = refs
    lv = lamv_ref[...]
    lam = (jnp.exp(jnp.sum(lv[0:1] * lv[1:2], axis=1, keepdims=True))
           - jnp.exp(jnp.sum(lv[2:3] * lv[3:4], axis=1, keepdims=True)) + lam_init)
    lane = lax.broadcasted_iota(jnp.int32, (1, V_DIM), 1)
    n_q = q_ref.shape[1]
    units = [(hh, c) for hh in range(q_ref.shape[2] // V_DIM) for c in range(n_q // tq)]

    def kv(hh):
        hs = slice(hh * V_DIM, (hh + 1) * V_DIM)
        vals = [v_ref[0, :, hs]]
        if latent:
            keys = [(k_ref[0, :, hs], False), (ck_ref[0, hs, :].astype(BF16), True)]
            vals.append(cv_ref[0, :, hs].astype(BF16))
        else:
            keys = [(k_ref[0, hs, :], True)]
        return keys, vals

    def scores(u):
        hh, c = units[u]
        q = q_ref[0, c * tq:(c + 1) * tq, hh * V_DIM:(hh + 1) * V_DIM]
        keys, _ = kv(hh)
        halves = (jnp.where(lane < QK_DIM, q, jnp.zeros_like(q)), jnp.where(lane >= QK_DIM, q, jnp.zeros_like(q)))
        return [[_dot(qh, kk) if feature_major else _dot_nt(qh, kk) for kk, feature_major in keys] for qh in halves]

    ones_col = jnp.where(lane == 0, 1.0, 0.0).astype(BF16)

    def softmax_pv(s, vals):
        m = s[0].max(axis=1, keepdims=True)
        for si in s[1:]:
            m = jnp.maximum(m, si.max(axis=1, keepdims=True))
        acc = None
        for si, vv in zip(s, vals):
            v_aug = jnp.concatenate([vv, jnp.broadcast_to(ones_col, vv.shape)], axis=1)
            part = _dot(jnp.exp2(si - m).astype(BF16), v_aug)
            acc = part if acc is None else acc + part
        return acc[:, :V_DIM] * (1.0 / acc[:, V_DIM:V_DIM + 1])

    def finish(u, s):
        hh, c = units[u]
        _, vals = kv(hh)
        o = softmax_pv(s[0], vals) - lam * softmax_pv(s[1], vals)
        o_ref[0, c * tq:(c + 1) * tq, hh * V_DIM:(hh + 1) * V_DIM] = (
            _rms(o) * sg_ref[...] * (1.0 - lam_init)).astype(BF16)

    ahead = ATTN_SCORES_AHEAD
    pending = [scores(u) for u in range(min(ahead, len(units)))]
    for u in range(len(units)):
        if u + ahead < len(units):
            pending.append(scores(u + ahead))
        finish(u, pending.pop(0))


def _merge_kernel(xp_ref, xs_ref, mod_ref, g1_ref, ycp_ref, ycs_ref, oap_ref, oas_ref, wg_ref, wco_ref, wao_ref,
                  wo_ref, g2_ref, wrh_ref, wrl_ref, br_ref, tri_ref, x1_ref, h2_ref, route_ref, meta_ref, cnt_ref,
                  *, n_p):
    is_ctx = pl.program_id(0) < n_p
    x = jnp.where(is_ctx, xp_ref[...], xs_ref[...])
    yc = jnp.where(is_ctx, ycp_ref[...], ycs_ref[...])
    oa = jnp.where(is_ctx, oap_ref[...], oas_ref[...])
    hb = _norm_mod(x, g1_ref[...], mod_ref[0, 1:2, :], mod_ref[0, 0:1, :]).astype(BF16)
    y_conv = _dot(yc, wco_ref[...])
    y_attn = _dot(oa, wao_ref[...])
    mixed = (_sigmoid(_dot(hb, wg_ref[:, 0:D_MODEL])) * y_conv
             + _sigmoid(_dot(hb, wg_ref[:, D_MODEL:2 * D_MODEL])) * y_attn)
    x1 = x + mod_ref[0, 2:3, :] * _dot(mixed.astype(BF16), wo_ref[...])
    x1_ref[...] = x1
    h2 = _norm_mod(x1, g2_ref[...], mod_ref[0, 4:5, :], mod_ref[0, 3:4, :])
    h2_ref[...] = h2

    h_hi, h_lo = _split_bf16(h2)
    logits = _dot(h_hi, wrh_ref[...]) + _dot(h_lo, wrh_ref[...]) + _dot(h_hi, wrl_ref[...]) + br_ref[...]
    tm = x.shape[0]
    lane = lax.broadcasted_iota(jnp.int32, (tm, LANES), 1).astype(F32)
    ninf = -jnp.inf
    big = float(LANES)

    def top(vals):
        vmax = vals.max(axis=1, keepdims=True)
        idx = jnp.where(vals == vmax, lane, big).min(axis=1, keepdims=True)
        return vmax, idx

    is_grp = lane < N_GROUPS
    gmax, gidx = top(jnp.where(is_grp, logits, ninf))
    g_p = 1.0 / jnp.where(is_grp, jnp.exp(logits - gmax), 0.0).sum(axis=1, keepdims=True)
    lo = ROUTER_LANE0 + EXPERTS_PER_GROUP * gidx
    el = jnp.where((lane >= lo) & (lane < lo + EXPERTS_PER_GROUP), logits, ninf)
    v1, i1 = top(el)
    v2, i2 = top(jnp.where(lane == i1, ninf, el))
    e2 = jnp.exp(v2 - v1)
    den = 1.0 / (1.0 + e2)
    e0 = i1 - ROUTER_LANE0
    e1 = i2 - ROUTER_LANE0
    onehot = jnp.where((lane == e0) | (lane == e1), 1.0, 0.0)
    before = _dot(tri_ref[...], onehot.astype(BF16))
    r0 = jnp.where(lane == e0, before, 0.0).sum(axis=1, keepdims=True)
    r1 = jnp.where(lane == e1, before, 0.0).sum(axis=1, keepdims=True)
    route = (jnp.where(lane == 0.0, e0, 0.0) + jnp.where(lane == 1.0, e1, 0.0)
             + jnp.where(lane == 2.0, den * g_p, 0.0) + jnp.where(lane == 3.0, e2 * den * g_p, 0.0)
             + jnp.where(lane == 4.0, r0, 0.0) + jnp.where(lane == 5.0, r1, 0.0))
    route_ref[...] = route
    meta_ref[...] = jnp.transpose(route)[:META_ROWS, :]
    cnt_ref[0] = jnp.broadcast_to(onehot.sum(axis=0, keepdims=True), (META_ROWS, LANES))


def _row_copy(src, dst, sem):
    return pltpu.make_async_copy(src, dst, sem)


def _for_groups(n_groups, body):
    if n_groups == 1:
        body(0, 0)
    else:
        lax.fori_loop(0, n_groups, body, 0)


def _invperm_kernel(pos_ref, last_ref, nu_ref, src_ref, *, n_tok):
    def zero(g, carry):
        for u in range(SCALAR_UNROLL):
            src_ref[g * SCALAR_UNROLL + u] = 0
        return carry

    groups_per_tile = MOE_ROWS // SCALAR_UNROLL
    for e in range(N_EXPERTS):
        first = jnp.maximum(last_ref[e], 0) // SCALAR_UNROLL
        lax.fori_loop(first, first + jnp.where(last_ref[e] >= 0, groups_per_tile, 0), zero, 0)
    lax.fori_loop(nu_ref[0] * groups_per_tile, src_ref.shape[0] // SCALAR_UNROLL, zero, 0)

    def fill(g, carry):
        for u in range(SCALAR_UNROLL):
            t = g * SCALAR_UNROLL + u
            for slot in range(2):
                src_ref[pos_ref[slot * n_tok + t]] = t
        return carry

    lax.fori_loop(0, n_tok // SCALAR_UNROLL, fill, 0)


def _expert_kernel(te_ref, nu_ref, src_ref, h2_ref, w1_ref, w3_ref, w2_ref, y_ref, xbuf, w1b, w3b, w2b, sem):
    j = pl.program_id(0)
    n_used = nu_ref[0]

    def gather(tile, slot):
        for r in range(MOE_ROWS):
            tok = src_ref[tile * MOE_ROWS + r]
            _row_copy(h2_ref.at[pl.ds(tok, 1)], xbuf.at[slot, pl.ds(r, 1)], sem.at[slot]).start(priority=r % 2)

    def wait_rows(slot):
        _row_copy(xbuf.at[slot], xbuf.at[slot], sem.at[slot]).wait()

    @pl.when(j == 0)
    def _():
        gather(0, 0)

    @pl.when((j == 0) | (te_ref[j] != te_ref[jnp.maximum(j - 1, 0)]))
    def _():
        w1b[...] = w1_ref[0].astype(BF16)
        w3b[...] = w3_ref[0].astype(BF16)
        w2b[...] = w2_ref[0].astype(BF16)

    @pl.when(j < n_used)
    def _():
        slot = j % 2
        gather(jnp.minimum(j + 1, n_used - 1), 1 - slot)
        wait_rows(slot)
        t = xbuf[slot].astype(BF16)
        a = _dot(t, w1b[...])
        he = (a * _sigmoid(a)) * _dot(t, w3b[...])
        y_ref[...] = _dot(he.astype(BF16), w2b[...])

    @pl.when(j == n_used - 1)
    def _():
        wait_rows(1 - j % 2)

    @pl.when(j >= n_used)
    def _():
        y_ref[...] = jnp.zeros_like(y_ref)


def _combine_kernel(pos_ref, x1_ref, route_ref, mod_ref, ys_ref, op_ref, os_ref, ybuf, sem, *, n_p, n_tok):
    i = pl.program_id(0)
    tm = ybuf.shape[1]
    base = i * tm

    def issue(g, carry):
        for u in range(DMA_UNROLL):
            r = g * DMA_UNROLL + u
            for slot in range(2):
                p = pos_ref[slot * n_tok + base + r]
                _row_copy(ys_ref.at[pl.ds(p, 1)], ybuf.at[slot, pl.ds(r, 1)], sem).start(priority=slot)
        return carry

    _for_groups(tm // DMA_UNROLL, issue)
    for slot in range(2):
        _row_copy(ybuf.at[slot], ybuf.at[slot], sem).wait()

    r = route_ref[...]
    out = x1_ref[...] + mod_ref[0, 5:6, :] * (r[:, 2:3] * ybuf[0] + r[:, 3:4] * ybuf[1])

    @pl.when(i < n_p)
    def _():
        op_ref[...] = out

    @pl.when(i >= n_p)
    def _():
        os_ref[...] = out


def _layer(x3, mod, row0, latent, cache, wts, lam_init):
    b, l, d = x3.shape
    t = b * l
    x = x3.reshape(t, d)
    tm = MOE_TOKEN_TILE
    nt = t // tm
    per_seq = l // tm if l >= tm else 1
    if latent:
        mod_row = lambda i: row0 + i // per_seq
    else:
        mod_row = lambda i: row0

    x_spec = pl.BlockSpec((tm, d), lambda i: (i, 0))
    mod_spec = pl.BlockSpec((1, 6, d), lambda i: (mod_row(i), 0, 0))
    row_spec = pl.BlockSpec((1, d), lambda i: (0, 0))
    act_spec = pl.BlockSpec((tm, d), lambda i: (i, 0))

    def full(a):
        nd = a.ndim
        return pl.BlockSpec(a.shape, lambda i: (0,) * nd)

    tmc = max(tm, l)
    assert tmc % l == 0 and t % tmc == 0
    seq_per_tile = tmc // l
    xc_spec = pl.BlockSpec((tmc, d), lambda i: (i, 0))
    modc_spec = pl.BlockSpec((1, 6, d), lambda i: (row0 + (i * seq_per_tile if latent else 0), 0, 0))
    assert not latent or seq_per_tile == 1
    yc = pl.pallas_call(
        functools.partial(_conv_kernel, seq_len=l),
        grid=(t // tmc,),
        in_specs=[xc_spec, modc_spec, row_spec, full(wts["w_conv_in"]), full(wts["conv_w"])],
        out_specs=xc_spec,
        out_shape=jax.ShapeDtypeStruct((t, d), BF16),
        compiler_params=_params("parallel"),
        name="conv_in",
    )(x, mod, wts["norm1_g"], wts["w_conv_in"], wts["conv_w"])

    if latent:
        qkv_in = [x, mod, wts["norm1_g"], wts["w_qkv"], wts["bd"], wts["gq"], wts["gk"], wts["cos"], wts["sin"]]
        tab_spec = pl.BlockSpec((tm, LANES), lambda i: (i % per_seq, 0))
        qkv_specs = [x_spec, mod_spec, row_spec, full(wts["w_qkv"]), full(wts["bd"]), row_spec, row_spec,
                     tab_spec, tab_spec]
        out_shapes = [jax.ShapeDtypeStruct((t, d), BF16)] * 3
        out_specs = [act_spec] * 3
    else:
        assert tm % l == 0
        kt_spec = pl.BlockSpec((tm // l, d, l), lambda i: (i, 0, 0))
        qkv_in = [x, mod, wts["norm1_g"], wts["w_qkv"], wts["w_kt"], wts["bd"], wts["gq"], wts["gkt"]]
        qkv_specs = [x_spec, mod_spec, row_spec, full(wts["w_qkv"]), full(wts["w_kt"]), full(wts["bd"]), row_spec,
                     full(wts["gkt"])]
        out_shapes = [jax.ShapeDtypeStruct((t, d), BF16), jax.ShapeDtypeStruct((b, d, l), BF16),
                      jax.ShapeDtypeStruct((t, d), BF16), jax.ShapeDtypeStruct((b, d, l), F32),
                      jax.ShapeDtypeStruct((t, d), F32)]
        out_specs = [act_spec, kt_spec, act_spec, kt_spec, act_spec]
    qkv_out = pl.pallas_call(
        functools.partial(_qkv_kernel, latent=latent),
        grid=(nt,),
        in_specs=qkv_specs,
        out_specs=out_specs,
        out_shape=out_shapes,
        compiler_params=_params("parallel"),
        name="qkv",
    )(*qkv_in)
    q, k, v = qkv_out[:3]

    tq = min(ATTN_Q_ROWS, l)
    heads_per_step = max(1, ATTN_UNITS_PER_STEP // (l // tq))
    hw = heads_per_step * V_DIM
    q3, v3 = q.reshape(b, l, d), v.reshape(b, l, d)
    qkv_spec = pl.BlockSpec((1, l, hw), lambda bi, h: (bi, 0, h))
    kt_blk = lambda n_tok: pl.BlockSpec((1, hw, n_tok), lambda bi, h: (bi, h, 0))
    sg_spec = pl.BlockSpec((1, V_DIM), lambda bi, h: (0, 0))
    lam_spec = pl.BlockSpec(wts["lamv"].shape, lambda bi, h: (0, 0))
    if latent:
        ckt, cv = cache
        attn_in = [wts["lamv"], q3, k.reshape(b, l, d), v3, ckt, cv]
        attn_specs = [lam_spec, qkv_spec, qkv_spec, qkv_spec, kt_blk(ckt.shape[2]),
                      pl.BlockSpec((1, cv.shape[1], hw), lambda bi, h: (bi, 0, h))]
    else:
        attn_in = [wts["lamv"], q3, k, v3]
        attn_specs = [lam_spec, qkv_spec, kt_blk(l), qkv_spec]
    attn_in.append(wts["subln_g"])
    attn_specs.append(sg_spec)
    oa = pl.pallas_call(
        functools.partial(_attn_kernel, latent=latent, lam_init=lam_init, tq=tq),
        grid=(b, N_HEADS // heads_per_step),
        in_specs=attn_specs,
        out_specs=qkv_spec,
        out_shape=jax.ShapeDtypeStruct((b, l, d), BF16),
        compiler_params=_params("parallel", "parallel"),
        name="attn",
    )(*attn_in).reshape(t, d)

    outs = [x, yc, oa]
    if not latent:
        outs += [qkv_out[3], qkv_out[4]]
    return outs


def _merge_all(ctx, lat, mod, wts, per_seq_tiles, tm):
    tp, d = ctx[0].shape
    ts = lat[0].shape[0]
    t = tp + ts
    n_p = tp // tm
    ctx_blk = lambda i: (jnp.minimum(i, n_p - 1), 0)
    lat_blk = lambda i: (jnp.maximum(i - n_p, 0), 0)
    pair = lambda: [pl.BlockSpec((tm, d), ctx_blk), pl.BlockSpec((tm, d), lat_blk)]
    row_spec = pl.BlockSpec((1, d), lambda i: (0, 0))
    mod_spec = pl.BlockSpec((1, 6, d), lambda i: (jnp.where(i < n_p, 0, 1 + (i - n_p) // per_seq_tiles), 0, 0))

    def full(a):
        nd = a.ndim
        return pl.BlockSpec(a.shape, lambda i: (0,) * nd)

    blk = pl.BlockSpec((tm, d), lambda i: (i, 0))
    weights = [wts[k] for k in ("w_gate", "w_conv_out", "w_attn_out", "w_o")]
    router = [wts[k] for k in ("wr_hi", "wr_lo", "b_router", "tri")]
    x1, h2, route, meta, counts = pl.pallas_call(
        functools.partial(_merge_kernel, n_p=n_p),
        grid=(t // tm,),
        in_specs=(pair() + [mod_spec, row_spec] + pair() + pair() + [full(w) for w in weights] + [row_spec]
                  + [full(w) for w in router]),
        out_specs=[blk, blk, pl.BlockSpec((tm, LANES), lambda i: (i, 0)),
                   pl.BlockSpec((META_ROWS, tm), lambda i: (0, i)),
                   pl.BlockSpec((1, META_ROWS, LANES), lambda i: (i, 0, 0))],
        out_shape=[jax.ShapeDtypeStruct((t, d), F32), jax.ShapeDtypeStruct((t, d), F32),
                   jax.ShapeDtypeStruct((t, LANES), F32), jax.ShapeDtypeStruct((META_ROWS, t), F32),
                   jax.ShapeDtypeStruct((t // tm, META_ROWS, LANES), F32)],
        compiler_params=_params("parallel"),
        name="merge",
    )(ctx[0], lat[0], mod, wts["norm1_g"], ctx[1], lat[1], ctx[2], lat[2], *weights, wts["norm2_g"], *router)
    return x1, h2, route, meta, counts[:, 0, :N_EXPERTS]


def _routing_tables(meta, counts, tm, n_tiles):
    eid = meta[0:2].astype(jnp.int32)
    rank = meta[4:6].astype(jnp.int32)
    counts = counts.astype(jnp.int32)
    totals = counts.sum(axis=0)
    tiles_e = (totals + MOE_ROWS - 1) // MOE_ROWS
    tile_end = jnp.cumsum(tiles_e)
    row_start = (tile_end - tiles_e) * MOE_ROWS
    base = row_start[None, :] + jnp.cumsum(counts, axis=0) - counts
    base_tok = jnp.repeat(base.T, tm, axis=1)
    ex = jnp.arange(N_EXPERTS, dtype=jnp.int32)[None, :, None]
    pos = jnp.sum(jnp.where(eid[:, None, :] == ex, base_tok[None], 0), axis=1) + rank
    n_used = tile_end[-1]
    tile_ids = jnp.minimum(jnp.arange(n_tiles, dtype=jnp.int32), n_used - 1)
    tile_expert = jnp.sum((tile_ids[:, None] >= tile_end[None, :]).astype(jnp.int32), axis=1)
    last_tile_row = jnp.where(tiles_e > 0, (tile_end - 1) * MOE_ROWS, -1)
    return (pos.reshape(-1), tile_expert.astype(jnp.int32), n_used.astype(jnp.int32).reshape(1),
            last_tile_row.astype(jnp.int32))


def _routed_moe(x1, h2, route, meta, counts, out_rows, mod, w1, w3, w2, per_seq_tiles, tm):
    t, d = x1.shape
    tp, ts = out_rows
    n_tiles = 2 * t // MOE_ROWS + N_EXPERTS
    pos, tile_expert, n_used, last_tile_row = _routing_tables(meta, counts, tm, n_tiles)
    any_spec = pl.BlockSpec(memory_space=pl.ANY)

    src_tok = pl.pallas_call(
        functools.partial(_invperm_kernel, n_tok=t),
        grid_spec=pltpu.PrefetchScalarGridSpec(
            num_scalar_prefetch=3, grid=(1,), in_specs=[],
            out_specs=pl.BlockSpec(memory_space=pltpu.SMEM)),
        out_shape=jax.ShapeDtypeStruct((n_tiles * MOE_ROWS,), jnp.int32),
        compiler_params=_params("arbitrary"),
        name="invperm",
    )(pos, last_tile_row, n_used)

    y_sorted = pl.pallas_call(
        _expert_kernel,
        grid_spec=pltpu.PrefetchScalarGridSpec(
            num_scalar_prefetch=3,
            grid=(n_tiles,),
            in_specs=[any_spec,
                      pl.BlockSpec((1, d, D_EXPERT), lambda j, te, nu, src: (te[j], 0, 0)),
                      pl.BlockSpec((1, d, D_EXPERT), lambda j, te, nu, src: (te[j], 0, 0)),
                      pl.BlockSpec((1, D_EXPERT, d), lambda j, te, nu, src: (te[j], 0, 0))],
            out_specs=pl.BlockSpec((MOE_ROWS, d), lambda j, te, nu, src: (j, 0)),
            scratch_shapes=[pltpu.VMEM((2, MOE_ROWS, d), F32), pltpu.VMEM((d, D_EXPERT), BF16),
                            pltpu.VMEM((d, D_EXPERT), BF16), pltpu.VMEM((D_EXPERT, d), BF16),
                            pltpu.SemaphoreType.DMA((2,))]),
        out_shape=jax.ShapeDtypeStruct((n_tiles * MOE_ROWS, d), F32),
        compiler_params=_params("arbitrary"),
        name="experts",
    )(tile_expert, n_used, src_tok, h2, w1, w3, w2)

    n_p = tp // tm
    blk = lambda i, p: (i, 0)
    return pl.pallas_call(
        functools.partial(_combine_kernel, n_p=n_p, n_tok=t),
        grid_spec=pltpu.PrefetchScalarGridSpec(
            num_scalar_prefetch=1, grid=(t // tm,),
            in_specs=[pl.BlockSpec((tm, d), blk), pl.BlockSpec((tm, LANES), blk),
                      pl.BlockSpec((1, 6, d),
                                   lambda i, p: (jnp.where(i < n_p, 0, 1 + (i - n_p) // per_seq_tiles), 0, 0)),
                      any_spec],
            out_specs=[pl.BlockSpec((tm, d), lambda i, p: (jnp.minimum(i, n_p - 1), 0)),
                       pl.BlockSpec((tm, d), lambda i, p: (jnp.maximum(i - n_p, 0), 0))],
            scratch_shapes=[pltpu.VMEM((2, tm, d), F32), pltpu.SemaphoreType.DMA]),
        out_shape=[jax.ShapeDtypeStruct((tp, d), F32), jax.ShapeDtypeStruct((ts, d), F32)],
        compiler_params=_params("arbitrary"),
        name="combine",
    )(pos, x1, route, mod, y_sorted)


def _rope_tables(n_tok):
    rows = n_tok // GRID_W
    pos_row = jnp.repeat(jnp.arange(rows, dtype=F32), GRID_W)
    pos_col = jnp.tile(jnp.arange(GRID_W, dtype=F32), rows)
    inv_freq = 1.0 / (ROPE_BASE ** (jnp.arange(0, AXIS_DIM, 2, dtype=F32) / AXIS_DIM))
    ang = jnp.concatenate([pos_row[:, None] * inv_freq[None, :]] * 2
                          + [pos_col[:, None] * inv_freq[None, :]] * 2, axis=1)
    sign = jnp.tile(jnp.concatenate([-jnp.ones((AXIS_DIM // 2,), F32), jnp.ones((AXIS_DIM // 2,), F32)]), 2)
    cos = jnp.cos(ang)
    sin = jnp.sin(ang) * sign[None, :]
    return jnp.tile(cos, (1, LANES // QK_DIM)), jnp.tile(sin, (1, LANES // QK_DIM))


def kernel(x_prompt, x_sample, cache_k, cache_v, c, c_ctx, w_ada, b_ada, norm1_g, w_in, conv_w, w_conv_out,
           q_norm_g, k_norm_g, lambda_q1, lambda_k1, lambda_q2, lambda_k2, subln_g, w_attn_out, w_o, norm2_g,
           w_grp, b_grp, w_exp, b_exp, w1, w3, w2):
    depth = w_in.shape[0]
    assert depth == 1
    d = D_MODEL
    dec_b, dec_l = x_sample.shape[0], x_sample.shape[1]
    cos, sin = _rope_tables(dec_l)
    bd = (jnp.arange(MXU_DIM)[:, None] // QK_DIM == jnp.arange(MXU_DIM)[None, :] // QK_DIM).astype(BF16)
    tri = (jnp.arange(MOE_TOKEN_TILE)[:, None] > jnp.arange(MOE_TOKEN_TILE)[None, :]).astype(BF16)
    xp, xs = x_prompt, x_sample
    new_k, new_v = [], []
    for l in range(depth):
        lam_init = 0.8 - 0.6 * math.exp(-0.3 * l)
        mod_rows = 16
        cvec = jnp.concatenate([c_ctx[None, :], c, jnp.zeros((mod_rows - 1 - dec_b, d), F32)], axis=0)
        mod = _ada(cvec, w_ada[l], b_ada[l][None, :]).reshape(mod_rows, 6, d)

        wi = w_in[l].astype(BF16)
        w_router = jnp.concatenate([w_grp[l], w_exp[l]], axis=1)
        w_router = jnp.pad(w_router, ((0, 0), (0, LANES - w_router.shape[1])))
        wr_hi = w_router.astype(BF16)
        wr_lo = (w_router - wr_hi.astype(F32)).astype(BF16)
        b_router = jnp.pad(jnp.concatenate([b_grp[l], b_exp[l]]), (0, LANES - N_GROUPS - N_EXPERTS))[None, :]
        wts = dict(
            norm1_g=norm1_g[l][None, :], norm2_g=norm2_g[l][None, :],
            w_conv_in=wi[:, :3 * d], w_qkv=wi[:, 3 * d:6 * d], w_gate=wi[:, 6 * d:],
            conv_w=conv_w[l], w_conv_out=w_conv_out[l].astype(BF16),
            gq=jnp.tile(q_norm_g[l], d // QK_DIM)[None, :], gk=jnp.tile(k_norm_g[l], d // QK_DIM)[None, :],
            w_kt=wi[:, 4 * d:5 * d].T, gkt=jnp.broadcast_to(jnp.tile(k_norm_g[l], d // QK_DIM)[:, None], (d, LANES)),
            bd=bd, cos=cos, sin=sin,
            lamv=jnp.stack([lambda_q1[l], lambda_k1[l], lambda_q2[l], lambda_k2[l]]),
            subln_g=subln_g[l][None, :], w_attn_out=w_attn_out[l].astype(BF16), w_o=w_o[l].astype(BF16),
            wr_hi=wr_hi, wr_lo=wr_lo, b_router=b_router, tri=tri,
        )
        *ctx, k_ctx, v_ctx = _layer(xp, mod, 0, False, None, wts, lam_init)
        new_k.append(k_ctx.reshape(x_prompt.shape[0], N_HEADS, 2, QK_DIM, x_prompt.shape[1]).transpose(0, 4, 1, 2, 3))
        new_v.append(v_ctx.reshape(x_prompt.shape[0], x_prompt.shape[1], N_HEADS, V_DIM))
        past = cache_k.shape[2]
        cache = (cache_k[:, l].transpose(0, 2, 3, 4, 1).reshape(dec_b, d, past), cache_v[:, l].reshape(dec_b, past, d))
        lat = _layer(xs, mod, 1, True, cache, wts, lam_init)
        tm = MOE_TOKEN_TILE
        merged = _merge_all(ctx, lat, mod, wts, dec_l // tm, tm)
        yp, ys = _routed_moe(*merged, (ctx[0].shape[0], lat[0].shape[0]), mod, w1[l], w3[l], w2[l], dec_l // tm, tm)
        xp, xs = yp.reshape(x_prompt.shape), ys.reshape(x_sample.shape)
    return (xp, xs, jnp.stack(new_k, axis=1), jnp.stack(new_v, axis=1))
```

```python
import functools
import math

import jax
import jax.numpy as jnp
from jax import lax
from jax.experimental import pallas as pl
from jax.experimental.pallas import tpu as pltpu

D_MODEL = 1024
GRID_W = 64
N_HEADS = 8
QK_DIM = 64
V_DIM = 2 * QK_DIM
AXIS_DIM = QK_DIM // 2
ROPE_BASE = 10000.0
N_GROUPS = 4
EXPERTS_PER_GROUP = 4
N_EXPERTS = N_GROUPS * EXPERTS_PER_GROUP
D_EXPERT = 512
EPS = 1e-6

LANES = 128
MXU_DIM = 256
VMEM_LIMIT_BYTES = 56 * 1024 * 1024
ROUTER_LANE0 = N_GROUPS
MOE_TOKEN_TILE = 512
MOE_ROWS = 512
META_ROWS = 8
DMA_UNROLL = 512
SCALAR_UNROLL = 16
ATTN_Q_ROWS = 256
ATTN_UNITS_PER_STEP = 8
ATTN_SCORES_AHEAD = 1

F32 = jnp.float32
BF16 = jnp.bfloat16


def _dot(a, b):
    return jnp.dot(a, b, preferred_element_type=F32)


def _dot_nt(a, b):
    return lax.dot_general(a, b, (((1,), (1,)), ((), ())), preferred_element_type=F32)


def _split_bf16(x):
    hi = x.astype(BF16)
    lo = (x - hi.astype(F32)).astype(BF16)
    return hi, lo


def _sigmoid(x):
    return 1.0 / (1.0 + jnp.exp(-x))


def _rms(x):
    return x * lax.rsqrt(jnp.mean(x * x, axis=-1, keepdims=True) + EPS)


def _params(*sem):
    return pltpu.CompilerParams(dimension_semantics=sem, vmem_limit_bytes=VMEM_LIMIT_BYTES)


def _ada_kernel(c_ref, w_ref, b_ref, o_ref):
    c = c_ref[...]
    s = c * _sigmoid(c)
    s_hi, s_lo = _split_bf16(s)
    w_hi, w_lo = _split_bf16(w_ref[...])
    o_ref[...] = _dot(s_hi, w_hi) + _dot(s_lo, w_hi) + _dot(s_hi, w_lo) + b_ref[...]


def _ada(cvec, w_ada, b_ada):
    rows, d = cvec.shape
    n = w_ada.shape[1]
    tn = 1024
    return pl.pallas_call(
        _ada_kernel,
        grid=(n // tn,),
        in_specs=[pl.BlockSpec((rows, d), lambda j: (0, 0)),
                  pl.BlockSpec((d, tn), lambda j: (0, j)),
                  pl.BlockSpec((1, tn), lambda j: (0, j))],
        out_specs=pl.BlockSpec((rows, tn), lambda j: (0, j)),
        out_shape=jax.ShapeDtypeStruct((rows, n), F32),
        compiler_params=_params("arbitrary"),
        name="ada",
    )(cvec, w_ada, b_ada)


def _norm_mod(x, g, scale, shift):
    return (_rms(x) * g) * (1.0 + scale) + shift


def _conv_kernel(x_ref, mod_ref, g1_ref, w_ref, cw_ref, o_ref, *, seq_len):
    x = x_ref[...]
    tm = x.shape[0]
    hb = _norm_mod(x, g1_ref[...], mod_ref[0, 1:2, :], mod_ref[0, 0:1, :]).astype(BF16)
    pos = lax.broadcasted_iota(jnp.int32, (tm, 1), 0) % seq_len
    first = pos == 0
    last = pos == seq_len - 1
    for c in range(D_MODEL // MXU_DIM):
        sl = slice(c * MXU_DIM, (c + 1) * MXU_DIM)
        cb, cc, cx = (_dot(hb, w_ref[:, pl.ds(j * D_MODEL + c * MXU_DIM, MXU_DIM)]) for j in range(3))
        u = cc * cx
        up = jnp.where(first, 0.0, pltpu.roll(u, 1, 0))
        dn = jnp.where(last, 0.0, pltpu.roll(u, tm - 1, 0))
        cw = cw_ref[:, sl]
        conv = cw[0:1] * up + cw[1:2] * u + cw[2:3] * dn
        o_ref[:, sl] = (cb * conv).astype(BF16)


def _chunk_norm(z, bd, g):
    sq = (z * z).astype(BF16)
    ss = jnp.concatenate(
        [_dot(sq[:, s * MXU_DIM:(s + 1) * MXU_DIM], bd) for s in range(z.shape[1] // MXU_DIM)], axis=1)
    return z * lax.rsqrt(ss * (1.0 / QK_DIM) + EPS) * g


def _rope(z, cos, sin, hi_half):
    n = z.shape[1]
    swap = jnp.where(hi_half, pltpu.roll(z, AXIS_DIM // 2, 1), pltpu.roll(z, n - AXIS_DIM // 2, 1))
    return z * cos + swap * sin


def _qkv_kernel(*refs, latent):
    if latent:
        x_ref, mod_ref, g1_ref, w_ref, bd_ref, gq_ref, gk_ref, cos_ref, sin_ref, q_ref, k_ref, v_ref = refs
    else:
        x_ref, mod_ref, g1_ref, w_ref, wkt_ref, bd_ref, gq_ref, gkt_ref, q_ref, kt_ref, v_ref, knt_ref, vn_ref = refs
    hb = _norm_mod(x_ref[...], g1_ref[...], mod_ref[0, 1:2, :], mod_ref[0, 0:1, :]).astype(BF16)
    tm = hb.shape[0]
    bd = bd_ref[...]
    q = _chunk_norm(_dot(hb, w_ref[:, 0:D_MODEL]), bd, gq_ref[...])
    v = _dot(hb, w_ref[:, 2 * D_MODEL:3 * D_MODEL])
    if latent:
        k = _chunk_norm(_dot(hb, w_ref[:, D_MODEL:2 * D_MODEL]), bd, gk_ref[...])
        reps = D_MODEL // LANES
        cos = jnp.concatenate([cos_ref[...]] * reps, axis=1)
        sin = jnp.concatenate([sin_ref[...]] * reps, axis=1)
        lane = lax.broadcasted_iota(jnp.int32, (1, D_MODEL), 1)
        hi_half = (lane & (AXIS_DIM // 2)) != 0
        q = _rope(q, cos, sin, hi_half)
        k_att = _rope(k, cos, sin, hi_half)
    else:
        kt = _dot_nt(wkt_ref[...], hb)
        gkt = jnp.concatenate([gkt_ref[...]] * (tm // LANES), axis=1)
        slabs = []
        for s in range(D_MODEL // MXU_DIM):
            z = kt[s * MXU_DIM:(s + 1) * MXU_DIM, :]
            ss = _dot(bd, (z * z).astype(BF16))
            slabs.append(z * lax.rsqrt(ss * (1.0 / QK_DIM) + EPS))
        knt = jnp.concatenate(slabs, axis=0) * gkt
        seq = knt_ref.shape[2]
        for j in range(tm // seq):
            knt_ref[j] = knt[:, j * seq:(j + 1) * seq]
            kt_ref[j] = knt[:, j * seq:(j + 1) * seq].astype(BF16)
        vn_ref[...] = v
    q_ref[...] = (q * (math.log2(math.e) / math.sqrt(QK_DIM))).astype(BF16)
    if latent:
        k_ref[...] = k_att.astype(BF16)
    v_ref[...] = v.astype(BF16)


def _attn_kernel(*refs, latent, lam_init, tq):
    if latent:
        lamv_ref, q_ref, k_ref, v_ref, ck_ref, cv_ref, sg_ref, o_ref = refs
    else:
        lamv_ref, q_ref, k_ref, v_ref, sg_ref, o_ref = refs
    lv = lamv_ref[...]
    lam = (jnp.exp(jnp.sum(lv[0:1] * lv[1:2], axis=1, keepdims=True))
           - jnp.exp(jnp.sum(lv[2:3] * lv[3:4], axis=1, keepdims=True)) + lam_init)
    lane = lax.broadcasted_iota(jnp.int32, (1, V_DIM), 1)
    n_q = q_ref.shape[1]
    units = [(hh, c) for hh in range(q_ref.shape[2] // V_DIM) for c in range(n_q // tq)]

    def kv(hh):
        hs = slice(hh * V_DIM, (hh + 1) * V_DIM)
        vals = [v_ref[0, :, hs]]
        if latent:
            keys = [(k_ref[0, :, hs], False), (ck_ref[0, hs, :].astype(BF16), True)]
            vals.append(cv_ref[0, :, hs].astype(BF16))
        else:
            keys = [(k_ref[0, hs, :], True)]
        return keys, vals

    def scores(u):
        hh, c = units[u]
        q = q_ref[0, c * tq:(c + 1) * tq, hh * V_DIM:(hh + 1) * V_DIM]
        keys, _ = kv(hh)
        halves = (jnp.where(lane < QK_DIM, q, jnp.zeros_like(q)), jnp.where(lane >= QK_DIM, q, jnp.zeros_like(q)))
        return [[_dot(qh, kk) if feature_major else _dot_nt(qh, kk) for kk, feature_major in keys] for qh in halves]

    ones_col = jnp.where(lane == 0, 1.0, 0.0).astype(BF16)

    def softmax_pv(s, vals):
        m = s[0].max(axis=1, keepdims=True)
        for si in s[1:]:
            m = jnp.maximum(m, si.max(axis=1, keepdims=True))
        acc = None
        for si, vv in zip(s, vals):
            v_aug = jnp.concatenate([vv, jnp.broadcast_to(ones_col, vv.shape)], axis=1)
            part = _dot(jnp.exp2(si - m).astype(BF16), v_aug)
            acc = part if acc is None else acc + part
        return acc[:, :V_DIM] * (1.0 / acc[:, V_DIM:V_DIM + 1])

    def finish(u, s):
        hh, c = units[u]
        _, vals = kv(hh)
        o = softmax_pv(s[0], vals) - lam * softmax_pv(s[1], vals)
        o_ref[0, c * tq:(c + 1) * tq, hh * V_DIM:(hh + 1) * V_DIM] = (
            _rms(o) * sg_ref[...] * (1.0 - lam_init)).astype(BF16)

    ahead = ATTN_SCORES_AHEAD
    pending = [scores(u) for u in range(min(ahead, len(units)))]
    for u in range(len(units)):
        if u + ahead < len(units):
            pending.append(scores(u + ahead))
        finish(u, pending.pop(0))


def _merge_kernel(xp_ref, xs_ref, mod_ref, g1_ref, ycp_ref, ycs_ref, oap_ref, oas_ref, wg_ref, wco_ref, wao_ref,
                  wo_ref, g2_ref, wrh_ref, wrl_ref, br_ref, tri_ref, x1_ref, h2_ref, route_ref, meta_ref, cnt_ref,
                  *, n_p):
    is_ctx = pl.program_id(0) < n_p
    x = jnp.where(is_ctx, xp_ref[...], xs_ref[...])
    yc = jnp.where(is_ctx, ycp_ref[...], ycs_ref[...])
    oa = jnp.where(is_ctx, oap_ref[...], oas_ref[...])
    hb = _norm_mod(x, g1_ref[...], mod_ref[0, 1:2, :], mod_ref[0, 0:1, :]).astype(BF16)
    y_conv = _dot(yc, wco_ref[...])
    y_attn = _dot(oa, wao_ref[...])
    mixed = (_sigmoid(_dot(hb, wg_ref[:, 0:D_MODEL])) * y_conv
             + _sigmoid(_dot(hb, wg_ref[:, D_MODEL:2 * D_MODEL])) * y_attn)
    x1 = x + mod_ref[0, 2:3, :] * _dot(mixed.astype(BF16), wo_ref[...])
    x1_ref[...] = x1
    h2 = _norm_mod(x1, g2_ref[...], mod_ref[0, 4:5, :], mod_ref[0, 3:4, :])
    h2_ref[...] = h2

    h_hi, h_lo = _split_bf16(h2)
    logits = _dot(h_hi, wrh_ref[...]) + _dot(h_lo, wrh_ref[...]) + _dot(h_hi, wrl_ref[...]) + br_ref[...]
    tm = x.shape[0]
    lane = lax.broadcasted_iota(jnp.int32, (tm, LANES), 1).astype(F32)
    ninf = -jnp.inf
    big = float(LANES)

    def top(vals):
        vmax = vals.max(axis=1, keepdims=True)
        idx = jnp.where(vals == vmax, lane, big).min(axis=1, keepdims=True)
        return vmax, idx

    is_grp = lane < N_GROUPS
    gmax, gidx = top(jnp.where(is_grp, logits, ninf))
    g_p = 1.0 / jnp.where(is_grp, jnp.exp(logits - gmax), 0.0).sum(axis=1, keepdims=True)
    lo = ROUTER_LANE0 + EXPERTS_PER_GROUP * gidx
    el = jnp.where((lane >= lo) & (lane < lo + EXPERTS_PER_GROUP), logits, ninf)
    v1, i1 = top(el)
    v2, i2 = top(jnp.where(lane == i1, ninf, el))
    e2 = jnp.exp(v2 - v1)
    den = 1.0 / (1.0 + e2)
    e0 = i1 - ROUTER_LANE0
    e1 = i2 - ROUTER_LANE0
    onehot = jnp.where((lane == e0) | (lane == e1), 1.0, 0.0)
    before = _dot(tri_ref[...], onehot.astype(BF16))
    r0 = jnp.where(lane == e0, before, 0.0).sum(axis=1, keepdims=True)
    r1 = jnp.where(lane == e1, before, 0.0).sum(axis=1, keepdims=True)
    route = (jnp.where(lane == 0.0, e0, 0.0) + jnp.where(lane == 1.0, e1, 0.0)
             + jnp.where(lane == 2.0, den * g_p, 0.0) + jnp.where(lane == 3.0, e2 * den * g_p, 0.0)
             + jnp.where(lane == 4.0, r0, 0.0) + jnp.where(lane == 5.0, r1, 0.0))
    route_ref[...] = route
    meta_ref[...] = jnp.transpose(route)[:META_ROWS, :]
    cnt_ref[0] = jnp.broadcast_to(onehot.sum(axis=0, keepdims=True), (META_ROWS, LANES))


def _row_copy(src, dst, sem):
    return pltpu.make_async_copy(src, dst, sem)


def _for_groups(n_groups, body):
    if n_groups == 1:
        body(0, 0)
    else:
        lax.fori_loop(0, n_groups, body, 0)


def _invperm_kernel(pos_ref, last_ref, nu_ref, src_ref, *, n_tok):
    def zero(g, carry):
        for u in range(SCALAR_UNROLL):
            src_ref[g * SCALAR_UNROLL + u] = 0
        return carry

    groups_per_tile = MOE_ROWS // SCALAR_UNROLL
    for e in range(N_EXPERTS):
        first = jnp.maximum(last_ref[e], 0) // SCALAR_UNROLL
        lax.fori_loop(first, first + jnp.where(last_ref[e] >= 0, groups_per_tile, 0), zero, 0)
    lax.fori_loop(nu_ref[0] * groups_per_tile, src_ref.shape[0] // SCALAR_UNROLL, zero, 0)

    def fill(g, carry):
        for u in range(SCALAR_UNROLL):
            t = g * SCALAR_UNROLL + u
            for slot in range(2):
                src_ref[pos_ref[slot * n_tok + t]] = t
        return carry

    lax.fori_loop(0, n_tok // SCALAR_UNROLL, fill, 0)


def _expert_kernel(te_ref, nu_ref, src0_ref, src1_ref, h2_ref, w1_ref, w3_ref, w2_ref, y_ref, xbuf, w1b, w3b, w2b,
                   sem):
    j = pl.program_id(0)
    n_used = nu_ref[0]

    def gather(src_ref, slot):
        for r in range(MOE_ROWS):
            tok = src_ref[0, 0, r]
            _row_copy(h2_ref.at[pl.ds(tok, 1)], xbuf.at[slot, pl.ds(r, 1)], sem.at[slot]).start(priority=r % 2)

    def wait_rows(slot):
        _row_copy(xbuf.at[slot], xbuf.at[slot], sem.at[slot]).wait()

    @pl.when(j == 0)
    def _():
        gather(src0_ref, 0)

    @pl.when((j == 0) | (te_ref[j] != te_ref[jnp.maximum(j - 1, 0)]))
    def _():
        w1b[...] = w1_ref[0].astype(BF16)
        w3b[...] = w3_ref[0].astype(BF16)
        w2b[...] = w2_ref[0].astype(BF16)

    @pl.when(j < n_used)
    def _():
        slot = j % 2
        gather(src1_ref, 1 - slot)
        wait_rows(slot)
        t = xbuf[slot].astype(BF16)
        a = _dot(t, w1b[...])
        he = (a * _sigmoid(a)) * _dot(t, w3b[...])
        y_ref[...] = _dot(he.astype(BF16), w2b[...])

    @pl.when(j == n_used - 1)
    def _():
        wait_rows(1 - j % 2)

    @pl.when(j >= n_used)
    def _():
        y_ref[...] = jnp.zeros_like(y_ref)


def _combine_kernel(pos_ref, x1_ref, route_ref, mod_ref, ys_ref, op_ref, os_ref, ybuf, sem, *, n_p, n_tok):
    i = pl.program_id(0)
    tm = ybuf.shape[1]
    base = i * tm

    def issue(g, carry):
        for u in range(DMA_UNROLL):
            r = g * DMA_UNROLL + u
            for slot in range(2):
                p = pos_ref[slot * n_tok + base + r]
                _row_copy(ys_ref.at[pl.ds(p, 1)], ybuf.at[slot, pl.ds(r, 1)], sem).start(priority=slot)
        return carry

    _for_groups(tm // DMA_UNROLL, issue)
    for slot in range(2):
        _row_copy(ybuf.at[slot], ybuf.at[slot], sem).wait()

    r = route_ref[...]
    out = x1_ref[...] + mod_ref[0, 5:6, :] * (r[:, 2:3] * ybuf[0] + r[:, 3:4] * ybuf[1])

    @pl.when(i < n_p)
    def _():
        op_ref[...] = out

    @pl.when(i >= n_p)
    def _():
        os_ref[...] = out


def _layer(x3, mod, row0, latent, cache, wts, lam_init):
    b, l, d = x3.shape
    t = b * l
    x = x3.reshape(t, d)
    tm = MOE_TOKEN_TILE
    nt = t // tm
    per_seq = l // tm if l >= tm else 1
    if latent:
        mod_row = lambda i: row0 + i // per_seq
    else:
        mod_row = lambda i: row0

    x_spec = pl.BlockSpec((tm, d), lambda i: (i, 0))
    mod_spec = pl.BlockSpec((1, 6, d), lambda i: (mod_row(i), 0, 0))
    row_spec = pl.BlockSpec((1, d), lambda i: (0, 0))
    act_spec = pl.BlockSpec((tm, d), lambda i: (i, 0))

    def full(a):
        nd = a.ndim
        return pl.BlockSpec(a.shape, lambda i: (0,) * nd)

    tmc = max(tm, l)
    assert tmc % l == 0 and t % tmc == 0
    seq_per_tile = tmc // l
    xc_spec = pl.BlockSpec((tmc, d), lambda i: (i, 0))
    modc_spec = pl.BlockSpec((1, 6, d), lambda i: (row0 + (i * seq_per_tile if latent else 0), 0, 0))
    assert not latent or seq_per_tile == 1
    yc = pl.pallas_call(
        functools.partial(_conv_kernel, seq_len=l),
        grid=(t // tmc,),
        in_specs=[xc_spec, modc_spec, row_spec, full(wts["w_conv_in"]), full(wts["conv_w"])],
        out_specs=xc_spec,
        out_shape=jax.ShapeDtypeStruct((t, d), BF16),
        compiler_params=_params("parallel"),
        name="conv_in",
    )(x, mod, wts["norm1_g"], wts["w_conv_in"], wts["conv_w"])

    if latent:
        qkv_in = [x, mod, wts["norm1_g"], wts["w_qkv"], wts["bd"], wts["gq"], wts["gk"], wts["cos"], wts["sin"]]
        tab_spec = pl.BlockSpec((tm, LANES), lambda i: (i % per_seq, 0))
        qkv_specs = [x_spec, mod_spec, row_spec, full(wts["w_qkv"]), full(wts["bd"]), row_spec, row_spec,
                     tab_spec, tab_spec]
        out_shapes = [jax.ShapeDtypeStruct((t, d), BF16)] * 3
        out_specs = [act_spec] * 3
    else:
        assert tm % l == 0
        kt_spec = pl.BlockSpec((tm // l, d, l), lambda i: (i, 0, 0))
        qkv_in = [x, mod, wts["norm1_g"], wts["w_qkv"], wts["w_kt"], wts["bd"], wts["gq"], wts["gkt"]]
        qkv_specs = [x_spec, mod_spec, row_spec, full(wts["w_qkv"]), full(wts["w_kt"]), full(wts["bd"]), row_spec,
                     full(wts["gkt"])]
        out_shapes = [jax.ShapeDtypeStruct((t, d), BF16), jax.ShapeDtypeStruct((b, d, l), BF16),
                      jax.ShapeDtypeStruct((t, d), BF16), jax.ShapeDtypeStruct((b, d, l), F32),
                      jax.ShapeDtypeStruct((t, d), F32)]
        out_specs = [act_spec, kt_spec, act_spec, kt_spec, act_spec]
    qkv_out = pl.pallas_call(
        functools.partial(_qkv_kernel, latent=latent),
        grid=(nt,),
        in_specs=qkv_specs,
        out_specs=out_specs,
        out_shape=out_shapes,
        compiler_params=_params("parallel"),
        name="qkv",
    )(*qkv_in)
    q, k, v = qkv_out[:3]

    tq = min(ATTN_Q_ROWS, l)
    heads_per_step = max(1, ATTN_UNITS_PER_STEP // (l // tq))
    hw = heads_per_step * V_DIM
    q3, v3 = q.reshape(b, l, d), v.reshape(b, l, d)
    qkv_spec = pl.BlockSpec((1, l, hw), lambda bi, h: (bi, 0, h))
    kt_blk = lambda n_tok: pl.BlockSpec((1, hw, n_tok), lambda bi, h: (bi, h, 0))
    sg_spec = pl.BlockSpec((1, V_DIM), lambda bi, h: (0, 0))
    lam_spec = pl.BlockSpec(wts["lamv"].shape, lambda bi, h: (0, 0))
    if latent:
        ckt, cv = cache
        attn_in = [wts["lamv"], q3, k.reshape(b, l, d), v3, ckt, cv]
        attn_specs = [lam_spec, qkv_spec, qkv_spec, qkv_spec, kt_blk(ckt.shape[2]),
                      pl.BlockSpec((1, cv.shape[1], hw), lambda bi, h: (bi, 0, h))]
    else:
        attn_in = [wts["lamv"], q3, k, v3]
        attn_specs = [lam_spec, qkv_spec, kt_blk(l), qkv_spec]
    attn_in.append(wts["subln_g"])
    attn_specs.append(sg_spec)
    oa = pl.pallas_call(
        functools.partial(_attn_kernel, latent=latent, lam_init=lam_init, tq=tq),
        grid=(b, N_HEADS // heads_per_step),
        in_specs=attn_specs,
        out_specs=qkv_spec,
        out_shape=jax.ShapeDtypeStruct((b, l, d), BF16),
        compiler_params=_params("parallel", "parallel"),
        name="attn",
    )(*attn_in).reshape(t, d)

    outs = [x, yc, oa]
    if not latent:
        outs += [qkv_out[3], qkv_out[4]]
    return outs


def _merge_all(ctx, lat, mod, wts, per_seq_tiles, tm):
    tp, d = ctx[0].shape
    ts = lat[0].shape[0]
    t = tp + ts
    n_p = tp // tm
    ctx_blk = lambda i: (jnp.minimum(i, n_p - 1), 0)
    lat_blk = lambda i: (jnp.maximum(i - n_p, 0), 0)
    pair = lambda: [pl.BlockSpec((tm, d), ctx_blk), pl.BlockSpec((tm, d), lat_blk)]
    row_spec = pl.BlockSpec((1, d), lambda i: (0, 0))
    mod_spec = pl.BlockSpec((1, 6, d), lambda i: (jnp.where(i < n_p, 0, 1 + (i - n_p) // per_seq_tiles), 0, 0))

    def full(a):
        nd = a.ndim
        return pl.BlockSpec(a.shape, lambda i: (0,) * nd)

    blk = pl.BlockSpec((tm, d), lambda i: (i, 0))
    weights = [wts[k] for k in ("w_gate", "w_conv_out", "w_attn_out", "w_o")]
    router = [wts[k] for k in ("wr_hi", "wr_lo", "b_router", "tri")]
    x1, h2, route, meta, counts = pl.pallas_call(
        functools.partial(_merge_kernel, n_p=n_p),
        grid=(t // tm,),
        in_specs=(pair() + [mod_spec, row_spec] + pair() + pair() + [full(w) for w in weights] + [row_spec]
                  + [full(w) for w in router]),
        out_specs=[blk, blk, pl.BlockSpec((tm, LANES), lambda i: (i, 0)),
                   pl.BlockSpec((META_ROWS, tm), lambda i: (0, i)),
                   pl.BlockSpec((1, META_ROWS, LANES), lambda i: (i, 0, 0))],
        out_shape=[jax.ShapeDtypeStruct((t, d), F32), jax.ShapeDtypeStruct((t, d), F32),
                   jax.ShapeDtypeStruct((t, LANES), F32), jax.ShapeDtypeStruct((META_ROWS, t), F32),
                   jax.ShapeDtypeStruct((t // tm, META_ROWS, LANES), F32)],
        compiler_params=_params("parallel"),
        name="merge",
    )(ctx[0], lat[0], mod, wts["norm1_g"], ctx[1], lat[1], ctx[2], lat[2], *weights, wts["norm2_g"], *router)
    return x1, h2, route, meta, counts[:, 0, :N_EXPERTS]


def _routing_tables(meta, counts, tm, n_tiles):
    eid = meta[0:2].astype(jnp.int32)
    rank = meta[4:6].astype(jnp.int32)
    counts = counts.astype(jnp.int32)
    totals = counts.sum(axis=0)
    tiles_e = (totals + MOE_ROWS - 1) // MOE_ROWS
    tile_end = jnp.cumsum(tiles_e)
    row_start = (tile_end - tiles_e) * MOE_ROWS
    base = row_start[None, :] + jnp.cumsum(counts, axis=0) - counts
    base_tok = jnp.repeat(base.T, tm, axis=1)
    ex = jnp.arange(N_EXPERTS, dtype=jnp.int32)[None, :, None]
    pos = jnp.sum(jnp.where(eid[:, None, :] == ex, base_tok[None], 0), axis=1) + rank
    n_used = tile_end[-1]
    tile_ids = jnp.minimum(jnp.arange(n_tiles, dtype=jnp.int32), n_used - 1)
    tile_expert = jnp.sum((tile_ids[:, None] >= tile_end[None, :]).astype(jnp.int32), axis=1)
    last_tile_row = jnp.where(tiles_e > 0, (tile_end - 1) * MOE_ROWS, -1)
    return (pos.reshape(-1), tile_expert.astype(jnp.int32), n_used.astype(jnp.int32).reshape(1),
            last_tile_row.astype(jnp.int32))


def _routed_moe(x1, h2, route, meta, counts, out_rows, mod, w1, w3, w2, per_seq_tiles, tm):
    t, d = x1.shape
    tp, ts = out_rows
    n_tiles = 2 * t // MOE_ROWS + N_EXPERTS
    pos, tile_expert, n_used, last_tile_row = _routing_tables(meta, counts, tm, n_tiles)
    any_spec = pl.BlockSpec(memory_space=pl.ANY)

    src_tok = pl.pallas_call(
        functools.partial(_invperm_kernel, n_tok=t),
        grid_spec=pltpu.PrefetchScalarGridSpec(
            num_scalar_prefetch=3, grid=(1,), in_specs=[],
            out_specs=pl.BlockSpec(memory_space=pltpu.SMEM)),
        out_shape=jax.ShapeDtypeStruct((n_tiles * MOE_ROWS,), jnp.int32),
        compiler_params=_params("arbitrary"),
        name="invperm",
    )(pos, last_tile_row, n_used)
    src_tiles = src_tok.reshape(n_tiles, 1, MOE_ROWS)

    y_sorted = pl.pallas_call(
        _expert_kernel,
        grid_spec=pltpu.PrefetchScalarGridSpec(
            num_scalar_prefetch=2,
            grid=(n_tiles,),
            in_specs=[pl.BlockSpec((1, 1, MOE_ROWS), lambda j, te, nu: (0, 0, 0), memory_space=pltpu.SMEM),
                      pl.BlockSpec((1, 1, MOE_ROWS), lambda j, te, nu: (jnp.minimum(j + 1, nu[0] - 1), 0, 0),
                                   memory_space=pltpu.SMEM),
                      any_spec,
                      pl.BlockSpec((1, d, D_EXPERT), lambda j, te, nu: (te[j], 0, 0)),
                      pl.BlockSpec((1, d, D_EXPERT), lambda j, te, nu: (te[j], 0, 0)),
                      pl.BlockSpec((1, D_EXPERT, d), lambda j, te, nu: (te[j], 0, 0))],
            out_specs=pl.BlockSpec((MOE_ROWS, d), lambda j, te, nu: (j, 0)),
            scratch_shapes=[pltpu.VMEM((2, MOE_ROWS, d), F32), pltpu.VMEM((d, D_EXPERT), BF16),
                            pltpu.VMEM((d, D_EXPERT), BF16), pltpu.VMEM((D_EXPERT, d), BF16),
                            pltpu.SemaphoreType.DMA((2,))]),
        out_shape=jax.ShapeDtypeStruct((n_tiles * MOE_ROWS, d), F32),
        compiler_params=_params("arbitrary"),
        name="experts",
    )(tile_expert, n_used, src_tiles, src_tiles, h2, w1, w3, w2)

    n_p = tp // tm
    blk = lambda i, p: (i, 0)
    return pl.pallas_call(
        functools.partial(_combine_kernel, n_p=n_p, n_tok=t),
        grid_spec=pltpu.PrefetchScalarGridSpec(
            num_scalar_prefetch=1, grid=(t // tm,),
            in_specs=[pl.BlockSpec((tm, d), blk), pl.BlockSpec((tm, LANES), blk),
                      pl.BlockSpec((1, 6, d),
                                   lambda i, p: (jnp.where(i < n_p, 0, 1 + (i - n_p) // per_seq_tiles), 0, 0)),
                      any_spec],
            out_specs=[pl.BlockSpec((tm, d), lambda i, p: (jnp.minimum(i, n_p - 1), 0)),
                       pl.BlockSpec((tm, d), lambda i, p: (jnp.maximum(i - n_p, 0), 0))],
            scratch_shapes=[pltpu.VMEM((2, tm, d), F32), pltpu.SemaphoreType.DMA]),
        out_shape=[jax.ShapeDtypeStruct((tp, d), F32), jax.ShapeDtypeStruct((ts, d), F32)],
        compiler_params=_params("arbitrary"),
        name="combine",
    )(pos, x1, route, mod, y_sorted)


def _rope_tables(n_tok):
    rows = n_tok // GRID_W
    pos_row = jnp.repeat(jnp.arange(rows, dtype=F32), GRID_W)
    pos_col = jnp.tile(jnp.arange(GRID_W, dtype=F32), rows)
    inv_freq = 1.0 / (ROPE_BASE ** (jnp.arange(0, AXIS_DIM, 2, dtype=F32) / AXIS_DIM))
    ang = jnp.concatenate([pos_row[:, None] * inv_freq[None, :]] * 2
                          + [pos_col[:, None] * inv_freq[None, :]] * 2, axis=1)
    sign = jnp.tile(jnp.concatenate([-jnp.ones((AXIS_DIM // 2,), F32), jnp.ones((AXIS_DIM // 2,), F32)]), 2)
    cos = jnp.cos(ang)
    sin = jnp.sin(ang) * sign[None, :]
    return jnp.tile(cos, (1, LANES // QK_DIM)), jnp.tile(sin, (1, LANES // QK_DIM))


def kernel(x_prompt, x_sample, cache_k, cache_v, c, c_ctx, w_ada, b_ada, norm1_g, w_in, conv_w, w_conv_out,
           q_norm_g, k_norm_g, lambda_q1, lambda_k1, lambda_q2, lambda_k2, subln_g, w_attn_out, w_o, norm2_g,
           w_grp, b_grp, w_exp, b_exp, w1, w3, w2):
    depth = w_in.shape[0]
    assert depth == 1
    d = D_MODEL
    dec_b, dec_l = x_sample.shape[0], x_sample.shape[1]
    cos, sin = _rope_tables(dec_l)
    bd = (jnp.arange(MXU_DIM)[:, None] // QK_DIM == jnp.arange(MXU_DIM)[None, :] // QK_DIM).astype(BF16)
    tri = (jnp.arange(MOE_TOKEN_TILE)[:, None] > jnp.arange(MOE_TOKEN_TILE)[None, :]).astype(BF16)
    xp, xs = x_prompt, x_sample
    new_k, new_v = [], []
    for l in range(depth):
        lam_init = 0.8 - 0.6 * math.exp(-0.3 * l)
        mod_rows = 16
        cvec = jnp.concatenate([c_ctx[None, :], c, jnp.zeros((mod_rows - 1 - dec_b, d), F32)], axis=0)
        mod = _ada(cvec, w_ada[l], b_ada[l][None, :]).reshape(mod_rows, 6, d)

        wi = w_in[l].astype(BF16)
        w_router = jnp.concatenate([w_grp[l], w_exp[l]], axis=1)
        w_router = jnp.pad(w_router, ((0, 0), (0, LANES - w_router.shape[1])))
        wr_hi = w_router.astype(BF16)
        wr_lo = (w_router - wr_hi.astype(F32)).astype(BF16)
        b_router = jnp.pad(jnp.concatenate([b_grp[l], b_exp[l]]), (0, LANES - N_GROUPS - N_EXPERTS))[None, :]
        wts = dict(
            norm1_g=norm1_g[l][None, :], norm2_g=norm2_g[l][None, :],
            w_conv_in=wi[:, :3 * d], w_qkv=wi[:, 3 * d:6 * d], w_gate=wi[:, 6 * d:],
            conv_w=conv_w[l], w_conv_out=w_conv_out[l].astype(BF16),
            gq=jnp.tile(q_norm_g[l], d // QK_DIM)[None, :], gk=jnp.tile(k_norm_g[l], d // QK_DIM)[None, :],
            w_kt=wi[:, 4 * d:5 * d].T, gkt=jnp.broadcast_to(jnp.tile(k_norm_g[l], d // QK_DIM)[:, None], (d, LANES)),
            bd=bd, cos=cos, sin=sin,
            lamv=jnp.stack([lambda_q1[l], lambda_k1[l], lambda_q2[l], lambda_k2[l]]),
            subln_g=subln_g[l][None, :], w_attn_out=w_attn_out[l].astype(BF16), w_o=w_o[l].astype(BF16),
            wr_hi=wr_hi, wr_lo=wr_lo, b_router=b_router, tri=tri,
        )
        *ctx, k_ctx, v_ctx = _layer(xp, mod, 0, False, None, wts, lam_init)
        new_k.append(k_ctx.reshape(x_prompt.shape[0], N_HEADS, 2, QK_DIM, x_prompt.shape[1]).transpose(0, 4, 1, 2, 3))
        new_v.append(v_ctx.reshape(x_prompt.shape[0], x_prompt.shape[1], N_HEADS, V_DIM))
        past = cache_k.shape[2]
        cache = (cache_k[:, l].transpose(0, 2, 3, 4, 1).reshape(dec_b, d, past), cache_v[:, l].reshape(dec_b, past, d))
        lat = _layer(xs, mod, 1, True, cache, wts, lam_init)
        tm = MOE_TOKEN_TILE
        merged = _merge_all(ctx, lat, mod, wts, dec_l // tm, tm)
        yp, ys = _routed_moe(*merged, (ctx[0].shape[0], lat[0].shape[0]), mod, w1[l], w3[l], w2[l], dec_l // tm, tm)
        xp, xs = yp.reshape(x_prompt.shape), ys.reshape(x_sample.shape)
    return (xp, xs, jnp.stack(new_k, axis=1), jnp.stack(new_v, axis=1))
```

```python
import functools
import math

import jax
import jax.numpy as jnp
from jax import lax
from jax.experimental import pallas as pl
from jax.experimental.pallas import tpu as pltpu

D_MODEL = 1024
GRID_W = 64
N_HEADS = 8
QK_DIM = 64
V_DIM = 2 * QK_DIM
AXIS_DIM = QK_DIM // 2
ROPE_BASE = 10000.0
N_GROUPS = 4
EXPERTS_PER_GROUP = 4
N_EXPERTS = N_GROUPS * EXPERTS_PER_GROUP
D_EXPERT = 512
EPS = 1e-6

LANES = 128
MXU_DIM = 256
VMEM_LIMIT_BYTES = 56 * 1024 * 1024
ROUTER_LANE0 = N_GROUPS
MOE_TOKEN_TILE = 512
MOE_ROWS = 512
META_ROWS = 8
DMA_UNROLL = 512
SCALAR_UNROLL = 16
ATTN_Q_ROWS = 256
ATTN_UNITS_PER_STEP = 8
ATTN_SCORES_AHEAD = 1

F32 = jnp.float32
BF16 = jnp.bfloat16


def _dot(a, b):
    return jnp.dot(a, b, preferred_element_type=F32)


def _dot_nt(a, b):
    return lax.dot_general(a, b, (((1,), (1,)), ((), ())), preferred_element_type=F32)


def _split_bf16(x):
    hi = x.astype(BF16)
    lo = (x - hi.astype(F32)).astype(BF16)
    return hi, lo


def _sigmoid(x):
    return 1.0 / (1.0 + jnp.exp(-x))


def _rms(x):
    return x * lax.rsqrt(jnp.mean(x * x, axis=-1, keepdims=True) + EPS)


def _params(*sem):
    return pltpu.CompilerParams(dimension_semantics=sem, vmem_limit_bytes=VMEM_LIMIT_BYTES)


def _ada_kernel(c_ref, w_ref, b_ref, o_ref):
    c = c_ref[...]
    s = c * _sigmoid(c)
    s_hi, s_lo = _split_bf16(s)
    w_hi, w_lo = _split_bf16(w_ref[...])
    o_ref[...] = _dot(s_hi, w_hi) + _dot(s_lo, w_hi) + _dot(s_hi, w_lo) + b_ref[...]


def _ada(cvec, w_ada, b_ada):
    rows, d = cvec.shape
    n = w_ada.shape[1]
    tn = 1024
    return pl.pallas_call(
        _ada_kernel,
        grid=(n // tn,),
        in_specs=[pl.BlockSpec((rows, d), lambda j: (0, 0)),
                  pl.BlockSpec((d, tn), lambda j: (0, j)),
                  pl.BlockSpec((1, tn), lambda j: (0, j))],
        out_specs=pl.BlockSpec((rows, tn), lambda j: (0, j)),
        out_shape=jax.ShapeDtypeStruct((rows, n), F32),
        compiler_params=_params("arbitrary"),
        name="ada",
    )(cvec, w_ada, b_ada)


def _norm_mod(x, g, scale, shift):
    return (_rms(x) * g) * (1.0 + scale) + shift


def _conv_kernel(x_ref, mod_ref, g1_ref, w_ref, cw_ref, o_ref, *, seq_len):
    x = x_ref[...]
    tm = x.shape[0]
    hb = _norm_mod(x, g1_ref[...], mod_ref[0, 1:2, :], mod_ref[0, 0:1, :]).astype(BF16)
    pos = lax.broadcasted_iota(jnp.int32, (tm, 1), 0) % seq_len
    first = pos == 0
    last = pos == seq_len - 1
    for c in range(D_MODEL // MXU_DIM):
        sl = slice(c * MXU_DIM, (c + 1) * MXU_DIM)
        cb, cc, cx = (_dot(hb, w_ref[:, pl.ds(j * D_MODEL + c * MXU_DIM, MXU_DIM)]) for j in range(3))
        u = cc * cx
        up = jnp.where(first, 0.0, pltpu.roll(u, 1, 0))
        dn = jnp.where(last, 0.0, pltpu.roll(u, tm - 1, 0))
        cw = cw_ref[:, sl]
        conv = cw[0:1] * up + cw[1:2] * u + cw[2:3] * dn
        o_ref[:, sl] = (cb * conv).astype(BF16)


def _chunk_norm(z, bd, g):
    sq = (z * z).astype(BF16)
    ss = jnp.concatenate(
        [_dot(sq[:, s * MXU_DIM:(s + 1) * MXU_DIM], bd) for s in range(z.shape[1] // MXU_DIM)], axis=1)
    return z * lax.rsqrt(ss * (1.0 / QK_DIM) + EPS) * g


def _rope(z, cos, sin, hi_half):
    n = z.shape[1]
    swap = jnp.where(hi_half, pltpu.roll(z, AXIS_DIM // 2, 1), pltpu.roll(z, n - AXIS_DIM // 2, 1))
    return z * cos + swap * sin


def _qkv_kernel(*refs, latent):
    if latent:
        x_ref, mod_ref, g1_ref, w_ref, bd_ref, gq_ref, gk_ref, cos_ref, sin_ref, q_ref, k_ref, v_ref = refs
    else:
        x_ref, mod_ref, g1_ref, w_ref, wkt_ref, bd_ref, gq_ref, gkt_ref, q_ref, kt_ref, v_ref, knt_ref, vn_ref = refs
    hb = _norm_mod(x_ref[...], g1_ref[...], mod_ref[0, 1:2, :], mod_ref[0, 0:1, :]).astype(BF16)
    tm = hb.shape[0]
    bd = bd_ref[...]
    q = _chunk_norm(_dot(hb, w_ref[:, 0:D_MODEL]), bd, gq_ref[...])
    v = _dot(hb, w_ref[:, 2 * D_MODEL:3 * D_MODEL])
    if latent:
        k = _chunk_norm(_dot(hb, w_ref[:, D_MODEL:2 * D_MODEL]), bd, gk_ref[...])
        reps = D_MODEL // LANES
        cos = jnp.concatenate([cos_ref[...]] * reps, axis=1)
        sin = jnp.concatenate([sin_ref[...]] * reps, axis=1)
        lane = lax.broadcasted_iota(jnp.int32, (1, D_MODEL), 1)
        hi_half = (lane & (AXIS_DIM // 2)) != 0
        q = _rope(q, cos, sin, hi_half)
        k_att = _rope(k, cos, sin, hi_half)
    else:
        kt = _dot_nt(wkt_ref[...], hb)
        gkt = jnp.concatenate([gkt_ref[...]] * (tm // LANES), axis=1)
        slabs = []
        for s in range(D_MODEL // MXU_DIM):
            z = kt[s * MXU_DIM:(s + 1) * MXU_DIM, :]
            ss = _dot(bd, (z * z).astype(BF16))
            slabs.append(z * lax.rsqrt(ss * (1.0 / QK_DIM) + EPS))
        knt = jnp.concatenate(slabs, axis=0) * gkt
        seq = knt_ref.shape[2]
        for j in range(tm // seq):
            knt_ref[j] = knt[:, j * seq:(j + 1) * seq]
            kt_ref[j] = knt[:, j * seq:(j + 1) * seq].astype(BF16)
        vn_ref[...] = v
    q_ref[...] = (q * (math.log2(math.e) / math.sqrt(QK_DIM))).astype(BF16)
    if latent:
        k_ref[...] = k_att.astype(BF16)
    v_ref[...] = v.astype(BF16)


def _attn_kernel(*refs, latent, lam_init, tq):
    if latent:
        lamv_ref, q_ref, k_ref, v_ref, ck_ref, cv_ref, sg_ref, o_ref = refs
    else:
        lamv_ref, q_ref, k_ref, v_ref, sg_ref, o_ref = refs
    lv = lamv_ref[...]
    lam = (jnp.exp(jnp.sum(lv[0:1] * lv[1:2], axis=1, keepdims=True))
           - jnp.exp(jnp.sum(lv[2:3] * lv[3:4], axis=1, keepdims=True)) + lam_init)
    lane = lax.broadcasted_iota(jnp.int32, (1, V_DIM), 1)
    n_q = q_ref.shape[1]
    units = [(hh, c) for hh in range(q_ref.shape[2] // V_DIM) for c in range(n_q // tq)]

    def kv(hh):
        hs = slice(hh * V_DIM, (hh + 1) * V_DIM)
        vals = [v_ref[0, :, hs]]
        if latent:
            keys = [(k_ref[0, :, hs], False), (ck_ref[0, hs, :].astype(BF16), True)]
            vals.append(cv_ref[0, :, hs].astype(BF16))
        else:
            keys = [(k_ref[0, hs, :], True)]
        return keys, vals

    def scores(u):
        hh, c = units[u]
        q = q_ref[0, c * tq:(c + 1) * tq, hh * V_DIM:(hh + 1) * V_DIM]
        keys, _ = kv(hh)
        halves = (jnp.where(lane < QK_DIM, q, jnp.zeros_like(q)), jnp.where(lane >= QK_DIM, q, jnp.zeros_like(q)))
        return [[_dot(qh, kk) if feature_major else _dot_nt(qh, kk) for kk, feature_major in keys] for qh in halves]

    ones_col = jnp.where(lane == 0, 1.0, 0.0).astype(BF16)

    def softmax_pv(s, vals):
        m = s[0].max(axis=1, keepdims=True)
        for si in s[1:]:
            m = jnp.maximum(m, si.max(axis=1, keepdims=True))
        acc = None
        for si, vv in zip(s, vals):
            v_aug = jnp.concatenate([vv, jnp.broadcast_to(ones_col, vv.shape)], axis=1)
            part = _dot(jnp.exp2(si - m).astype(BF16), v_aug)
            acc = part if acc is None else acc + part
        return acc[:, :V_DIM] * (1.0 / acc[:, V_DIM:V_DIM + 1])

    def finish(u, s):
        hh, c = units[u]
        _, vals = kv(hh)
        o = softmax_pv(s[0], vals) - lam * softmax_pv(s[1], vals)
        o_ref[0, c * tq:(c + 1) * tq, hh * V_DIM:(hh + 1) * V_DIM] = (
            _rms(o) * sg_ref[...] * (1.0 - lam_init)).astype(BF16)

    ahead = ATTN_SCORES_AHEAD
    pending = [scores(u) for u in range(min(ahead, len(units)))]
    for u in range(len(units)):
        if u + ahead < len(units):
            pending.append(scores(u + ahead))
        finish(u, pending.pop(0))


def _merge_kernel(xp_ref, xs_ref, mod_ref, g1_ref, ycp_ref, ycs_ref, oap_ref, oas_ref, wg_ref, wco_ref, wao_ref,
                  wo_ref, g2_ref, wrh_ref, wrl_ref, br_ref, tri_ref, x1_ref, h2_ref, route_ref, meta_ref, cnt_ref,
                  *, n_p):
    is_ctx = pl.program_id(0) < n_p
    x = jnp.where(is_ctx, xp_ref[...], xs_ref[...])
    yc = jnp.where(is_ctx, ycp_ref[...], ycs_ref[...])
    oa = jnp.where(is_ctx, oap_ref[...], oas_ref[...])
    hb = _norm_mod(x, g1_ref[...], mod_ref[0, 1:2, :], mod_ref[0, 0:1, :]).astype(BF16)
    y_conv = _dot(yc, wco_ref[...])
    y_attn = _dot(oa, wao_ref[...])
    mixed = (_sigmoid(_dot(hb, wg_ref[:, 0:D_MODEL])) * y_conv
             + _sigmoid(_dot(hb, wg_ref[:, D_MODEL:2 * D_MODEL])) * y_attn)
    x1 = x + mod_ref[0, 2:3, :] * _dot(mixed.astype(BF16), wo_ref[...])
    x1_ref[...] = x1
    h2 = _norm_mod(x1, g2_ref[...], mod_ref[0, 4:5, :], mod_ref[0, 3:4, :])
    h2_ref[...] = h2

    h_hi, h_lo = _split_bf16(h2)
    logits = _dot(h_hi, wrh_ref[...]) + _dot(h_lo, wrh_ref[...]) + _dot(h_hi, wrl_ref[...]) + br_ref[...]
    tm = x.shape[0]
    lane = lax.broadcasted_iota(jnp.int32, (tm, LANES), 1).astype(F32)
    ninf = -jnp.inf
    big = float(LANES)

    def top(vals):
        vmax = vals.max(axis=1, keepdims=True)
        idx = jnp.where(vals == vmax, lane, big).min(axis=1, keepdims=True)
        return vmax, idx

    is_grp = lane < N_GROUPS
    gmax, gidx = top(jnp.where(is_grp, logits, ninf))
    g_p = 1.0 / jnp.where(is_grp, jnp.exp(logits - gmax), 0.0).sum(axis=1, keepdims=True)
    lo = ROUTER_LANE0 + EXPERTS_PER_GROUP * gidx
    el = jnp.where((lane >= lo) & (lane < lo + EXPERTS_PER_GROUP), logits, ninf)
    v1, i1 = top(el)
    v2, i2 = top(jnp.where(lane == i1, ninf, el))
    e2 = jnp.exp(v2 - v1)
    den = 1.0 / (1.0 + e2)
    e0 = i1 - ROUTER_LANE0
    e1 = i2 - ROUTER_LANE0
    onehot = jnp.where((lane == e0) | (lane == e1), 1.0, 0.0)
    before = _dot(tri_ref[...], onehot.astype(BF16))
    r0 = jnp.where(lane == e0, before, 0.0).sum(axis=1, keepdims=True)
    r1 = jnp.where(lane == e1, before, 0.0).sum(axis=1, keepdims=True)
    route = (jnp.where(lane == 0.0, e0, 0.0) + jnp.where(lane == 1.0, e1, 0.0)
             + jnp.where(lane == 2.0, den * g_p, 0.0) + jnp.where(lane == 3.0, e2 * den * g_p, 0.0)
             + jnp.where(lane == 4.0, r0, 0.0) + jnp.where(lane == 5.0, r1, 0.0))
    route_ref[...] = route
    meta_ref[...] = jnp.transpose(route)[:META_ROWS, :]
    cnt_ref[0] = jnp.broadcast_to(onehot.sum(axis=0, keepdims=True), (META_ROWS, LANES))


def _row_copy(src, dst, sem):
    return pltpu.make_async_copy(src, dst, sem)


def _for_groups(n_groups, body):
    if n_groups == 1:
        body(0, 0)
    else:
        lax.fori_loop(0, n_groups, body, 0)


def _invperm_kernel(pos_ref, last_ref, nu_ref, src_ref, *, n_tok):
    def zero(g, carry):
        for u in range(SCALAR_UNROLL):
            src_ref[g * SCALAR_UNROLL + u] = 0
        return carry

    groups_per_tile = MOE_ROWS // SCALAR_UNROLL
    for e in range(N_EXPERTS):
        first = jnp.maximum(last_ref[e], 0) // SCALAR_UNROLL
        lax.fori_loop(first, first + jnp.where(last_ref[e] >= 0, groups_per_tile, 0), zero, 0)
    lax.fori_loop(nu_ref[0] * groups_per_tile, src_ref.shape[0] // SCALAR_UNROLL, zero, 0)

    def fill(g, carry):
        for u in range(SCALAR_UNROLL):
            t = g * SCALAR_UNROLL + u
            for slot in range(2):
                src_ref[pos_ref[slot * n_tok + t]] = t
        return carry

    lax.fori_loop(0, n_tok // SCALAR_UNROLL, fill, 0)


def _expert_kernel(te_ref, nu_ref, src0_ref, src1_ref, h2_ref, w1_ref, w3_ref, w2_ref, y_ref, xbuf, w1b, w3b, w2b,
                   sem):
    j = pl.program_id(0)
    n_used = nu_ref[0]

    def gather(src_ref, slot):
        for r in range(MOE_ROWS):
            tok = src_ref[0, 0, r]
            _row_copy(h2_ref.at[pl.ds(tok, 1)], xbuf.at[slot, pl.ds(r, 1)], sem.at[slot]).start(priority=1)

    def wait_rows(slot):
        _row_copy(xbuf.at[slot], xbuf.at[slot], sem.at[slot]).wait()

    @pl.when(j == 0)
    def _():
        gather(src0_ref, 0)

    @pl.when((j == 0) | (te_ref[j] != te_ref[jnp.maximum(j - 1, 0)]))
    def _():
        w1b[...] = w1_ref[0].astype(BF16)
        w3b[...] = w3_ref[0].astype(BF16)
        w2b[...] = w2_ref[0].astype(BF16)

    @pl.when(j < n_used)
    def _():
        slot = j % 2
        gather(src1_ref, 1 - slot)
        wait_rows(slot)
        t = xbuf[slot].astype(BF16)
        a = _dot(t, w1b[...])
        he = (a * _sigmoid(a)) * _dot(t, w3b[...])
        y_ref[...] = _dot(he.astype(BF16), w2b[...])

    @pl.when(j == n_used - 1)
    def _():
        wait_rows(1 - j % 2)

    @pl.when(j >= n_used)
    def _():
        y_ref[...] = jnp.zeros_like(y_ref)


def _combine_kernel(pos_ref, x1_ref, route_ref, mod_ref, ys_ref, op_ref, os_ref, ybuf, sem, *, n_p, n_tok):
    i = pl.program_id(0)
    tm = ybuf.shape[1]
    base = i * tm

    def issue(g, carry):
        for u in range(DMA_UNROLL):
            r = g * DMA_UNROLL + u
            for slot in range(2):
                p = pos_ref[slot * n_tok + base + r]
                _row_copy(ys_ref.at[pl.ds(p, 1)], ybuf.at[slot, pl.ds(r, 1)], sem).start(priority=slot)
        return carry

    _for_groups(tm // DMA_UNROLL, issue)
    for slot in range(2):
        _row_copy(ybuf.at[slot], ybuf.at[slot], sem).wait()

    r = route_ref[...]
    out = x1_ref[...] + mod_ref[0, 5:6, :] * (r[:, 2:3] * ybuf[0] + r[:, 3:4] * ybuf[1])

    @pl.when(i < n_p)
    def _():
        op_ref[...] = out

    @pl.when(i >= n_p)
    def _():
        os_ref[...] = out


def _layer(x3, mod, row0, latent, cache, wts, lam_init):
    b, l, d = x3.shape
    t = b * l
    x = x3.reshape(t, d)
    tm = MOE_TOKEN_TILE
    nt = t // tm
    per_seq = l // tm if l >= tm else 1
    if latent:
        mod_row = lambda i: row0 + i // per_seq
    else:
        mod_row = lambda i: row0

    x_spec = pl.BlockSpec((tm, d), lambda i: (i, 0))
    mod_spec = pl.BlockSpec((1, 6, d), lambda i: (mod_row(i), 0, 0))
    row_spec = pl.BlockSpec((1, d), lambda i: (0, 0))
    act_spec = pl.BlockSpec((tm, d), lambda i: (i, 0))

    def full(a):
        nd = a.ndim
        return pl.BlockSpec(a.shape, lambda i: (0,) * nd)

    tmc = max(tm, l)
    assert tmc % l == 0 and t % tmc == 0
    seq_per_tile = tmc // l
    xc_spec = pl.BlockSpec((tmc, d), lambda i: (i, 0))
    modc_spec = pl.BlockSpec((1, 6, d), lambda i: (row0 + (i * seq_per_tile if latent else 0), 0, 0))
    assert not latent or seq_per_tile == 1
    yc = pl.pallas_call(
        functools.partial(_conv_kernel, seq_len=l),
        grid=(t // tmc,),
        in_specs=[xc_spec, modc_spec, row_spec, full(wts["w_conv_in"]), full(wts["conv_w"])],
        out_specs=xc_spec,
        out_shape=jax.ShapeDtypeStruct((t, d), BF16),
        compiler_params=_params("parallel"),
        name="conv_in",
    )(x, mod, wts["norm1_g"], wts["w_conv_in"], wts["conv_w"])

    if latent:
        qkv_in = [x, mod, wts["norm1_g"], wts["w_qkv"], wts["bd"], wts["gq"], wts["gk"], wts["cos"], wts["sin"]]
        tab_spec = pl.BlockSpec((tm, LANES), lambda i: (i % per_seq, 0))
        qkv_specs = [x_spec, mod_spec, row_spec, full(wts["w_qkv"]), full(wts["bd"]), row_spec, row_spec,
                     tab_spec, tab_spec]
        out_shapes = [jax.ShapeDtypeStruct((t, d), BF16)] * 3
        out_specs = [act_spec] * 3
    else:
        assert tm % l == 0
        kt_spec = pl.BlockSpec((tm // l, d, l), lambda i: (i, 0, 0))
        qkv_in = [x, mod, wts["norm1_g"], wts["w_qkv"], wts["w_kt"], wts["bd"], wts["gq"], wts["gkt"]]
        qkv_specs = [x_spec, mod_spec, row_spec, full(wts["w_qkv"]), full(wts["w_kt"]), full(wts["bd"]), row_spec,
                     full(wts["gkt"])]
        out_shapes = [jax.ShapeDtypeStruct((t, d), BF16), jax.ShapeDtypeStruct((b, d, l), BF16),
                      jax.ShapeDtypeStruct((t, d), BF16), jax.ShapeDtypeStruct((b, d, l), F32),
                      jax.ShapeDtypeStruct((t, d), F32)]
        out_specs = [act_spec, kt_spec, act_spec, kt_spec, act_spec]
    qkv_out = pl.pallas_call(
        functools.partial(_qkv_kernel, latent=latent),
        grid=(nt,),
        in_specs=qkv_specs,
        out_specs=out_specs,
        out_shape=out_shapes,
        compiler_params=_params("parallel"),
        name="qkv",
    )(*qkv_in)
    q, k, v = qkv_out[:3]

    tq = min(ATTN_Q_ROWS, l)
    heads_per_step = max(1, ATTN_UNITS_PER_STEP // (l // tq))
    hw = heads_per_step * V_DIM
    q3, v3 = q.reshape(b, l, d), v.reshape(b, l, d)
    qkv_spec = pl.BlockSpec((1, l, hw), lambda bi, h: (bi, 0, h))
    kt_blk = lambda n_tok: pl.BlockSpec((1, hw, n_tok), lambda bi, h: (bi, h, 0))
    sg_spec = pl.BlockSpec((1, V_DIM), lambda bi, h: (0, 0))
    lam_spec = pl.BlockSpec(wts["lamv"].shape, lambda bi, h: (0, 0))
    if latent:
        ckt, cv = cache
        attn_in = [wts["lamv"], q3, k.reshape(b, l, d), v3, ckt, cv]
        attn_specs = [lam_spec, qkv_spec, qkv_spec, qkv_spec, kt_blk(ckt.shape[2]),
                      pl.BlockSpec((1, cv.shape[1], hw), lambda bi, h: (bi, 0, h))]
    else:
        attn_in = [wts["lamv"], q3, k, v3]
        attn_specs = [lam_spec, qkv_spec, kt_blk(l), qkv_spec]
    attn_in.append(wts["subln_g"])
    attn_specs.append(sg_spec)
    oa = pl.pallas_call(
        functools.partial(_attn_kernel, latent=latent, lam_init=lam_init, tq=tq),
        grid=(b, N_HEADS // heads_per_step),
        in_specs=attn_specs,
        out_specs=qkv_spec,
        out_shape=jax.ShapeDtypeStruct((b, l, d), BF16),
        compiler_params=_params("parallel", "parallel"),
        name="attn",
    )(*attn_in).reshape(t, d)

    outs = [x, yc, oa]
    if not latent:
        outs += [qkv_out[3], qkv_out[4]]
    return outs


def _merge_all(ctx, lat, mod, wts, per_seq_tiles, tm):
    tp, d = ctx[0].shape
    ts = lat[0].shape[0]
    t = tp + ts
    n_p = tp // tm
    ctx_blk = lambda i: (jnp.minimum(i, n_p - 1), 0)
    lat_blk = lambda i: (jnp.maximum(i - n_p, 0), 0)
    pair = lambda: [pl.BlockSpec((tm, d), ctx_blk), pl.BlockSpec((tm, d), lat_blk)]
    row_spec = pl.BlockSpec((1, d), lambda i: (0, 0))
    mod_spec = pl.BlockSpec((1, 6, d), lambda i: (jnp.where(i < n_p, 0, 1 + (i - n_p) // per_seq_tiles), 0, 0))

    def full(a):
        nd = a.ndim
        return pl.BlockSpec(a.shape, lambda i: (0,) * nd)

    blk = pl.BlockSpec((tm, d), lambda i: (i, 0))
    weights = [wts[k] for k in ("w_gate", "w_conv_out", "w_attn_out", "w_o")]
    router = [wts[k] for k in ("wr_hi", "wr_lo", "b_router", "tri")]
    x1, h2, route, meta, counts = pl.pallas_call(
        functools.partial(_merge_kernel, n_p=n_p),
        grid=(t // tm,),
        in_specs=(pair() + [mod_spec, row_spec] + pair() + pair() + [full(w) for w in weights] + [row_spec]
                  + [full(w) for w in router]),
        out_specs=[blk, blk, pl.BlockSpec((tm, LANES), lambda i: (i, 0)),
                   pl.BlockSpec((META_ROWS, tm), lambda i: (0, i)),
                   pl.BlockSpec((1, META_ROWS, LANES), lambda i: (i, 0, 0))],
        out_shape=[jax.ShapeDtypeStruct((t, d), F32), jax.ShapeDtypeStruct((t, d), F32),
                   jax.ShapeDtypeStruct((t, LANES), F32), jax.ShapeDtypeStruct((META_ROWS, t), F32),
                   jax.ShapeDtypeStruct((t // tm, META_ROWS, LANES), F32)],
        compiler_params=_params("parallel"),
        name="merge",
    )(ctx[0], lat[0], mod, wts["norm1_g"], ctx[1], lat[1], ctx[2], lat[2], *weights, wts["norm2_g"], *router)
    return x1, h2, route, meta, counts[:, 0, :N_EXPERTS]


def _routing_tables(meta, counts, tm, n_tiles):
    eid = meta[0:2].astype(jnp.int32)
    rank = meta[4:6].astype(jnp.int32)
    counts = counts.astype(jnp.int32)
    totals = counts.sum(axis=0)
    tiles_e = (totals + MOE_ROWS - 1) // MOE_ROWS
    tile_end = jnp.cumsum(tiles_e)
    row_start = (tile_end - tiles_e) * MOE_ROWS
    base = row_start[None, :] + jnp.cumsum(counts, axis=0) - counts
    base_tok = jnp.repeat(base.T, tm, axis=1)
    ex = jnp.arange(N_EXPERTS, dtype=jnp.int32)[None, :, None]
    pos = jnp.sum(jnp.where(eid[:, None, :] == ex, base_tok[None], 0), axis=1) + rank
    n_used = tile_end[-1]
    tile_ids = jnp.minimum(jnp.arange(n_tiles, dtype=jnp.int32), n_used - 1)
    tile_expert = jnp.sum((tile_ids[:, None] >= tile_end[None, :]).astype(jnp.int32), axis=1)
    last_tile_row = jnp.where(tiles_e > 0, (tile_end - 1) * MOE_ROWS, -1)
    return (pos.reshape(-1), tile_expert.astype(jnp.int32), n_used.astype(jnp.int32).reshape(1),
            last_tile_row.astype(jnp.int32))


def _routed_moe(x1, h2, route, meta, counts, out_rows, mod, w1, w3, w2, per_seq_tiles, tm):
    t, d = x1.shape
    tp, ts = out_rows
    n_tiles = 2 * t // MOE_ROWS + N_EXPERTS
    pos, tile_expert, n_used, last_tile_row = _routing_tables(meta, counts, tm, n_tiles)
    any_spec = pl.BlockSpec(memory_space=pl.ANY)

    src_tok = pl.pallas_call(
        functools.partial(_invperm_kernel, n_tok=t),
        grid_spec=pltpu.PrefetchScalarGridSpec(
            num_scalar_prefetch=3, grid=(1,), in_specs=[],
            out_specs=pl.BlockSpec(memory_space=pltpu.SMEM)),
        out_shape=jax.ShapeDtypeStruct((n_tiles * MOE_ROWS,), jnp.int32),
        compiler_params=_params("arbitrary"),
        name="invperm",
    )(pos, last_tile_row, n_used)
    src_tiles = src_tok.reshape(n_tiles, 1, MOE_ROWS)

    y_sorted = pl.pallas_call(
        _expert_kernel,
        grid_spec=pltpu.PrefetchScalarGridSpec(
            num_scalar_prefetch=2,
            grid=(n_tiles,),
            in_specs=[pl.BlockSpec((1, 1, MOE_ROWS), lambda j, te, nu: (0, 0, 0), memory_space=pltpu.SMEM),
                      pl.BlockSpec((1, 1, MOE_ROWS), lambda j, te, nu: (jnp.minimum(j + 1, nu[0] - 1), 0, 0),
                                   memory_space=pltpu.SMEM),
                      any_spec,
                      pl.BlockSpec((1, d, D_EXPERT), lambda j, te, nu: (te[j], 0, 0)),
                      pl.BlockSpec((1, d, D_EXPERT), lambda j, te, nu: (te[j], 0, 0)),
                      pl.BlockSpec((1, D_EXPERT, d), lambda j, te, nu: (te[j], 0, 0))],
            out_specs=pl.BlockSpec((MOE_ROWS, d), lambda j, te, nu: (j, 0)),
            scratch_shapes=[pltpu.VMEM((2, MOE_ROWS, d), F32), pltpu.VMEM((d, D_EXPERT), BF16),
                            pltpu.VMEM((d, D_EXPERT), BF16), pltpu.VMEM((D_EXPERT, d), BF16),
                            pltpu.SemaphoreType.DMA((2,))]),
        out_shape=jax.ShapeDtypeStruct((n_tiles * MOE_ROWS, d), F32),
        compiler_params=_params("arbitrary"),
        name="experts",
    )(tile_expert, n_used, src_tiles, src_tiles, h2, w1, w3, w2)

    n_p = tp // tm
    blk = lambda i, p: (i, 0)
    return pl.pallas_call(
        functools.partial(_combine_kernel, n_p=n_p, n_tok=t),
        grid_spec=pltpu.PrefetchScalarGridSpec(
            num_scalar_prefetch=1, grid=(t // tm,),
            in_specs=[pl.BlockSpec((tm, d), blk), pl.BlockSpec((tm, LANES), blk),
                      pl.BlockSpec((1, 6, d),
                                   lambda i, p: (jnp.where(i < n_p, 0, 1 + (i - n_p) // per_seq_tiles), 0, 0)),
                      any_spec],
            out_specs=[pl.BlockSpec((tm, d), lambda i, p: (jnp.minimum(i, n_p - 1), 0)),
                       pl.BlockSpec((tm, d), lambda i, p: (jnp.maximum(i - n_p, 0), 0))],
            scratch_shapes=[pltpu.VMEM((2, tm, d), F32), pltpu.SemaphoreType.DMA]),
        out_shape=[jax.ShapeDtypeStruct((tp, d), F32), jax.ShapeDtypeStruct((ts, d), F32)],
        compiler_params=_params("arbitrary"),
        name="combine",
    )(pos, x1, route, mod, y_sorted)


def _rope_tables(n_tok):
    rows = n_tok // GRID_W
    pos_row = jnp.repeat(jnp.arange(rows, dtype=F32), GRID_W)
    pos_col = jnp.tile(jnp.arange(GRID_W, dtype=F32), rows)
    inv_freq = 1.0 / (ROPE_BASE ** (jnp.arange(0, AXIS_DIM, 2, dtype=F32) / AXIS_DIM))
    ang = jnp.concatenate([pos_row[:, None] * inv_freq[None, :]] * 2
                          + [pos_col[:, None] * inv_freq[None, :]] * 2, axis=1)
    sign = jnp.tile(jnp.concatenate([-jnp.ones((AXIS_DIM // 2,), F32), jnp.ones((AXIS_DIM // 2,), F32)]), 2)
    cos = jnp.cos(ang)
    sin = jnp.sin(ang) * sign[None, :]
    return jnp.tile(cos, (1, LANES // QK_DIM)), jnp.tile(sin, (1, LANES // QK_DIM))


def kernel(x_prompt, x_sample, cache_k, cache_v, c, c_ctx, w_ada, b_ada, norm1_g, w_in, conv_w, w_conv_out,
           q_norm_g, k_norm_g, lambda_q1, lambda_k1, lambda_q2, lambda_k2, subln_g, w_attn_out, w_o, norm2_g,
           w_grp, b_grp, w_exp, b_exp, w1, w3, w2):
    depth = w_in.shape[0]
    assert depth == 1
    d = D_MODEL
    dec_b, dec_l = x_sample.shape[0], x_sample.shape[1]
    cos, sin = _rope_tables(dec_l)
    bd = (jnp.arange(MXU_DIM)[:, None] // QK_DIM == jnp.arange(MXU_DIM)[None, :] // QK_DIM).astype(BF16)
    tri = (jnp.arange(MOE_TOKEN_TILE)[:, None] > jnp.arange(MOE_TOKEN_TILE)[None, :]).astype(BF16)
    xp, xs = x_prompt, x_sample
    new_k, new_v = [], []
    for l in range(depth):
        lam_init = 0.8 - 0.6 * math.exp(-0.3 * l)
        mod_rows = 16
        cvec = jnp.concatenate([c_ctx[None, :], c, jnp.zeros((mod_rows - 1 - dec_b, d), F32)], axis=0)
        mod = _ada(cvec, w_ada[l], b_ada[l][None, :]).reshape(mod_rows, 6, d)

        wi = w_in[l].astype(BF16)
        w_router = jnp.concatenate([w_grp[l], w_exp[l]], axis=1)
        w_router = jnp.pad(w_router, ((0, 0), (0, LANES - w_router.shape[1])))
        wr_hi = w_router.astype(BF16)
        wr_lo = (w_router - wr_hi.astype(F32)).astype(BF16)
        b_router = jnp.pad(jnp.concatenate([b_grp[l], b_exp[l]]), (0, LANES - N_GROUPS - N_EXPERTS))[None, :]
        wts = dict(
            norm1_g=norm1_g[l][None, :], norm2_g=norm2_g[l][None, :],
            w_conv_in=wi[:, :3 * d], w_qkv=wi[:, 3 * d:6 * d], w_gate=wi[:, 6 * d:],
            conv_w=conv_w[l], w_conv_out=w_conv_out[l].astype(BF16),
            gq=jnp.tile(q_norm_g[l], d // QK_DIM)[None, :], gk=jnp.tile(k_norm_g[l], d // QK_DIM)[None, :],
            w_kt=wi[:, 4 * d:5 * d].T, gkt=jnp.broadcast_to(jnp.tile(k_norm_g[l], d // QK_DIM)[:, None], (d, LANES)),
            bd=bd, cos=cos, sin=sin,
            lamv=jnp.stack([lambda_q1[l], lambda_k1[l], lambda_q2[l], lambda_k2[l]]),
            subln_g=subln_g[l][None, :], w_attn_out=w_attn_out[l].astype(BF16), w_o=w_o[l].astype(BF16),
            wr_hi=wr_hi, wr_lo=wr_lo, b_router=b_router, tri=tri,
        )
        *ctx, k_ctx, v_ctx = _layer(xp, mod, 0, False, None, wts, lam_init)
        new_k.append(k_ctx.reshape(x_prompt.shape[0], N_HEADS, 2, QK_DIM, x_prompt.shape[1]).transpose(0, 4, 1, 2, 3))
        new_v.append(v_ctx.reshape(x_prompt.shape[0], x_prompt.shape[1], N_HEADS, V_DIM))
        past = cache_k.shape[2]
        cache = (cache_k[:, l].transpose(0, 2, 3, 4, 1).reshape(dec_b, d, past), cache_v[:, l].reshape(dec_b, past, d))
        lat = _layer(xs, mod, 1, True, cache, wts, lam_init)
        tm = MOE_TOKEN_TILE
        merged = _merge_all(ctx, lat, mod, wts, dec_l // tm, tm)
        yp, ys = _routed_moe(*merged, (ctx[0].shape[0], lat[0].shape[0]), mod, w1[l], w3[l], w2[l], dec_l // tm, tm)
        xp, xs = yp.reshape(x_prompt.shape), ys.reshape(x_sample.shape)
    return (xp, xs, jnp.stack(new_k, axis=1), jnp.stack(new_v, axis=1))
```

```python
import functools
import math

import jax
import jax.numpy as jnp
from jax import lax
from jax.experimental import pallas as pl
from jax.experimental.pallas import tpu as pltpu

D_MODEL = 1024
GRID_W = 64
N_HEADS = 8
QK_DIM = 64
V_DIM = 2 * QK_DIM
AXIS_DIM = QK_DIM // 2
ROPE_BASE = 10000.0
N_GROUPS = 4
EXPERTS_PER_GROUP = 4
N_EXPERTS = N_GROUPS * EXPERTS_PER_GROUP
D_EXPERT = 512
EPS = 1e-6

LANES = 128
MXU_DIM = 256
VMEM_LIMIT_BYTES = 56 * 1024 * 1024
ROUTER_LANE0 = N_GROUPS
CONV_TOKEN_TILE = 1024
MOE_TOKEN_TILE = 512
MOE_ROWS = 512
META_ROWS = 8
DMA_UNROLL = 512
ATTN_Q_ROWS = 256
ATTN_UNITS_PER_STEP = 8
ATTN_SCORES_AHEAD = 1

F32 = jnp.float32
BF16 = jnp.bfloat16


def _dot(a, b):
    return jnp.dot(a, b, preferred_element_type=F32)


def _dot_nt(a, b):
    return lax.dot_general(a, b, (((1,), (1,)), ((), ())), preferred_element_type=F32)


def _split_bf16(x):
    hi = x.astype(BF16)
    lo = (x - hi.astype(F32)).astype(BF16)
    return hi, lo


def _sigmoid(x):
    return 1.0 / (1.0 + jnp.exp(-x))


def _rms(x):
    return x * lax.rsqrt(jnp.mean(x * x, axis=-1, keepdims=True) + EPS)


def _params(*sem):
    return pltpu.CompilerParams(dimension_semantics=sem, vmem_limit_bytes=VMEM_LIMIT_BYTES)


def _ada_kernel(c_ref, w_ref, b_ref, o_ref):
    c = c_ref[...]
    s = c * _sigmoid(c)
    s_hi, s_lo = _split_bf16(s)
    w_hi, w_lo = _split_bf16(w_ref[...])
    o_ref[...] = _dot(s_hi, w_hi) + _dot(s_lo, w_hi) + _dot(s_hi, w_lo) + b_ref[...]


def _ada(cvec, w_ada, b_ada):
    rows, d = cvec.shape
    n = w_ada.shape[1]
    tn = 1024
    return pl.pallas_call(
        _ada_kernel,
        grid=(n // tn,),
        in_specs=[pl.BlockSpec((rows, d), lambda j: (0, 0)),
                  pl.BlockSpec((d, tn), lambda j: (0, j)),
                  pl.BlockSpec((1, tn), lambda j: (0, j))],
        out_specs=pl.BlockSpec((rows, tn), lambda j: (0, j)),
        out_shape=jax.ShapeDtypeStruct((rows, n), F32),
        compiler_params=_params("arbitrary"),
        name="ada",
    )(cvec, w_ada, b_ada)


def _norm_mod(x, g, scale, shift):
    return (_rms(x) * g) * (1.0 + scale) + shift


def _conv_kernel(x_ref, mod_ref, g1_ref, w_ref, cw_ref, o_ref, *, seq_len):
    x = x_ref[...]
    tm = x.shape[0]
    hb = _norm_mod(x, g1_ref[...], mod_ref[0, 1:2, :], mod_ref[0, 0:1, :]).astype(BF16)
    pos = lax.broadcasted_iota(jnp.int32, (tm, 1), 0) % seq_len
    first = pos == 0
    last = pos == seq_len - 1
    for c in range(D_MODEL // MXU_DIM):
        sl = slice(c * MXU_DIM, (c + 1) * MXU_DIM)
        cb, cc, cx = (_dot(hb, w_ref[:, pl.ds(j * D_MODEL + c * MXU_DIM, MXU_DIM)]) for j in range(3))
        u = cc * cx
        up = jnp.where(first, 0.0, pltpu.roll(u, 1, 0))
        dn = jnp.where(last, 0.0, pltpu.roll(u, tm - 1, 0))
        cw = cw_ref[:, sl]
        conv = cw[0:1] * up + cw[1:2] * u + cw[2:3] * dn
        o_ref[:, sl] = (cb * conv).astype(BF16)


def _chunk_norm(z, bd, g):
    sq = (z * z).astype(BF16)
    ss = jnp.concatenate(
        [_dot(sq[:, s * MXU_DIM:(s + 1) * MXU_DIM], bd) for s in range(z.shape[1] // MXU_DIM)], axis=1)
    return z * lax.rsqrt(ss * (1.0 / QK_DIM) + EPS) * g


def _rope(z, cos, sin, hi_half):
    n = z.shape[1]
    swap = jnp.where(hi_half, pltpu.roll(z, AXIS_DIM // 2, 1), pltpu.roll(z, n - AXIS_DIM // 2, 1))
    return z * cos + swap * sin


def _qkv_kernel(*refs, latent):
    if latent:
        x_ref, mod_ref, g1_ref, w_ref, bd_ref, gq_ref, gk_ref, cos_ref, sin_ref, q_ref, k_ref, v_ref = refs
    else:
        x_ref, mod_ref, g1_ref, w_ref, wkt_ref, bd_ref, gq_ref, gkt_ref, q_ref, kt_ref, v_ref, knt_ref, vn_ref = refs
    hb = _norm_mod(x_ref[...], g1_ref[...], mod_ref[0, 1:2, :], mod_ref[0, 0:1, :]).astype(BF16)
    tm = hb.shape[0]
    bd = bd_ref[...]
    q = _chunk_norm(_dot(hb, w_ref[:, 0:D_MODEL]), bd, gq_ref[...])
    if latent:
        k = _chunk_norm(_dot(hb, w_ref[:, D_MODEL:2 * D_MODEL]), bd, gk_ref[...])
    v = _dot(hb, w_ref[:, 2 * D_MODEL:3 * D_MODEL])
    if latent:
        reps = D_MODEL // LANES
        cos = jnp.concatenate([cos_ref[...]] * reps, axis=1)
        sin = jnp.concatenate([sin_ref[...]] * reps, axis=1)
        lane = lax.broadcasted_iota(jnp.int32, (1, D_MODEL), 1)
        hi_half = (lane & (AXIS_DIM // 2)) != 0
        q = _rope(q, cos, sin, hi_half)
        k_att = _rope(k, cos, sin, hi_half)
    else:
        kt = _dot_nt(wkt_ref[...], hb)
        gkt = jnp.concatenate([gkt_ref[...]] * (tm // LANES), axis=1)
        slabs = []
        for s in range(D_MODEL // MXU_DIM):
            z = kt[s * MXU_DIM:(s + 1) * MXU_DIM, :]
            ss = _dot(bd, (z * z).astype(BF16))
            slabs.append(z * lax.rsqrt(ss * (1.0 / QK_DIM) + EPS))
        knt = jnp.concatenate(slabs, axis=0) * gkt
        seq = knt_ref.shape[2]
        for j in range(tm // seq):
            knt_ref[j] = knt[:, j * seq:(j + 1) * seq]
            kt_ref[j] = knt[:, j * seq:(j + 1) * seq].astype(BF16)
        vn_ref[...] = v
    q_ref[...] = (q * (math.log2(math.e) / math.sqrt(QK_DIM))).astype(BF16)
    if latent:
        k_ref[...] = k_att.astype(BF16)
    v_ref[...] = v.astype(BF16)


def _attn_kernel(*refs, latent, lam_init, tq):
    if latent:
        lamv_ref, q_ref, k_ref, v_ref, ck_ref, cv_ref, sg_ref, o_ref = refs
    else:
        lamv_ref, q_ref, k_ref, v_ref, sg_ref, o_ref = refs
    lv = lamv_ref[...]
    lam = (jnp.exp(jnp.sum(lv[0:1] * lv[1:2], axis=1, keepdims=True))
           - jnp.exp(jnp.sum(lv[2:3] * lv[3:4], axis=1, keepdims=True)) + lam_init)
    lane = lax.broadcasted_iota(jnp.int32, (1, V_DIM), 1)
    n_q = q_ref.shape[1]
    units = [(hh, c) for hh in range(q_ref.shape[2] // V_DIM) for c in range(n_q // tq)]

    def kv(hh):
        hs = slice(hh * V_DIM, (hh + 1) * V_DIM)
        vals = [v_ref[0, :, hs]]
        if latent:
            keys = [(k_ref[0, :, hs], False), (ck_ref[0, hs, :].astype(BF16), True)]
            vals.append(cv_ref[0, :, hs].astype(BF16))
        else:
            keys = [(k_ref[0, hs, :], True)]
        return keys, vals

    def scores(u):
        hh, c = units[u]
        q = q_ref[0, c * tq:(c + 1) * tq, hh * V_DIM:(hh + 1) * V_DIM]
        keys, _ = kv(hh)
        halves = (jnp.where(lane < QK_DIM, q, jnp.zeros_like(q)), jnp.where(lane >= QK_DIM, q, jnp.zeros_like(q)))
        return [[_dot(qh, kk) if feature_major else _dot_nt(qh, kk) for kk, feature_major in keys] for qh in halves]

    ones_col = jnp.where(lane == 0, 1.0, 0.0).astype(BF16)

    def softmax_pv(s, vals):
        m = s[0].max(axis=1, keepdims=True)
        for si in s[1:]:
            m = jnp.maximum(m, si.max(axis=1, keepdims=True))
        acc = None
        for si, vv in zip(s, vals):
            v_aug = jnp.concatenate([vv, jnp.broadcast_to(ones_col, vv.shape)], axis=1)
            part = _dot(jnp.exp2(si - m).astype(BF16), v_aug)
            acc = part if acc is None else acc + part
        return acc[:, :V_DIM] * (1.0 / acc[:, V_DIM:V_DIM + 1])

    def finish(u, s):
        hh, c = units[u]
        _, vals = kv(hh)
        o = softmax_pv(s[0], vals) - lam * softmax_pv(s[1], vals)
        o_ref[0, c * tq:(c + 1) * tq, hh * V_DIM:(hh + 1) * V_DIM] = (
            _rms(o) * sg_ref[...] * (1.0 - lam_init)).astype(BF16)

    ahead = ATTN_SCORES_AHEAD
    pending = [scores(u) for u in range(min(ahead, len(units)))]
    for u in range(len(units)):
        if u + ahead < len(units):
            pending.append(scores(u + ahead))
        finish(u, pending.pop(0))


def _merge_kernel(x_ref, mod_ref, g1_ref, yc_ref, oa_ref, wg_ref, wco_ref, wao_ref, wo_ref, g2_ref,
                  wrh_ref, wrl_ref, br_ref, tri_ref, x1_ref, h2_ref, route_ref, meta_ref, cnt_ref):
    x = x_ref[...]
    hb = _norm_mod(x, g1_ref[...], mod_ref[0, 1:2, :], mod_ref[0, 0:1, :]).astype(BF16)
    y_conv = _dot(yc_ref[...], wco_ref[...])
    y_attn = _dot(oa_ref[...], wao_ref[...])
    mixed = (_sigmoid(_dot(hb, wg_ref[:, 0:D_MODEL])) * y_conv
             + _sigmoid(_dot(hb, wg_ref[:, D_MODEL:2 * D_MODEL])) * y_attn)
    x1 = x + mod_ref[0, 2:3, :] * _dot(mixed.astype(BF16), wo_ref[...])
    x1_ref[...] = x1
    h2 = _norm_mod(x1, g2_ref[...], mod_ref[0, 4:5, :], mod_ref[0, 3:4, :])
    h2_ref[...] = h2

    h_hi, h_lo = _split_bf16(h2)
    logits = _dot(h_hi, wrh_ref[...]) + _dot(h_lo, wrh_ref[...]) + _dot(h_hi, wrl_ref[...]) + br_ref[...]
    tm = x.shape[0]
    lane = lax.broadcasted_iota(jnp.int32, (tm, LANES), 1).astype(F32)
    ninf = -jnp.inf
    big = float(LANES)

    def top(vals):
        vmax = vals.max(axis=1, keepdims=True)
        idx = jnp.where(vals == vmax, lane, big).min(axis=1, keepdims=True)
        return vmax, idx

    is_grp = lane < N_GROUPS
    gmax, gidx = top(jnp.where(is_grp, logits, ninf))
    g_p = 1.0 / jnp.where(is_grp, jnp.exp(logits - gmax), 0.0).sum(axis=1, keepdims=True)
    lo = ROUTER_LANE0 + EXPERTS_PER_GROUP * gidx
    el = jnp.where((lane >= lo) & (lane < lo + EXPERTS_PER_GROUP), logits, ninf)
    v1, i1 = top(el)
    v2, i2 = top(jnp.where(lane == i1, ninf, el))
    e2 = jnp.exp(v2 - v1)
    den = 1.0 / (1.0 + e2)
    e0 = i1 - ROUTER_LANE0
    e1 = i2 - ROUTER_LANE0
    onehot = jnp.where((lane == e0) | (lane == e1), 1.0, 0.0)
    before = _dot(tri_ref[...], onehot.astype(BF16))
    r0 = jnp.where(lane == e0, before, 0.0).sum(axis=1, keepdims=True)
    r1 = jnp.where(lane == e1, before, 0.0).sum(axis=1, keepdims=True)
    route = (jnp.where(lane == 0.0, e0, 0.0) + jnp.where(lane == 1.0, e1, 0.0)
             + jnp.where(lane == 2.0, den * g_p, 0.0) + jnp.where(lane == 3.0, e2 * den * g_p, 0.0)
             + jnp.where(lane == 4.0, r0, 0.0) + jnp.where(lane == 5.0, r1, 0.0))
    route_ref[...] = route
    meta_ref[...] = jnp.transpose(route)[:META_ROWS, :]
    cnt_ref[0] = jnp.broadcast_to(onehot.sum(axis=0, keepdims=True), (META_ROWS, LANES))


def _row_copy(src, dst, sem):
    return pltpu.make_async_copy(src, dst, sem)


def _for_groups(n_groups, body):
    if n_groups == 1:
        body(0, 0)
    else:
        lax.fori_loop(0, n_groups, body, 0)


def _dispatch_kernel(pos_ref, last_ref, nu_ref, h2p_ref, h2s_ref, xs_ref, zero_buf, sem, *, n_p, n_tok, n_tiles):
    i = pl.program_id(0)
    tm = h2p_ref.shape[0]

    @pl.when(i == 0)
    def _():
        zero_buf[...] = jnp.zeros_like(zero_buf)

        def zero_tile(row):
            return _row_copy(zero_buf, xs_ref.at[pl.ds(pl.multiple_of(row, MOE_ROWS), MOE_ROWS)], sem)

        for e in range(N_EXPERTS):
            @pl.when(last_ref[e] >= 0)
            def _():
                zero_tile(last_ref[e]).start()

        def start_tail(j, carry):
            zero_tile(j * MOE_ROWS).start()
            return carry

        lax.fori_loop(nu_ref[0], n_tiles, start_tail, 0)
        for e in range(N_EXPERTS):
            @pl.when(last_ref[e] >= 0)
            def _():
                zero_tile(0).wait()

        def wait_tail(j, carry):
            zero_tile(0).wait()
            return carry

        lax.fori_loop(nu_ref[0], n_tiles, wait_tail, 0)

    base = i * tm

    def scatter(h2_ref):
        def issue(g, carry):
            for u in range(DMA_UNROLL):
                r = g * DMA_UNROLL + u
                for slot in range(2):
                    p = pos_ref[slot * n_tok + base + r]
                    _row_copy(h2_ref.at[pl.ds(r, 1)], xs_ref.at[pl.ds(p, 1)], sem).start(priority=slot)
            return carry

        _for_groups(tm // DMA_UNROLL, issue)
        for slot in range(2):
            _row_copy(h2_ref, xs_ref.at[pl.ds(0, tm)], sem).wait()

    @pl.when(i < n_p)
    def _():
        scatter(h2p_ref)

    @pl.when(i >= n_p)
    def _():
        scatter(h2s_ref)


def _expert_kernel(te_ref, nu_ref, x_ref, w1_ref, w3_ref, w2_ref, y_ref, w1b, w3b, w2b):
    j = pl.program_id(0)

    @pl.when((j == 0) | (te_ref[j] != te_ref[jnp.maximum(j - 1, 0)]))
    def _():
        w1b[...] = w1_ref[0].astype(BF16)
        w3b[...] = w3_ref[0].astype(BF16)
        w2b[...] = w2_ref[0].astype(BF16)

    @pl.when(j < nu_ref[0])
    def _():
        t = x_ref[...].astype(BF16)
        a = _dot(t, w1b[...])
        he = (a * _sigmoid(a)) * _dot(t, w3b[...])
        y_ref[...] = _dot(he.astype(BF16), w2b[...])

    @pl.when(j >= nu_ref[0])
    def _():
        y_ref[...] = jnp.zeros_like(y_ref)


def _combine_kernel(pos_ref, x1p_ref, x1s_ref, rp_ref, rs_ref, mod_ref, ys_ref, op_ref, os_ref, ybuf, sem,
                    *, n_p, n_tok):
    i = pl.program_id(0)
    tm = ybuf.shape[1]
    base = i * tm

    def issue(g, carry):
        for u in range(DMA_UNROLL):
            r = g * DMA_UNROLL + u
            for slot in range(2):
                p = pos_ref[slot * n_tok + base + r]
                _row_copy(ys_ref.at[pl.ds(p, 1)], ybuf.at[slot, pl.ds(r, 1)], sem).start(priority=slot)
        return carry

    _for_groups(tm // DMA_UNROLL, issue)
    for slot in range(2):
        _row_copy(ybuf.at[slot], ybuf.at[slot], sem).wait()

    def out(x1_ref, r_ref, o_ref):
        r = r_ref[...]
        o_ref[...] = x1_ref[...] + mod_ref[0, 5:6, :] * (r[:, 2:3] * ybuf[0] + r[:, 3:4] * ybuf[1])

    @pl.when(i < n_p)
    def _():
        out(x1p_ref, rp_ref, op_ref)

    @pl.when(i >= n_p)
    def _():
        out(x1s_ref, rs_ref, os_ref)


def _layer(x3, mod, row0, latent, cache, wts, lam_init):
    b, l, d = x3.shape
    t = b * l
    x = x3.reshape(t, d)
    tm = MOE_TOKEN_TILE
    nt = t // tm
    per_seq = l // tm if l >= tm else 1
    if latent:
        mod_row = lambda i: row0 + i // per_seq
    else:
        mod_row = lambda i: row0

    x_spec = pl.BlockSpec((tm, d), lambda i: (i, 0))
    mod_spec = pl.BlockSpec((1, 6, d), lambda i: (mod_row(i), 0, 0))
    row_spec = pl.BlockSpec((1, d), lambda i: (0, 0))
    act_spec = pl.BlockSpec((tm, d), lambda i: (i, 0))

    def full(a):
        nd = a.ndim
        return pl.BlockSpec(a.shape, lambda i: (0,) * nd)

    tmc = max(CONV_TOKEN_TILE, l)
    assert tmc % l == 0 and t % tmc == 0
    seq_per_tile = tmc // l
    xc_spec = pl.BlockSpec((tmc, d), lambda i: (i, 0))
    modc_spec = pl.BlockSpec((1, 6, d), lambda i: (row0 + (i * seq_per_tile if latent else 0), 0, 0))
    assert not latent or seq_per_tile == 1
    yc = pl.pallas_call(
        functools.partial(_conv_kernel, seq_len=l),
        grid=(t // tmc,),
        in_specs=[xc_spec, modc_spec, row_spec, full(wts["w_conv_in"]), full(wts["conv_w"])],
        out_specs=xc_spec,
        out_shape=jax.ShapeDtypeStruct((t, d), BF16),
        compiler_params=_params("parallel"),
        name="conv_in",
    )(x, mod, wts["norm1_g"], wts["w_conv_in"], wts["conv_w"])

    if latent:
        qkv_in = [x, mod, wts["norm1_g"], wts["w_qkv"], wts["bd"], wts["gq"], wts["gk"], wts["cos"], wts["sin"]]
        tab_spec = pl.BlockSpec((tm, LANES), lambda i: (i % per_seq, 0))
        qkv_specs = [x_spec, mod_spec, row_spec, full(wts["w_qkv"]), full(wts["bd"]), row_spec, row_spec,
                     tab_spec, tab_spec]
        out_shapes = [jax.ShapeDtypeStruct((t, d), BF16)] * 3
        out_specs = [act_spec] * 3
    else:
        assert tm % l == 0
        kt_spec = pl.BlockSpec((tm // l, d, l), lambda i: (i, 0, 0))
        qkv_in = [x, mod, wts["norm1_g"], wts["w_qkv"], wts["w_kt"], wts["bd"], wts["gq"], wts["gkt"]]
        qkv_specs = [x_spec, mod_spec, row_spec, full(wts["w_qkv"]), full(wts["w_kt"]), full(wts["bd"]), row_spec,
                     full(wts["gkt"])]
        out_shapes = [jax.ShapeDtypeStruct((t, d), BF16), jax.ShapeDtypeStruct((b, d, l), BF16),
                      jax.ShapeDtypeStruct((t, d), BF16), jax.ShapeDtypeStruct((b, d, l), F32),
                      jax.ShapeDtypeStruct((t, d), F32)]
        out_specs = [act_spec, kt_spec, act_spec, kt_spec, act_spec]
    qkv_out = pl.pallas_call(
        functools.partial(_qkv_kernel, latent=latent),
        grid=(nt,),
        in_specs=qkv_specs,
        out_specs=out_specs,
        out_shape=out_shapes,
        compiler_params=_params("parallel"),
        name="qkv",
    )(*qkv_in)
    q, k, v = qkv_out[:3]

    tq = min(ATTN_Q_ROWS, l)
    heads_per_step = max(1, ATTN_UNITS_PER_STEP // (l // tq))
    hw = heads_per_step * V_DIM
    q3, v3 = q.reshape(b, l, d), v.reshape(b, l, d)
    qkv_spec = pl.BlockSpec((1, l, hw), lambda bi, h: (bi, 0, h))
    kt_blk = lambda n_tok: pl.BlockSpec((1, hw, n_tok), lambda bi, h: (bi, h, 0))
    sg_spec = pl.BlockSpec((1, V_DIM), lambda bi, h: (0, 0))
    lam_spec = pl.BlockSpec(wts["lamv"].shape, lambda bi, h: (0, 0))
    if latent:
        ckt, cv = cache
        attn_in = [wts["lamv"], q3, k.reshape(b, l, d), v3, ckt, cv]
        attn_specs = [lam_spec, qkv_spec, qkv_spec, qkv_spec, kt_blk(ckt.shape[2]),
                      pl.BlockSpec((1, cv.shape[1], hw), lambda bi, h: (bi, 0, h))]
    else:
        attn_in = [wts["lamv"], q3, k, v3]
        attn_specs = [lam_spec, qkv_spec, kt_blk(l), qkv_spec]
    attn_in.append(wts["subln_g"])
    attn_specs.append(sg_spec)
    oa = pl.pallas_call(
        functools.partial(_attn_kernel, latent=latent, lam_init=lam_init, tq=tq),
        grid=(b, N_HEADS // heads_per_step),
        in_specs=attn_specs,
        out_specs=qkv_spec,
        out_shape=jax.ShapeDtypeStruct((b, l, d), BF16),
        compiler_params=_params("parallel", "parallel"),
        name="attn",
    )(*attn_in).reshape(t, d)

    lane_spec = pl.BlockSpec((tm, LANES), lambda i: (i, 0))
    x1, h2, route, meta, counts = pl.pallas_call(
        _merge_kernel,
        grid=(nt,),
        in_specs=[x_spec, mod_spec, row_spec, act_spec, act_spec, full(wts["w_gate"]), full(wts["w_conv_out"]),
                  full(wts["w_attn_out"]), full(wts["w_o"]), row_spec, full(wts["wr_hi"]), full(wts["wr_lo"]),
                  full(wts["b_router"]), full(wts["tri"])],
        out_specs=[x_spec, x_spec, lane_spec, pl.BlockSpec((META_ROWS, tm), lambda i: (0, i)),
                   pl.BlockSpec((1, META_ROWS, LANES), lambda i: (i, 0, 0))],
        out_shape=[jax.ShapeDtypeStruct((t, d), F32), jax.ShapeDtypeStruct((t, d), F32),
                   jax.ShapeDtypeStruct((t, LANES), F32), jax.ShapeDtypeStruct((META_ROWS, t), F32),
                   jax.ShapeDtypeStruct((nt, META_ROWS, LANES), F32)],
        compiler_params=_params("parallel"),
        name="merge",
    )(x, mod, wts["norm1_g"], yc, oa, wts["w_gate"], wts["w_conv_out"], wts["w_attn_out"], wts["w_o"],
      wts["norm2_g"], wts["wr_hi"], wts["wr_lo"], wts["b_router"], wts["tri"])

    outs = [x1, h2, route, meta, counts[:, 0, :N_EXPERTS]]
    if not latent:
        outs += [qkv_out[3], qkv_out[4]]
    return outs


def _routing_tables(meta, counts, tm, n_tiles):
    eid = meta[0:2].astype(jnp.int32)
    rank = meta[4:6].astype(jnp.int32)
    counts = counts.astype(jnp.int32)
    totals = counts.sum(axis=0)
    tiles_e = (totals + MOE_ROWS - 1) // MOE_ROWS
    tile_end = jnp.cumsum(tiles_e)
    row_start = (tile_end - tiles_e) * MOE_ROWS
    base = row_start[None, :] + jnp.cumsum(counts, axis=0) - counts
    base_tok = jnp.repeat(base.T, tm, axis=1)
    ex = jnp.arange(N_EXPERTS, dtype=jnp.int32)[None, :, None]
    pos = jnp.sum(jnp.where(eid[:, None, :] == ex, base_tok[None], 0), axis=1) + rank
    n_used = tile_end[-1]
    tile_ids = jnp.minimum(jnp.arange(n_tiles, dtype=jnp.int32), n_used - 1)
    tile_expert = jnp.sum((tile_ids[:, None] >= tile_end[None, :]).astype(jnp.int32), axis=1)
    last_tile_row = jnp.where(tiles_e > 0, (tile_end - 1) * MOE_ROWS, -1)
    return (pos.reshape(-1), tile_expert.astype(jnp.int32), n_used.astype(jnp.int32).reshape(1),
            last_tile_row.astype(jnp.int32))


def _routed_moe(streams, mod, w1, w3, w2, per_seq_tiles, tm):
    (x1p, h2p, rp, _, _), (x1s, h2s, rs, _, _) = streams
    d = x1p.shape[1]
    tp, ts = x1p.shape[0], x1s.shape[0]
    t = tp + ts
    n_tiles = 2 * t // MOE_ROWS + N_EXPERTS
    meta = jnp.concatenate([s[3] for s in streams], axis=1)
    counts = jnp.concatenate([s[4] for s in streams], axis=0)
    pos, tile_expert, n_used, last_tile_row = _routing_tables(meta, counts, tm, n_tiles)

    sorted_shape = jax.ShapeDtypeStruct((n_tiles * MOE_ROWS, d), F32)
    any_spec = pl.BlockSpec(memory_space=pl.ANY)
    n_p = tp // tm
    x_sorted = pl.pallas_call(
        functools.partial(_dispatch_kernel, n_p=n_p, n_tok=t, n_tiles=n_tiles),
        grid_spec=pltpu.PrefetchScalarGridSpec(
            num_scalar_prefetch=3, grid=(t // tm,),
            in_specs=[pl.BlockSpec((tm, d), lambda i, *_: (jnp.minimum(i, n_p - 1), 0)),
                      pl.BlockSpec((tm, d), lambda i, *_: (jnp.maximum(i - n_p, 0), 0))],
            out_specs=any_spec,
            scratch_shapes=[pltpu.VMEM((MOE_ROWS, d), F32), pltpu.SemaphoreType.DMA]),
        out_shape=sorted_shape,
        compiler_params=_params("arbitrary"),
        name="dispatch",
    )(pos, last_tile_row, n_used, h2p, h2s)

    def row_blk(j, te, nu):
        return (jnp.minimum(j, nu[0] - 1), 0)

    y_sorted = pl.pallas_call(
        _expert_kernel,
        grid_spec=pltpu.PrefetchScalarGridSpec(
            num_scalar_prefetch=2,
            grid=(n_tiles,),
            in_specs=[pl.BlockSpec((MOE_ROWS, d), row_blk),
                      pl.BlockSpec((1, d, D_EXPERT), lambda j, te, nu: (te[j], 0, 0)),
                      pl.BlockSpec((1, d, D_EXPERT), lambda j, te, nu: (te[j], 0, 0)),
                      pl.BlockSpec((1, D_EXPERT, d), lambda j, te, nu: (te[j], 0, 0))],
            out_specs=pl.BlockSpec((MOE_ROWS, d), lambda j, te, nu: (j, 0)),
            scratch_shapes=[pltpu.VMEM((d, D_EXPERT), BF16), pltpu.VMEM((d, D_EXPERT), BF16),
                            pltpu.VMEM((D_EXPERT, d), BF16)]),
        out_shape=jax.ShapeDtypeStruct((n_tiles * MOE_ROWS, d), F32),
        compiler_params=_params("arbitrary"),
        name="experts",
    )(tile_expert, n_used, x_sorted, w1, w3, w2)

    ctx_blk = lambda i, p: (jnp.minimum(i, n_p - 1), 0)
    lat_blk = lambda i, p: (jnp.maximum(i - n_p, 0), 0)
    return pl.pallas_call(
        functools.partial(_combine_kernel, n_p=n_p, n_tok=t),
        grid_spec=pltpu.PrefetchScalarGridSpec(
            num_scalar_prefetch=1, grid=(t // tm,),
            in_specs=[pl.BlockSpec((tm, d), ctx_blk), pl.BlockSpec((tm, d), lat_blk),
                      pl.BlockSpec((tm, LANES), ctx_blk), pl.BlockSpec((tm, LANES), lat_blk),
                      pl.BlockSpec((1, 6, d),
                                   lambda i, p: (jnp.where(i < n_p, 0, 1 + (i - n_p) // per_seq_tiles), 0, 0)),
                      any_spec],
            out_specs=[pl.BlockSpec((tm, d), ctx_blk), pl.BlockSpec((tm, d), lat_blk)],
            scratch_shapes=[pltpu.VMEM((2, tm, d), F32), pltpu.SemaphoreType.DMA]),
        out_shape=[jax.ShapeDtypeStruct((tp, d), F32), jax.ShapeDtypeStruct((ts, d), F32)],
        compiler_params=_params("arbitrary"),
        name="combine",
    )(pos, x1p, x1s, rp, rs, mod, y_sorted)


def _rope_tables(n_tok):
    rows = n_tok // GRID_W
    pos_row = jnp.repeat(jnp.arange(rows, dtype=F32), GRID_W)
    pos_col = jnp.tile(jnp.arange(GRID_W, dtype=F32), rows)
    inv_freq = 1.0 / (ROPE_BASE ** (jnp.arange(0, AXIS_DIM, 2, dtype=F32) / AXIS_DIM))
    ang = jnp.concatenate([pos_row[:, None] * inv_freq[None, :]] * 2
                          + [pos_col[:, None] * inv_freq[None, :]] * 2, axis=1)
    sign = jnp.tile(jnp.concatenate([-jnp.ones((AXIS_DIM // 2,), F32), jnp.ones((AXIS_DIM // 2,), F32)]), 2)
    cos = jnp.cos(ang)
    sin = jnp.sin(ang) * sign[None, :]
    return jnp.tile(cos, (1, LANES // QK_DIM)), jnp.tile(sin, (1, LANES // QK_DIM))


def kernel(x_prompt, x_sample, cache_k, cache_v, c, c_ctx, w_ada, b_ada, norm1_g, w_in, conv_w, w_conv_out,
           q_norm_g, k_norm_g, lambda_q1, lambda_k1, lambda_q2, lambda_k2, subln_g, w_attn_out, w_o, norm2_g,
           w_grp, b_grp, w_exp, b_exp, w1, w3, w2):
    depth = w_in.shape[0]
    assert depth == 1
    d = D_MODEL
    dec_b, dec_l = x_sample.shape[0], x_sample.shape[1]
    cos, sin = _rope_tables(dec_l)
    bd = (jnp.arange(MXU_DIM)[:, None] // QK_DIM == jnp.arange(MXU_DIM)[None, :] // QK_DIM).astype(BF16)
    tri = (jnp.arange(MOE_TOKEN_TILE)[:, None] > jnp.arange(MOE_TOKEN_TILE)[None, :]).astype(BF16)
    xp, xs = x_prompt, x_sample
    new_k, new_v = [], []
    for l in range(depth):
        lam_init = 0.8 - 0.6 * math.exp(-0.3 * l)
        mod_rows = 16
        cvec = jnp.concatenate([c_ctx[None, :], c, jnp.zeros((mod_rows - 1 - dec_b, d), F32)], axis=0)
        mod = _ada(cvec, w_ada[l], b_ada[l][None, :]).reshape(mod_rows, 6, d)

        wi = w_in[l].astype(BF16)
        w_router = jnp.concatenate([w_grp[l], w_exp[l]], axis=1)
        w_router = jnp.pad(w_router, ((0, 0), (0, LANES - w_router.shape[1])))
        wr_hi = w_router.astype(BF16)
        wr_lo = (w_router - wr_hi.astype(F32)).astype(BF16)
        b_router = jnp.pad(jnp.concatenate([b_grp[l], b_exp[l]]), (0, LANES - N_GROUPS - N_EXPERTS))[None, :]
        wts = dict(
            norm1_g=norm1_g[l][None, :], norm2_g=norm2_g[l][None, :],
            w_conv_in=wi[:, :3 * d], w_qkv=wi[:, 3 * d:6 * d], w_gate=wi[:, 6 * d:],
            conv_w=conv_w[l], w_conv_out=w_conv_out[l].astype(BF16),
            gq=jnp.tile(q_norm_g[l], d // QK_DIM)[None, :], gk=jnp.tile(k_norm_g[l], d // QK_DIM)[None, :],
            w_kt=wi[:, 4 * d:5 * d].T, gkt=jnp.broadcast_to(jnp.tile(k_norm_g[l], d // QK_DIM)[:, None], (d, LANES)),
            bd=bd, cos=cos, sin=sin,
            lamv=jnp.stack([lambda_q1[l], lambda_k1[l], lambda_q2[l], lambda_k2[l]]),
            subln_g=subln_g[l][None, :], w_attn_out=w_attn_out[l].astype(BF16), w_o=w_o[l].astype(BF16),
            wr_hi=wr_hi, wr_lo=wr_lo, b_router=b_router, tri=tri,
        )
        *ctx, k_ctx, v_ctx = _layer(xp, mod, 0, False, None, wts, lam_init)
        new_k.append(k_ctx.reshape(x_prompt.shape[0], N_HEADS, 2, QK_DIM, x_prompt.shape[1]).transpose(0, 4, 1, 2, 3))
        new_v.append(v_ctx.reshape(x_prompt.shape[0], x_prompt.shape[1], N_HEADS, V_DIM))
        past = cache_k.shape[2]
        cache = (cache_k[:, l].transpose(0, 2, 3, 4, 1).reshape(dec_b, d, past), cache_v[:, l].reshape(dec_b, past, d))
        lat = _layer(xs, mod, 1, True, cache, wts, lam_init)
        tm = MOE_TOKEN_TILE
        yp, ys = _routed_moe((ctx, lat), mod, w1[l], w3[l], w2[l], dec_l // tm, tm)
        xp, xs = yp.reshape(x_prompt.shape), ys.reshape(x_sample.shape)
    return (xp, xs, jnp.stack(new_k, axis=1), jnp.stack(new_v, axis=1))
```

```python
import functools
import math

import jax
import jax.numpy as jnp
from jax import lax
from jax.experimental import pallas as pl
from jax.experimental.pallas import tpu as pltpu

D_MODEL = 1024
GRID_W = 64
N_HEADS = 8
QK_DIM = 64
V_DIM = 2 * QK_DIM
AXIS_DIM = QK_DIM // 2
ROPE_BASE = 10000.0
N_GROUPS = 4
EXPERTS_PER_GROUP = 4
N_EXPERTS = N_GROUPS * EXPERTS_PER_GROUP
D_EXPERT = 512
EPS = 1e-6

LANES = 128
MXU_DIM = 256
VMEM_LIMIT_BYTES = 56 * 1024 * 1024
ROUTER_LANE0 = N_GROUPS
CONV_TOKEN_TILE = 1024
MOE_TOKEN_TILE = 512
MOE_ROWS = 512
META_ROWS = 8
DMA_UNROLL = 512
ATTN_Q_ROWS = 256
ATTN_UNITS_PER_STEP = 8
ATTN_SCORES_AHEAD = 1

F32 = jnp.float32
BF16 = jnp.bfloat16


def _dot(a, b):
    return jnp.dot(a, b, preferred_element_type=F32)


def _dot_nt(a, b):
    return lax.dot_general(a, b, (((1,), (1,)), ((), ())), preferred_element_type=F32)


def _split_bf16(x):
    hi = x.astype(BF16)
    lo = (x - hi.astype(F32)).astype(BF16)
    return hi, lo


def _sigmoid(x):
    return 1.0 / (1.0 + jnp.exp(-x))


def _rms(x):
    return x * lax.rsqrt(jnp.mean(x * x, axis=-1, keepdims=True) + EPS)


def _params(*sem):
    return pltpu.CompilerParams(dimension_semantics=sem, vmem_limit_bytes=VMEM_LIMIT_BYTES)


def _ada_kernel(c_ref, w_ref, b_ref, o_ref):
    c = c_ref[...]
    s = c * _sigmoid(c)
    s_hi, s_lo = _split_bf16(s)
    w_hi, w_lo = _split_bf16(w_ref[...])
    o_ref[...] = _dot(s_hi, w_hi) + _dot(s_lo, w_hi) + _dot(s_hi, w_lo) + b_ref[...]


def _ada(cvec, w_ada, b_ada):
    rows, d = cvec.shape
    n = w_ada.shape[1]
    tn = 1024
    return pl.pallas_call(
        _ada_kernel,
        grid=(n // tn,),
        in_specs=[pl.BlockSpec((rows, d), lambda j: (0, 0)),
                  pl.BlockSpec((d, tn), lambda j: (0, j)),
                  pl.BlockSpec((1, tn), lambda j: (0, j))],
        out_specs=pl.BlockSpec((rows, tn), lambda j: (0, j)),
        out_shape=jax.ShapeDtypeStruct((rows, n), F32),
        compiler_params=_params("arbitrary"),
        name="ada",
    )(cvec, w_ada, b_ada)


def _norm_mod(x, g, scale, shift):
    return (_rms(x) * g) * (1.0 + scale) + shift


def _conv_kernel(x_ref, mod_ref, g1_ref, w_ref, cw_ref, o_ref, *, seq_len):
    x = x_ref[...]
    tm = x.shape[0]
    hb = _norm_mod(x, g1_ref[...], mod_ref[0, 1:2, :], mod_ref[0, 0:1, :]).astype(BF16)
    pos = lax.broadcasted_iota(jnp.int32, (tm, 1), 0) % seq_len
    first = pos == 0
    last = pos == seq_len - 1
    for c in range(D_MODEL // MXU_DIM):
        sl = slice(c * MXU_DIM, (c + 1) * MXU_DIM)
        cb, cc, cx = (_dot(hb, w_ref[:, pl.ds(j * D_MODEL + c * MXU_DIM, MXU_DIM)]) for j in range(3))
        u = cc * cx
        up = jnp.where(first, 0.0, pltpu.roll(u, 1, 0))
        dn = jnp.where(last, 0.0, pltpu.roll(u, tm - 1, 0))
        cw = cw_ref[:, sl]
        conv = cw[0:1] * up + cw[1:2] * u + cw[2:3] * dn
        o_ref[:, sl] = (cb * conv).astype(BF16)


def _chunk_norm(z, bd, g):
    sq = (z * z).astype(BF16)
    ss = jnp.concatenate(
        [_dot(sq[:, s * MXU_DIM:(s + 1) * MXU_DIM], bd) for s in range(z.shape[1] // MXU_DIM)], axis=1)
    return z * lax.rsqrt(ss * (1.0 / QK_DIM) + EPS) * g


def _rope(z, cos, sin, hi_half):
    n = z.shape[1]
    swap = jnp.where(hi_half, pltpu.roll(z, AXIS_DIM // 2, 1), pltpu.roll(z, n - AXIS_DIM // 2, 1))
    return z * cos + swap * sin


def _qkv_kernel(*refs, latent):
    if latent:
        x_ref, mod_ref, g1_ref, w_ref, bd_ref, gq_ref, gk_ref, cos_ref, sin_ref, q_ref, k_ref, v_ref = refs
    else:
        x_ref, mod_ref, g1_ref, w_ref, wkt_ref, bd_ref, gq_ref, gkt_ref, q_ref, kt_ref, v_ref, knt_ref, vn_ref = refs
    hb = _norm_mod(x_ref[...], g1_ref[...], mod_ref[0, 1:2, :], mod_ref[0, 0:1, :]).astype(BF16)
    tm = hb.shape[0]
    bd = bd_ref[...]
    q = _chunk_norm(_dot(hb, w_ref[:, 0:D_MODEL]), bd, gq_ref[...])
    if latent:
        k = _chunk_norm(_dot(hb, w_ref[:, D_MODEL:2 * D_MODEL]), bd, gk_ref[...])
    v = _dot(hb, w_ref[:, 2 * D_MODEL:3 * D_MODEL])
    if latent:
        reps = D_MODEL // LANES
        cos = jnp.concatenate([cos_ref[...]] * reps, axis=1)
        sin = jnp.concatenate([sin_ref[...]] * reps, axis=1)
        lane = lax.broadcasted_iota(jnp.int32, (1, D_MODEL), 1)
        hi_half = (lane & (AXIS_DIM // 2)) != 0
        q = _rope(q, cos, sin, hi_half)
        k_att = _rope(k, cos, sin, hi_half)
    else:
        kt = _dot_nt(wkt_ref[...], hb)
        gkt = jnp.concatenate([gkt_ref[...]] * (tm // LANES), axis=1)
        slabs = []
        for s in range(D_MODEL // MXU_DIM):
            z = kt[s * MXU_DIM:(s + 1) * MXU_DIM, :]
            ss = _dot(bd, (z * z).astype(BF16))
            slabs.append(z * lax.rsqrt(ss * (1.0 / QK_DIM) + EPS))
        knt = jnp.concatenate(slabs, axis=0) * gkt
        seq = knt_ref.shape[2]
        for j in range(tm // seq):
            knt_ref[j] = knt[:, j * seq:(j + 1) * seq]
            kt_ref[j] = knt[:, j * seq:(j + 1) * seq].astype(BF16)
        vn_ref[...] = v
    q_ref[...] = (q * (math.log2(math.e) / math.sqrt(QK_DIM))).astype(BF16)
    if latent:
        k_ref[...] = k_att.astype(BF16)
    v_ref[...] = v.astype(BF16)


def _attn_kernel(*refs, latent, lam_init, tq):
    if latent:
        lamv_ref, q_ref, k_ref, v_ref, ck_ref, cv_ref, sg_ref, o_ref = refs
    else:
        lamv_ref, q_ref, k_ref, v_ref, sg_ref, o_ref = refs
    lv = lamv_ref[...]
    lam = (jnp.exp(jnp.sum(lv[0:1] * lv[1:2], axis=1, keepdims=True))
           - jnp.exp(jnp.sum(lv[2:3] * lv[3:4], axis=1, keepdims=True)) + lam_init)
    lane = lax.broadcasted_iota(jnp.int32, (1, V_DIM), 1)
    n_q = q_ref.shape[1]
    units = [(hh, c) for hh in range(q_ref.shape[2] // V_DIM) for c in range(n_q // tq)]

    def kv(hh):
        hs = slice(hh * V_DIM, (hh + 1) * V_DIM)
        vals = [v_ref[0, :, hs]]
        if latent:
            keys = [(k_ref[0, :, hs], False), (ck_ref[0, hs, :].astype(BF16), True)]
            vals.append(cv_ref[0, :, hs].astype(BF16))
        else:
            keys = [(k_ref[0, hs, :], True)]
        return keys, vals

    def scores(u):
        hh, c = units[u]
        q = q_ref[0, c * tq:(c + 1) * tq, hh * V_DIM:(hh + 1) * V_DIM]
        keys, _ = kv(hh)
        halves = (jnp.where(lane < QK_DIM, q, jnp.zeros_like(q)), jnp.where(lane >= QK_DIM, q, jnp.zeros_like(q)))
        return [[_dot(qh, kk) if feature_major else _dot_nt(qh, kk) for kk, feature_major in keys] for qh in halves]

    ones_col = jnp.where(lane == 0, 1.0, 0.0).astype(BF16)

    def softmax_pv(s, vals):
        m = s[0].max(axis=1, keepdims=True)
        for si in s[1:]:
            m = jnp.maximum(m, si.max(axis=1, keepdims=True))
        acc = None
        for si, vv in zip(s, vals):
            v_aug = jnp.concatenate([vv, jnp.broadcast_to(ones_col, vv.shape)], axis=1)
            part = _dot(jnp.exp2(si - m).astype(BF16), v_aug)
            acc = part if acc is None else acc + part
        return acc[:, :V_DIM] * (1.0 / acc[:, V_DIM:V_DIM + 1])

    def finish(u, s):
        hh, c = units[u]
        _, vals = kv(hh)
        o = softmax_pv(s[0], vals) - lam * softmax_pv(s[1], vals)
        o_ref[0, c * tq:(c + 1) * tq, hh * V_DIM:(hh + 1) * V_DIM] = (
            _rms(o) * sg_ref[...] * (1.0 - lam_init)).astype(BF16)

    ahead = ATTN_SCORES_AHEAD
    pending = [scores(u) for u in range(min(ahead, len(units)))]
    for u in range(len(units)):
        if u + ahead < len(units):
            pending.append(scores(u + ahead))
        finish(u, pending.pop(0))


def _merge_kernel(x_ref, mod_ref, g1_ref, yc_ref, oa_ref, wg_ref, wco_ref, wao_ref, wo_ref, g2_ref,
                  wrh_ref, wrl_ref, br_ref, tri_ref, x1_ref, h2_ref, route_ref, meta_ref, cnt_ref):
    x = x_ref[...]
    hb = _norm_mod(x, g1_ref[...], mod_ref[0, 1:2, :], mod_ref[0, 0:1, :]).astype(BF16)
    y_conv = _dot(yc_ref[...], wco_ref[...])
    y_attn = _dot(oa_ref[...], wao_ref[...])
    mixed = (_sigmoid(_dot(hb, wg_ref[:, 0:D_MODEL])) * y_conv
             + _sigmoid(_dot(hb, wg_ref[:, D_MODEL:2 * D_MODEL])) * y_attn)
    x1 = x + mod_ref[0, 2:3, :] * _dot(mixed.astype(BF16), wo_ref[...])
    x1_ref[...] = x1
    h2 = _norm_mod(x1, g2_ref[...], mod_ref[0, 4:5, :], mod_ref[0, 3:4, :])
    h2_ref[...] = h2

    h_hi, h_lo = _split_bf16(h2)
    logits = _dot(h_hi, wrh_ref[...]) + _dot(h_lo, wrh_ref[...]) + _dot(h_hi, wrl_ref[...]) + br_ref[...]
    tm = x.shape[0]
    lane = lax.broadcasted_iota(jnp.int32, (tm, LANES), 1).astype(F32)
    ninf = -jnp.inf
    big = float(LANES)

    def top(vals):
        vmax = vals.max(axis=1, keepdims=True)
        idx = jnp.where(vals == vmax, lane, big).min(axis=1, keepdims=True)
        return vmax, idx

    is_grp = lane < N_GROUPS
    gmax, gidx = top(jnp.where(is_grp, logits, ninf))
    g_p = 1.0 / jnp.where(is_grp, jnp.exp(logits - gmax), 0.0).sum(axis=1, keepdims=True)
    lo = ROUTER_LANE0 + EXPERTS_PER_GROUP * gidx
    el = jnp.where((lane >= lo) & (lane < lo + EXPERTS_PER_GROUP), logits, ninf)
    v1, i1 = top(el)
    v2, i2 = top(jnp.where(lane == i1, ninf, el))
    e2 = jnp.exp(v2 - v1)
    den = 1.0 / (1.0 + e2)
    e0 = i1 - ROUTER_LANE0
    e1 = i2 - ROUTER_LANE0
    onehot = jnp.where((lane == e0) | (lane == e1), 1.0, 0.0)
    before = _dot(tri_ref[...], onehot.astype(BF16))
    r0 = jnp.where(lane == e0, before, 0.0).sum(axis=1, keepdims=True)
    r1 = jnp.where(lane == e1, before, 0.0).sum(axis=1, keepdims=True)
    route = (jnp.where(lane == 0.0, e0, 0.0) + jnp.where(lane == 1.0, e1, 0.0)
             + jnp.where(lane == 2.0, den * g_p, 0.0) + jnp.where(lane == 3.0, e2 * den * g_p, 0.0)
             + jnp.where(lane == 4.0, r0, 0.0) + jnp.where(lane == 5.0, r1, 0.0))
    route_ref[...] = route
    meta_ref[...] = jnp.transpose(route)[:META_ROWS, :]
    cnt_ref[0] = jnp.broadcast_to(onehot.sum(axis=0, keepdims=True), (META_ROWS, LANES))


def _row_copy(src, dst, sem):
    return pltpu.make_async_copy(src, dst, sem)


def _for_groups(n_groups, body):
    if n_groups == 1:
        body(0, 0)
    else:
        lax.fori_loop(0, n_groups, body, 0)


def _dispatch_kernel(pos_ref, last_ref, nu_ref, h2p_ref, h2s_ref, xs_ref, zero_buf, sem, *, n_p, n_tok, n_tiles):
    i = pl.program_id(0)
    tm = h2p_ref.shape[0]

    @pl.when(i == 0)
    def _():
        zero_buf[...] = jnp.zeros_like(zero_buf)

        def zero_tile(row):
            return _row_copy(zero_buf, xs_ref.at[pl.ds(pl.multiple_of(row, MOE_ROWS), MOE_ROWS)], sem)

        for e in range(N_EXPERTS):
            @pl.when(last_ref[e] >= 0)
            def _():
                zero_tile(last_ref[e]).start()

        def start_tail(j, carry):
            zero_tile(j * MOE_ROWS).start()
            return carry

        lax.fori_loop(nu_ref[0], n_tiles, start_tail, 0)
        for e in range(N_EXPERTS):
            @pl.when(last_ref[e] >= 0)
            def _():
                zero_tile(0).wait()

        def wait_tail(j, carry):
            zero_tile(0).wait()
            return carry

        lax.fori_loop(nu_ref[0], n_tiles, wait_tail, 0)

    base = i * tm

    def scatter(h2_ref):
        def issue(g, carry):
            for u in range(DMA_UNROLL):
                r = g * DMA_UNROLL + u
                for slot in range(2):
                    p = pos_ref[slot * n_tok + base + r]
                    _row_copy(h2_ref.at[pl.ds(r, 1)], xs_ref.at[pl.ds(p, 1)], sem).start(priority=slot)
            return carry

        _for_groups(tm // DMA_UNROLL, issue)
        for slot in range(2):
            _row_copy(h2_ref, xs_ref.at[pl.ds(0, tm)], sem).wait()

    @pl.when(i < n_p)
    def _():
        scatter(h2p_ref)

    @pl.when(i >= n_p)
    def _():
        scatter(h2s_ref)


def _expert_kernel(te_ref, nu_ref, x_ref, w1_ref, w3_ref, w2_ref, y_ref, w1b, w3b, w2b):
    j = pl.program_id(0)

    @pl.when((j == 0) | (te_ref[j] != te_ref[jnp.maximum(j - 1, 0)]))
    def _():
        w1b[...] = w1_ref[0].astype(BF16)
        w3b[...] = w3_ref[0].astype(BF16)
        w2b[...] = w2_ref[0].astype(BF16)

    @pl.when(j < nu_ref[0])
    def _():
        t = x_ref[...].astype(BF16)
        a = _dot(t, w1b[...])
        he = (a * _sigmoid(a)) * _dot(t, w3b[...])
        y_ref[...] = _dot(he.astype(BF16), w2b[...])

    @pl.when(j >= nu_ref[0])
    def _():
        y_ref[...] = jnp.zeros_like(y_ref)


def _combine_kernel(pos_ref, x1p_ref, x1s_ref, rp_ref, rs_ref, mod_ref, ys_ref, op_ref, os_ref, ybuf, sem,
                    *, n_p, n_tok, n_steps):
    i = pl.program_id(0)
    tm = ybuf.shape[2]

    def gather(tile, buf):
        base = tile * tm
        for r in range(tm):
            for slot in range(2):
                p = pos_ref[slot * n_tok + base + r]
                _row_copy(ys_ref.at[pl.ds(p, 1)], ybuf.at[buf, slot, pl.ds(r, 1)], sem.at[buf]).start(priority=slot)

    @pl.when(i == 0)
    def _():
        gather(0, 0)

    cur = i % 2

    @pl.when(i + 1 < n_steps)
    def _():
        gather(i + 1, 1 - cur)

    for slot in range(2):
        _row_copy(ybuf.at[cur, slot], ybuf.at[cur, slot], sem.at[cur]).wait()

    def out(x1_ref, r_ref, o_ref):
        r = r_ref[...]
        o_ref[...] = x1_ref[...] + mod_ref[0, 5:6, :] * (r[:, 2:3] * ybuf[cur, 0] + r[:, 3:4] * ybuf[cur, 1])

    @pl.when(i < n_p)
    def _():
        out(x1p_ref, rp_ref, op_ref)

    @pl.when(i >= n_p)
    def _():
        out(x1s_ref, rs_ref, os_ref)


def _layer(x3, mod, row0, latent, cache, wts, lam_init):
    b, l, d = x3.shape
    t = b * l
    x = x3.reshape(t, d)
    tm = MOE_TOKEN_TILE
    nt = t // tm
    per_seq = l // tm if l >= tm else 1
    if latent:
        mod_row = lambda i: row0 + i // per_seq
    else:
        mod_row = lambda i: row0

    x_spec = pl.BlockSpec((tm, d), lambda i: (i, 0))
    mod_spec = pl.BlockSpec((1, 6, d), lambda i: (mod_row(i), 0, 0))
    row_spec = pl.BlockSpec((1, d), lambda i: (0, 0))
    act_spec = pl.BlockSpec((tm, d), lambda i: (i, 0))

    def full(a):
        nd = a.ndim
        return pl.BlockSpec(a.shape, lambda i: (0,) * nd)

    tmc = max(CONV_TOKEN_TILE, l)
    assert tmc % l == 0 and t % tmc == 0
    seq_per_tile = tmc // l
    xc_spec = pl.BlockSpec((tmc, d), lambda i: (i, 0))
    modc_spec = pl.BlockSpec((1, 6, d), lambda i: (row0 + (i * seq_per_tile if latent else 0), 0, 0))
    assert not latent or seq_per_tile == 1
    yc = pl.pallas_call(
        functools.partial(_conv_kernel, seq_len=l),
        grid=(t // tmc,),
        in_specs=[xc_spec, modc_spec, row_spec, full(wts["w_conv_in"]), full(wts["conv_w"])],
        out_specs=xc_spec,
        out_shape=jax.ShapeDtypeStruct((t, d), BF16),
        compiler_params=_params("parallel"),
        name="conv_in",
    )(x, mod, wts["norm1_g"], wts["w_conv_in"], wts["conv_w"])

    if latent:
        qkv_in = [x, mod, wts["norm1_g"], wts["w_qkv"], wts["bd"], wts["gq"], wts["gk"], wts["cos"], wts["sin"]]
        tab_spec = pl.BlockSpec((tm, LANES), lambda i: (i % per_seq, 0))
        qkv_specs = [x_spec, mod_spec, row_spec, full(wts["w_qkv"]), full(wts["bd"]), row_spec, row_spec,
                     tab_spec, tab_spec]
        out_shapes = [jax.ShapeDtypeStruct((t, d), BF16)] * 3
        out_specs = [act_spec] * 3
    else:
        assert tm % l == 0
        kt_spec = pl.BlockSpec((tm // l, d, l), lambda i: (i, 0, 0))
        qkv_in = [x, mod, wts["norm1_g"], wts["w_qkv"], wts["w_kt"], wts["bd"], wts["gq"], wts["gkt"]]
        qkv_specs = [x_spec, mod_spec, row_spec, full(wts["w_qkv"]), full(wts["w_kt"]), full(wts["bd"]), row_spec,
                     full(wts["gkt"])]
        out_shapes = [jax.ShapeDtypeStruct((t, d), BF16), jax.ShapeDtypeStruct((b, d, l), BF16),
                      jax.ShapeDtypeStruct((t, d), BF16), jax.ShapeDtypeStruct((b, d, l), F32),
                      jax.ShapeDtypeStruct((t, d), F32)]
        out_specs = [act_spec, kt_spec, act_spec, kt_spec, act_spec]
    qkv_out = pl.pallas_call(
        functools.partial(_qkv_kernel, latent=latent),
        grid=(nt,),
        in_specs=qkv_specs,
        out_specs=out_specs,
        out_shape=out_shapes,
        compiler_params=_params("parallel"),
        name="qkv",
    )(*qkv_in)
    q, k, v = qkv_out[:3]

    tq = min(ATTN_Q_ROWS, l)
    heads_per_step = max(1, ATTN_UNITS_PER_STEP // (l // tq))
    hw = heads_per_step * V_DIM
    q3, v3 = q.reshape(b, l, d), v.reshape(b, l, d)
    qkv_spec = pl.BlockSpec((1, l, hw), lambda bi, h: (bi, 0, h))
    kt_blk = lambda n_tok: pl.BlockSpec((1, hw, n_tok), lambda bi, h: (bi, h, 0))
    sg_spec = pl.BlockSpec((1, V_DIM), lambda bi, h: (0, 0))
    lam_spec = pl.BlockSpec(wts["lamv"].shape, lambda bi, h: (0, 0))
    if latent:
        ckt, cv = cache
        attn_in = [wts["lamv"], q3, k.reshape(b, l, d), v3, ckt, cv]
        attn_specs = [lam_spec, qkv_spec, qkv_spec, qkv_spec, kt_blk(ckt.shape[2]),
                      pl.BlockSpec((1, cv.shape[1], hw), lambda bi, h: (bi, 0, h))]
    else:
        attn_in = [wts["lamv"], q3, k, v3]
        attn_specs = [lam_spec, qkv_spec, kt_blk(l), qkv_spec]
    attn_in.append(wts["subln_g"])
    attn_specs.append(sg_spec)
    oa = pl.pallas_call(
        functools.partial(_attn_kernel, latent=latent, lam_init=lam_init, tq=tq),
        grid=(b, N_HEADS // heads_per_step),
        in_specs=attn_specs,
        out_specs=qkv_spec,
        out_shape=jax.ShapeDtypeStruct((b, l, d), BF16),
        compiler_params=_params("parallel", "parallel"),
        name="attn",
    )(*attn_in).reshape(t, d)

    lane_spec = pl.BlockSpec((tm, LANES), lambda i: (i, 0))
    x1, h2, route, meta, counts = pl.pallas_call(
        _merge_kernel,
        grid=(nt,),
        in_specs=[x_spec, mod_spec, row_spec, act_spec, act_spec, full(wts["w_gate"]), full(wts["w_conv_out"]),
                  full(wts["w_attn_out"]), full(wts["w_o"]), row_spec, full(wts["wr_hi"]), full(wts["wr_lo"]),
                  full(wts["b_router"]), full(wts["tri"])],
        out_specs=[x_spec, x_spec, lane_spec, pl.BlockSpec((META_ROWS, tm), lambda i: (0, i)),
                   pl.BlockSpec((1, META_ROWS, LANES), lambda i: (i, 0, 0))],
        out_shape=[jax.ShapeDtypeStruct((t, d), F32), jax.ShapeDtypeStruct((t, d), F32),
                   jax.ShapeDtypeStruct((t, LANES), F32), jax.ShapeDtypeStruct((META_ROWS, t), F32),
                   jax.ShapeDtypeStruct((nt, META_ROWS, LANES), F32)],
        compiler_params=_params("parallel"),
        name="merge",
    )(x, mod, wts["norm1_g"], yc, oa, wts["w_gate"], wts["w_conv_out"], wts["w_attn_out"], wts["w_o"],
      wts["norm2_g"], wts["wr_hi"], wts["wr_lo"], wts["b_router"], wts["tri"])

    outs = [x1, h2, route, meta, counts[:, 0, :N_EXPERTS]]
    if not latent:
        outs += [qkv_out[3], qkv_out[4]]
    return outs


def _routing_tables(meta, counts, tm, n_tiles):
    eid = meta[0:2].astype(jnp.int32)
    rank = meta[4:6].astype(jnp.int32)
    counts = counts.astype(jnp.int32)
    totals = counts.sum(axis=0)
    tiles_e = (totals + MOE_ROWS - 1) // MOE_ROWS
    tile_end = jnp.cumsum(tiles_e)
    row_start = (tile_end - tiles_e) * MOE_ROWS
    base = row_start[None, :] + jnp.cumsum(counts, axis=0) - counts
    base_tok = jnp.repeat(base.T, tm, axis=1)
    ex = jnp.arange(N_EXPERTS, dtype=jnp.int32)[None, :, None]
    pos = jnp.sum(jnp.where(eid[:, None, :] == ex, base_tok[None], 0), axis=1) + rank
    n_used = tile_end[-1]
    tile_ids = jnp.minimum(jnp.arange(n_tiles, dtype=jnp.int32), n_used - 1)
    tile_expert = jnp.sum((tile_ids[:, None] >= tile_end[None, :]).astype(jnp.int32), axis=1)
    last_tile_row = jnp.where(tiles_e > 0, (tile_end - 1) * MOE_ROWS, -1)
    return (pos.reshape(-1), tile_expert.astype(jnp.int32), n_used.astype(jnp.int32).reshape(1),
            last_tile_row.astype(jnp.int32))


def _routed_moe(streams, mod, w1, w3, w2, per_seq_tiles, tm):
    (x1p, h2p, rp, _, _), (x1s, h2s, rs, _, _) = streams
    d = x1p.shape[1]
    tp, ts = x1p.shape[0], x1s.shape[0]
    t = tp + ts
    n_tiles = 2 * t // MOE_ROWS + N_EXPERTS
    meta = jnp.concatenate([s[3] for s in streams], axis=1)
    counts = jnp.concatenate([s[4] for s in streams], axis=0)
    pos, tile_expert, n_used, last_tile_row = _routing_tables(meta, counts, tm, n_tiles)

    sorted_shape = jax.ShapeDtypeStruct((n_tiles * MOE_ROWS, d), F32)
    any_spec = pl.BlockSpec(memory_space=pl.ANY)
    n_p = tp // tm
    x_sorted = pl.pallas_call(
        functools.partial(_dispatch_kernel, n_p=n_p, n_tok=t, n_tiles=n_tiles),
        grid_spec=pltpu.PrefetchScalarGridSpec(
            num_scalar_prefetch=3, grid=(t // tm,),
            in_specs=[pl.BlockSpec((tm, d), lambda i, *_: (jnp.minimum(i, n_p - 1), 0)),
                      pl.BlockSpec((tm, d), lambda i, *_: (jnp.maximum(i - n_p, 0), 0))],
            out_specs=any_spec,
            scratch_shapes=[pltpu.VMEM((MOE_ROWS, d), F32), pltpu.SemaphoreType.DMA]),
        out_shape=sorted_shape,
        compiler_params=_params("arbitrary"),
        name="dispatch",
    )(pos, last_tile_row, n_used, h2p, h2s)

    def row_blk(j, te, nu):
        return (jnp.minimum(j, nu[0] - 1), 0)

    y_sorted = pl.pallas_call(
        _expert_kernel,
        grid_spec=pltpu.PrefetchScalarGridSpec(
            num_scalar_prefetch=2,
            grid=(n_tiles,),
            in_specs=[pl.BlockSpec((MOE_ROWS, d), row_blk),
                      pl.BlockSpec((1, d, D_EXPERT), lambda j, te, nu: (te[j], 0, 0)),
                      pl.BlockSpec((1, d, D_EXPERT), lambda j, te, nu: (te[j], 0, 0)),
                      pl.BlockSpec((1, D_EXPERT, d), lambda j, te, nu: (te[j], 0, 0))],
            out_specs=pl.BlockSpec((MOE_ROWS, d), lambda j, te, nu: (j, 0)),
            scratch_shapes=[pltpu.VMEM((d, D_EXPERT), BF16), pltpu.VMEM((d, D_EXPERT), BF16),
                            pltpu.VMEM((D_EXPERT, d), BF16)]),
        out_shape=jax.ShapeDtypeStruct((n_tiles * MOE_ROWS, d), F32),
        compiler_params=_params("arbitrary"),
        name="experts",
    )(tile_expert, n_used, x_sorted, w1, w3, w2)

    ctx_blk = lambda i, p: (jnp.minimum(i, n_p - 1), 0)
    lat_blk = lambda i, p: (jnp.maximum(i - n_p, 0), 0)
    return pl.pallas_call(
        functools.partial(_combine_kernel, n_p=n_p, n_tok=t, n_steps=t // tm),
        grid_spec=pltpu.PrefetchScalarGridSpec(
            num_scalar_prefetch=1, grid=(t // tm,),
            in_specs=[pl.BlockSpec((tm, d), ctx_blk), pl.BlockSpec((tm, d), lat_blk),
                      pl.BlockSpec((tm, LANES), ctx_blk), pl.BlockSpec((tm, LANES), lat_blk),
                      pl.BlockSpec((1, 6, d),
                                   lambda i, p: (jnp.where(i < n_p, 0, 1 + (i - n_p) // per_seq_tiles), 0, 0)),
                      any_spec],
            out_specs=[pl.BlockSpec((tm, d), ctx_blk), pl.BlockSpec((tm, d), lat_blk)],
            scratch_shapes=[pltpu.VMEM((2, 2, tm, d), F32), pltpu.SemaphoreType.DMA((2,))]),
        out_shape=[jax.ShapeDtypeStruct((tp, d), F32), jax.ShapeDtypeStruct((ts, d), F32)],
        compiler_params=_params("arbitrary"),
        name="combine",
    )(pos, x1p, x1s, rp, rs, mod, y_sorted)


def _rope_tables(n_tok):
    rows = n_tok // GRID_W
    pos_row = jnp.repeat(jnp.arange(rows, dtype=F32), GRID_W)
    pos_col = jnp.tile(jnp.arange(GRID_W, dtype=F32), rows)
    inv_freq = 1.0 / (ROPE_BASE ** (jnp.arange(0, AXIS_DIM, 2, dtype=F32) / AXIS_DIM))
    ang = jnp.concatenate([pos_row[:, None] * inv_freq[None, :]] * 2
                          + [pos_col[:, None] * inv_freq[None, :]] * 2, axis=1)
    sign = jnp.tile(jnp.concatenate([-jnp.ones((AXIS_DIM // 2,), F32), jnp.ones((AXIS_DIM // 2,), F32)]), 2)
    cos = jnp.cos(ang)
    sin = jnp.sin(ang) * sign[None, :]
    return jnp.tile(cos, (1, LANES // QK_DIM)), jnp.tile(sin, (1, LANES // QK_DIM))


def kernel(x_prompt, x_sample, cache_k, cache_v, c, c_ctx, w_ada, b_ada, norm1_g, w_in, conv_w, w_conv_out,
           q_norm_g, k_norm_g, lambda_q1, lambda_k1, lambda_q2, lambda_k2, subln_g, w_attn_out, w_o, norm2_g,
           w_grp, b_grp, w_exp, b_exp, w1, w3, w2):
    depth = w_in.shape[0]
    assert depth == 1
    d = D_MODEL
    dec_b, dec_l = x_sample.shape[0], x_sample.shape[1]
    cos, sin = _rope_tables(dec_l)
    bd = (jnp.arange(MXU_DIM)[:, None] // QK_DIM == jnp.arange(MXU_DIM)[None, :] // QK_DIM).astype(BF16)
    tri = (jnp.arange(MOE_TOKEN_TILE)[:, None] > jnp.arange(MOE_TOKEN_TILE)[None, :]).astype(BF16)
    xp, xs = x_prompt, x_sample
    new_k, new_v = [], []
    for l in range(depth):
        lam_init = 0.8 - 0.6 * math.exp(-0.3 * l)
        mod_rows = 16
        cvec = jnp.concatenate([c_ctx[None, :], c, jnp.zeros((mod_rows - 1 - dec_b, d), F32)], axis=0)
        mod = _ada(cvec, w_ada[l], b_ada[l][None, :]).reshape(mod_rows, 6, d)

        wi = w_in[l].astype(BF16)
        w_router = jnp.concatenate([w_grp[l], w_exp[l]], axis=1)
        w_router = jnp.pad(w_router, ((0, 0), (0, LANES - w_router.shape[1])))
        wr_hi = w_router.astype(BF16)
        wr_lo = (w_router - wr_hi.astype(F32)).astype(BF16)
        b_router = jnp.pad(jnp.concatenate([b_grp[l], b_exp[l]]), (0, LANES - N_GROUPS - N_EXPERTS))[None, :]
        wts = dict(
            norm1_g=norm1_g[l][None, :], norm2_g=norm2_g[l][None, :],
            w_conv_in=wi[:, :3 * d], w_qkv=wi[:, 3 * d:6 * d], w_gate=wi[:, 6 * d:],
            conv_w=conv_w[l], w_conv_out=w_conv_out[l].astype(BF16),
            gq=jnp.tile(q_norm_g[l], d // QK_DIM)[None, :], gk=jnp.tile(k_norm_g[l], d // QK_DIM)[None, :],
            w_kt=wi[:, 4 * d:5 * d].T, gkt=jnp.broadcast_to(jnp.tile(k_norm_g[l], d // QK_DIM)[:, None], (d, LANES)),
            bd=bd, cos=cos, sin=sin,
            lamv=jnp.stack([lambda_q1[l], lambda_k1[l], lambda_q2[l], lambda_k2[l]]),
            subln_g=subln_g[l][None, :], w_attn_out=w_attn_out[l].astype(BF16), w_o=w_o[l].astype(BF16),
            wr_hi=wr_hi, wr_lo=wr_lo, b_router=b_router, tri=tri,
        )
        *ctx, k_ctx, v_ctx = _layer(xp, mod, 0, False, None, wts, lam_init)
        new_k.append(k_ctx.reshape(x_prompt.shape[0], N_HEADS, 2, QK_DIM, x_prompt.shape[1]).transpose(0, 4, 1, 2, 3))
        new_v.append(v_ctx.reshape(x_prompt.shape[0], x_prompt.shape[1], N_HEADS, V_DIM))
        past = cache_k.shape[2]
        cache = (cache_k[:, l].transpose(0, 2, 3, 4, 1).reshape(dec_b, d, past), cache_v[:, l].reshape(dec_b, past, d))
        lat = _layer(xs, mod, 1, True, cache, wts, lam_init)
        tm = MOE_TOKEN_TILE
        yp, ys = _routed_moe((ctx, lat), mod, w1[l], w3[l], w2[l], dec_l // tm, tm)
        xp, xs = yp.reshape(x_prompt.shape), ys.reshape(x_sample.shape)
    return (xp, xs, jnp.stack(new_k, axis=1), jnp.stack(new_v, axis=1))
```

```python
import functools
import math

import jax
import jax.numpy as jnp
from jax import lax
from jax.experimental import pallas as pl
from jax.experimental.pallas import tpu as pltpu

D_MODEL = 1024
GRID_W = 64
N_HEADS = 8
QK_DIM = 64
V_DIM = 2 * QK_DIM
AXIS_DIM = QK_DIM // 2
ROPE_BASE = 10000.0
N_GROUPS = 4
EXPERTS_PER_GROUP = 4
N_EXPERTS = N_GROUPS * EXPERTS_PER_GROUP
D_EXPERT = 512
EPS = 1e-6

LANES = 128
MXU_DIM = 256
VMEM_LIMIT_BYTES = 56 * 1024 * 1024
ROUTER_LANE0 = N_GROUPS
CONV_TOKEN_TILE = 1024
MOE_TOKEN_TILE = 512
MOE_ROWS = 512
META_ROWS = 8
DMA_UNROLL = 512
ATTN_Q_ROWS = 256
ATTN_UNITS_PER_STEP = 8
ATTN_SCORES_AHEAD = 1

F32 = jnp.float32
BF16 = jnp.bfloat16


def _dot(a, b):
    return jnp.dot(a, b, preferred_element_type=F32)


def _dot_nt(a, b):
    return lax.dot_general(a, b, (((1,), (1,)), ((), ())), preferred_element_type=F32)


def _split_bf16(x):
    hi = x.astype(BF16)
    lo = (x - hi.astype(F32)).astype(BF16)
    return hi, lo


def _sigmoid(x):
    return 1.0 / (1.0 + jnp.exp(-x))


def _rms(x):
    return x * lax.rsqrt(jnp.mean(x * x, axis=-1, keepdims=True) + EPS)


def _params(*sem):
    return pltpu.CompilerParams(dimension_semantics=sem, vmem_limit_bytes=VMEM_LIMIT_BYTES)


def _ada_kernel(c_ref, w_ref, b_ref, o_ref):
    c = c_ref[...]
    s = c * _sigmoid(c)
    o_ref[...] = _dot(s.astype(BF16), w_ref[...].astype(BF16)) + b_ref[...]


def _ada(cvec, w_ada, b_ada):
    rows, d = cvec.shape
    n = w_ada.shape[1]
    tn = 1024
    return pl.pallas_call(
        _ada_kernel,
        grid=(n // tn,),
        in_specs=[pl.BlockSpec((rows, d), lambda j: (0, 0)),
                  pl.BlockSpec((d, tn), lambda j: (0, j)),
                  pl.BlockSpec((1, tn), lambda j: (0, j))],
        out_specs=pl.BlockSpec((rows, tn), lambda j: (0, j)),
        out_shape=jax.ShapeDtypeStruct((rows, n), F32),
        compiler_params=_params("arbitrary"),
        name="ada",
    )(cvec, w_ada, b_ada)


def _norm_mod(x, g, scale, shift):
    return (_rms(x) * g) * (1.0 + scale) + shift


def _conv_kernel(x_ref, mod_ref, g1_ref, w_ref, cw_ref, o_ref, *, seq_len):
    x = x_ref[...]
    tm = x.shape[0]
    hb = _norm_mod(x, g1_ref[...], mod_ref[0, 1:2, :], mod_ref[0, 0:1, :]).astype(BF16)
    pos = lax.broadcasted_iota(jnp.int32, (tm, 1), 0) % seq_len
    first = pos == 0
    last = pos == seq_len - 1
    for c in range(D_MODEL // MXU_DIM):
        sl = slice(c * MXU_DIM, (c + 1) * MXU_DIM)
        cb, cc, cx = (_dot(hb, w_ref[:, pl.ds(j * D_MODEL + c * MXU_DIM, MXU_DIM)]) for j in range(3))
        u = cc * cx
        up = jnp.where(first, 0.0, pltpu.roll(u, 1, 0))
        dn = jnp.where(last, 0.0, pltpu.roll(u, tm - 1, 0))
        cw = cw_ref[:, sl]
        conv = cw[0:1] * up + cw[1:2] * u + cw[2:3] * dn
        o_ref[:, sl] = (cb * conv).astype(BF16)


def _chunk_norm(z, bd, g):
    sq = (z * z).astype(BF16)
    ss = jnp.concatenate(
        [_dot(sq[:, s * MXU_DIM:(s + 1) * MXU_DIM], bd) for s in range(z.shape[1] // MXU_DIM)], axis=1)
    return z * lax.rsqrt(ss * (1.0 / QK_DIM) + EPS) * g


def _rope(z, cos, sin, hi_half):
    n = z.shape[1]
    swap = jnp.where(hi_half, pltpu.roll(z, AXIS_DIM // 2, 1), pltpu.roll(z, n - AXIS_DIM // 2, 1))
    return z * cos + swap * sin


def _qkv_kernel(*refs, latent):
    if latent:
        x_ref, mod_ref, g1_ref, w_ref, bd_ref, gq_ref, gk_ref, cos_ref, sin_ref, q_ref, k_ref, v_ref = refs
    else:
        x_ref, mod_ref, g1_ref, w_ref, wkt_ref, bd_ref, gq_ref, gkt_ref, q_ref, kt_ref, v_ref, knt_ref, vn_ref = refs
    hb = _norm_mod(x_ref[...], g1_ref[...], mod_ref[0, 1:2, :], mod_ref[0, 0:1, :]).astype(BF16)
    tm = hb.shape[0]
    bd = bd_ref[...]
    q = _chunk_norm(_dot(hb, w_ref[:, 0:D_MODEL]), bd, gq_ref[...])
    if latent:
        k = _chunk_norm(_dot(hb, w_ref[:, D_MODEL:2 * D_MODEL]), bd, gk_ref[...])
    v = _dot(hb, w_ref[:, 2 * D_MODEL:3 * D_MODEL])
    if latent:
        reps = D_MODEL // LANES
        cos = jnp.concatenate([cos_ref[...]] * reps, axis=1)
        sin = jnp.concatenate([sin_ref[...]] * reps, axis=1)
        lane = lax.broadcasted_iota(jnp.int32, (1, D_MODEL), 1)
        hi_half = (lane & (AXIS_DIM // 2)) != 0
        q = _rope(q, cos, sin, hi_half)
        k_att = _rope(k, cos, sin, hi_half)
    else:
        kt = _dot_nt(wkt_ref[...], hb)
        gkt = jnp.concatenate([gkt_ref[...]] * (tm // LANES), axis=1)
        slabs = []
        for s in range(D_MODEL // MXU_DIM):
            z = kt[s * MXU_DIM:(s + 1) * MXU_DIM, :]
            ss = _dot(bd, (z * z).astype(BF16))
            slabs.append(z * lax.rsqrt(ss * (1.0 / QK_DIM) + EPS))
        knt = jnp.concatenate(slabs, axis=0) * gkt
        seq = knt_ref.shape[2]
        for j in range(tm // seq):
            knt_ref[j] = knt[:, j * seq:(j + 1) * seq]
            kt_ref[j] = knt[:, j * seq:(j + 1) * seq].astype(BF16)
        vn_ref[...] = v
    q_ref[...] = (q * (math.log2(math.e) / math.sqrt(QK_DIM))).astype(BF16)
    if latent:
        k_ref[...] = k_att.astype(BF16)
    v_ref[...] = v.astype(BF16)


def _attn_kernel(*refs, latent, lam_init, tq):
    if latent:
        lamv_ref, q_ref, k_ref, v_ref, ck_ref, cv_ref, sg_ref, o_ref = refs
    else:
        lamv_ref, q_ref, k_ref, v_ref, sg_ref, o_ref = refs
    lv = lamv_ref[...]
    lam = (jnp.exp(jnp.sum(lv[0:1] * lv[1:2], axis=1, keepdims=True))
           - jnp.exp(jnp.sum(lv[2:3] * lv[3:4], axis=1, keepdims=True)) + lam_init)
    lane = lax.broadcasted_iota(jnp.int32, (1, V_DIM), 1)
    n_q = q_ref.shape[1]
    units = [(hh, c) for hh in range(q_ref.shape[2] // V_DIM) for c in range(n_q // tq)]

    def kv(hh):
        hs = slice(hh * V_DIM, (hh + 1) * V_DIM)
        vals = [v_ref[0, :, hs]]
        if latent:
            keys = [(k_ref[0, :, hs], False), (ck_ref[0, hs, :].astype(BF16), True)]
            vals.append(cv_ref[0, :, hs].astype(BF16))
        else:
            keys = [(k_ref[0, hs, :], True)]
        return keys, vals

    def scores(u):
        hh, c = units[u]
        q = q_ref[0, c * tq:(c + 1) * tq, hh * V_DIM:(hh + 1) * V_DIM]
        keys, _ = kv(hh)
        halves = (jnp.where(lane < QK_DIM, q, jnp.zeros_like(q)), jnp.where(lane >= QK_DIM, q, jnp.zeros_like(q)))
        return [[_dot(qh, kk) if feature_major else _dot_nt(qh, kk) for kk, feature_major in keys] for qh in halves]

    ones_col = jnp.where(lane == 0, 1.0, 0.0).astype(BF16)

    def softmax_pv(s, vals):
        m = s[0].max(axis=1, keepdims=True)
        for si in s[1:]:
            m = jnp.maximum(m, si.max(axis=1, keepdims=True))
        acc = None
        for si, vv in zip(s, vals):
            v_aug = jnp.concatenate([vv, jnp.broadcast_to(ones_col, vv.shape)], axis=1)
            part = _dot(jnp.exp2(si - m).astype(BF16), v_aug)
            acc = part if acc is None else acc + part
        return acc[:, :V_DIM] * (1.0 / acc[:, V_DIM:V_DIM + 1])

    def finish(u, s):
        hh, c = units[u]
        _, vals = kv(hh)
        o = softmax_pv(s[0], vals) - lam * softmax_pv(s[1], vals)
        o_ref[0, c * tq:(c + 1) * tq, hh * V_DIM:(hh + 1) * V_DIM] = (
            _rms(o) * sg_ref[...] * (1.0 - lam_init)).astype(BF16)

    ahead = ATTN_SCORES_AHEAD
    pending = [scores(u) for u in range(min(ahead, len(units)))]
    for u in range(len(units)):
        if u + ahead < len(units):
            pending.append(scores(u + ahead))
        finish(u, pending.pop(0))


def _merge_kernel(x_ref, mod_ref, g1_ref, yc_ref, oa_ref, wg_ref, wco_ref, wao_ref, wo_ref, g2_ref,
                  wrh_ref, wrl_ref, br_ref, tri_ref, x1_ref, h2_ref, route_ref, meta_ref, cnt_ref):
    x = x_ref[...]
    hb = _norm_mod(x, g1_ref[...], mod_ref[0, 1:2, :], mod_ref[0, 0:1, :]).astype(BF16)
    y_conv = _dot(yc_ref[...], wco_ref[...])
    y_attn = _dot(oa_ref[...], wao_ref[...])
    mixed = (_sigmoid(_dot(hb, wg_ref[:, 0:D_MODEL])) * y_conv
             + _sigmoid(_dot(hb, wg_ref[:, D_MODEL:2 * D_MODEL])) * y_attn)
    x1 = x + mod_ref[0, 2:3, :] * _dot(mixed.astype(BF16), wo_ref[...])
    x1_ref[...] = x1
    h2 = _norm_mod(x1, g2_ref[...], mod_ref[0, 4:5, :], mod_ref[0, 3:4, :])
    h2_ref[...] = h2

    h_hi, h_lo = _split_bf16(h2)
    logits = _dot(h_hi, wrh_ref[...]) + _dot(h_lo, wrh_ref[...]) + _dot(h_hi, wrl_ref[...]) + br_ref[...]
    tm = x.shape[0]
    lane = lax.broadcasted_iota(jnp.int32, (tm, LANES), 1).astype(F32)
    ninf = -jnp.inf
    big = float(LANES)

    def top(vals):
        vmax = vals.max(axis=1, keepdims=True)
        idx = jnp.where(vals == vmax, lane, big).min(axis=1, keepdims=True)
        return vmax, idx

    is_grp = lane < N_GROUPS
    gmax, gidx = top(jnp.where(is_grp, logits, ninf))
    g_p = 1.0 / jnp.where(is_grp, jnp.exp(logits - gmax), 0.0).sum(axis=1, keepdims=True)
    lo = ROUTER_LANE0 + EXPERTS_PER_GROUP * gidx
    el = jnp.where((lane >= lo) & (lane < lo + EXPERTS_PER_GROUP), logits, ninf)
    v1, i1 = top(el)
    v2, i2 = top(jnp.where(lane == i1, ninf, el))
    e2 = jnp.exp(v2 - v1)
    den = 1.0 / (1.0 + e2)
    e0 = i1 - ROUTER_LANE0
    e1 = i2 - ROUTER_LANE0
    onehot = jnp.where((lane == e0) | (lane == e1), 1.0, 0.0)
    before = _dot(tri_ref[...], onehot.astype(BF16))
    r0 = jnp.where(lane == e0, before, 0.0).sum(axis=1, keepdims=True)
    r1 = jnp.where(lane == e1, before, 0.0).sum(axis=1, keepdims=True)
    route = (jnp.where(lane == 0.0, e0, 0.0) + jnp.where(lane == 1.0, e1, 0.0)
             + jnp.where(lane == 2.0, den * g_p, 0.0) + jnp.where(lane == 3.0, e2 * den * g_p, 0.0)
             + jnp.where(lane == 4.0, r0, 0.0) + jnp.where(lane == 5.0, r1, 0.0))
    route_ref[...] = route
    meta_ref[...] = jnp.transpose(route)[:META_ROWS, :]
    cnt_ref[0] = jnp.broadcast_to(onehot.sum(axis=0, keepdims=True), (META_ROWS, LANES))


def _row_copy(src, dst, sem):
    return pltpu.make_async_copy(src, dst, sem)


def _for_groups(n_groups, body):
    if n_groups == 1:
        body(0, 0)
    else:
        lax.fori_loop(0, n_groups, body, 0)


def _dispatch_kernel(pos_ref, last_ref, nu_ref, h2p_ref, h2s_ref, xs_ref, zero_buf, sem, *, n_p, n_tok, n_tiles):
    i = pl.program_id(0)
    tm = h2p_ref.shape[0]

    @pl.when(i == 0)
    def _():
        zero_buf[...] = jnp.zeros_like(zero_buf)

        def zero_tile(row):
            return _row_copy(zero_buf, xs_ref.at[pl.ds(pl.multiple_of(row, MOE_ROWS), MOE_ROWS)], sem)

        for e in range(N_EXPERTS):
            @pl.when(last_ref[e] >= 0)
            def _():
                zero_tile(last_ref[e]).start()

        def start_tail(j, carry):
            zero_tile(j * MOE_ROWS).start()
            return carry

        lax.fori_loop(nu_ref[0], n_tiles, start_tail, 0)
        for e in range(N_EXPERTS):
            @pl.when(last_ref[e] >= 0)
            def _():
                zero_tile(0).wait()

        def wait_tail(j, carry):
            zero_tile(0).wait()
            return carry

        lax.fori_loop(nu_ref[0], n_tiles, wait_tail, 0)

    base = i * tm

    def scatter(h2_ref):
        def issue(g, carry):
            for u in range(DMA_UNROLL):
                r = g * DMA_UNROLL + u
                for slot in range(2):
                    p = pos_ref[slot * n_tok + base + r]
                    _row_copy(h2_ref.at[pl.ds(r, 1)], xs_ref.at[pl.ds(p, 1)], sem).start(priority=slot)
            return carry

        _for_groups(tm // DMA_UNROLL, issue)
        for slot in range(2):
            _row_copy(h2_ref, xs_ref.at[pl.ds(0, tm)], sem).wait()

    @pl.when(i < n_p)
    def _():
        scatter(h2p_ref)

    @pl.when(i >= n_p)
    def _():
        scatter(h2s_ref)


def _expert_kernel(te_ref, nu_ref, x_ref, w1_ref, w3_ref, w2_ref, y_ref, w1b, w3b, w2b):
    j = pl.program_id(0)

    @pl.when((j == 0) | (te_ref[j] != te_ref[jnp.maximum(j - 1, 0)]))
    def _():
        w1b[...] = w1_ref[0].astype(BF16)
        w3b[...] = w3_ref[0].astype(BF16)
        w2b[...] = w2_ref[0].astype(BF16)

    @pl.when(j < nu_ref[0])
    def _():
        t = x_ref[...].astype(BF16)
        a = _dot(t, w1b[...])
        he = (a * _sigmoid(a)) * _dot(t, w3b[...])
        y_ref[...] = _dot(he.astype(BF16), w2b[...])

    @pl.when(j >= nu_ref[0])
    def _():
        y_ref[...] = jnp.zeros_like(y_ref)


def _combine_kernel(pos_ref, x1p_ref, x1s_ref, rp_ref, rs_ref, mod_ref, ys_ref, op_ref, os_ref, ybuf, sem,
                    *, n_p, n_tok, n_steps):
    i = pl.program_id(0)
    tm = ybuf.shape[2]

    def gather(tile, buf):
        base = tile * tm
        for r in range(tm):
            for slot in range(2):
                p = pos_ref[slot * n_tok + base + r]
                _row_copy(ys_ref.at[pl.ds(p, 1)], ybuf.at[buf, slot, pl.ds(r, 1)], sem.at[buf]).start(priority=slot)

    @pl.when(i == 0)
    def _():
        gather(0, 0)

    cur = i % 2

    @pl.when(i + 1 < n_steps)
    def _():
        gather(i + 1, 1 - cur)

    for slot in range(2):
        _row_copy(ybuf.at[cur, slot], ybuf.at[cur, slot], sem.at[cur]).wait()

    def out(x1_ref, r_ref, o_ref):
        r = r_ref[...]
        o_ref[...] = x1_ref[...] + mod_ref[0, 5:6, :] * (r[:, 2:3] * ybuf[cur, 0] + r[:, 3:4] * ybuf[cur, 1])

    @pl.when(i < n_p)
    def _():
        out(x1p_ref, rp_ref, op_ref)

    @pl.when(i >= n_p)
    def _():
        out(x1s_ref, rs_ref, os_ref)


def _layer(x3, mod, row0, latent, cache, wts, lam_init):
    b, l, d = x3.shape
    t = b * l
    x = x3.reshape(t, d)
    tm = MOE_TOKEN_TILE
    nt = t // tm
    per_seq = l // tm if l >= tm else 1
    if latent:
        mod_row = lambda i: row0 + i // per_seq
    else:
        mod_row = lambda i: row0

    x_spec = pl.BlockSpec((tm, d), lambda i: (i, 0))
    mod_spec = pl.BlockSpec((1, 6, d), lambda i: (mod_row(i), 0, 0))
    row_spec = pl.BlockSpec((1, d), lambda i: (0, 0))
    act_spec = pl.BlockSpec((tm, d), lambda i: (i, 0))

    def full(a):
        nd = a.ndim
        return pl.BlockSpec(a.shape, lambda i: (0,) * nd)

    def w_in_cols(width, idx):
        return pl.BlockSpec((d, width), lambda i: (0, idx))

    tmc = max(CONV_TOKEN_TILE, l)
    assert tmc % l == 0 and t % tmc == 0
    seq_per_tile = tmc // l
    xc_spec = pl.BlockSpec((tmc, d), lambda i: (i, 0))
    modc_spec = pl.BlockSpec((1, 6, d), lambda i: (row0 + (i * seq_per_tile if latent else 0), 0, 0))
    assert not latent or seq_per_tile == 1
    yc = pl.pallas_call(
        functools.partial(_conv_kernel, seq_len=l),
        grid=(t // tmc,),
        in_specs=[xc_spec, modc_spec, row_spec, w_in_cols(3 * d, 0), full(wts["conv_w"])],
        out_specs=xc_spec,
        out_shape=jax.ShapeDtypeStruct((t, d), BF16),
        compiler_params=_params("parallel"),
        name="conv_in",
    )(x, mod, wts["norm1_g"], wts["w_in"], wts["conv_w"])

    if latent:
        qkv_in = [x, mod, wts["norm1_g"], wts["w_in"], wts["bd"], wts["gq"], wts["gk"], wts["cos"], wts["sin"]]
        tab_spec = pl.BlockSpec((tm, LANES), lambda i: (i % per_seq, 0))
        qkv_specs = [x_spec, mod_spec, row_spec, w_in_cols(3 * d, 1), full(wts["bd"]), row_spec, row_spec,
                     tab_spec, tab_spec]
        out_shapes = [jax.ShapeDtypeStruct((t, d), BF16)] * 3
        out_specs = [act_spec] * 3
    else:
        assert tm % l == 0
        kt_spec = pl.BlockSpec((tm // l, d, l), lambda i: (i, 0, 0))
        qkv_in = [x, mod, wts["norm1_g"], wts["w_in"], wts["w_kt"], wts["bd"], wts["gq"], wts["gkt"]]
        qkv_specs = [x_spec, mod_spec, row_spec, w_in_cols(3 * d, 1), full(wts["w_kt"]), full(wts["bd"]), row_spec,
                     full(wts["gkt"])]
        out_shapes = [jax.ShapeDtypeStruct((t, d), BF16), jax.ShapeDtypeStruct((b, d, l), BF16),
                      jax.ShapeDtypeStruct((t, d), BF16), jax.ShapeDtypeStruct((b, d, l), F32),
                      jax.ShapeDtypeStruct((t, d), F32)]
        out_specs = [act_spec, kt_spec, act_spec, kt_spec, act_spec]
    qkv_out = pl.pallas_call(
        functools.partial(_qkv_kernel, latent=latent),
        grid=(nt,),
        in_specs=qkv_specs,
        out_specs=out_specs,
        out_shape=out_shapes,
        compiler_params=_params("parallel"),
        name="qkv",
    )(*qkv_in)
    q, k, v = qkv_out[:3]

    tq = min(ATTN_Q_ROWS, l)
    heads_per_step = max(1, ATTN_UNITS_PER_STEP // (l // tq))
    hw = heads_per_step * V_DIM
    q3, v3 = q.reshape(b, l, d), v.reshape(b, l, d)
    qkv_spec = pl.BlockSpec((1, l, hw), lambda bi, h: (bi, 0, h))
    kt_blk = lambda n_tok: pl.BlockSpec((1, hw, n_tok), lambda bi, h: (bi, h, 0))
    sg_spec = pl.BlockSpec((1, V_DIM), lambda bi, h: (0, 0))
    lam_spec = pl.BlockSpec(wts["lamv"].shape, lambda bi, h: (0, 0))
    if latent:
        ckt, cv = cache
        attn_in = [wts["lamv"], q3, k.reshape(b, l, d), v3, ckt, cv]
        attn_specs = [lam_spec, qkv_spec, qkv_spec, qkv_spec, kt_blk(ckt.shape[2]),
                      pl.BlockSpec((1, cv.shape[1], hw), lambda bi, h: (bi, 0, h))]
    else:
        attn_in = [wts["lamv"], q3, k, v3]
        attn_specs = [lam_spec, qkv_spec, kt_blk(l), qkv_spec]
    attn_in.append(wts["subln_g"])
    attn_specs.append(sg_spec)
    oa = pl.pallas_call(
        functools.partial(_attn_kernel, latent=latent, lam_init=lam_init, tq=tq),
        grid=(b, N_HEADS // heads_per_step),
        in_specs=attn_specs,
        out_specs=qkv_spec,
        out_shape=jax.ShapeDtypeStruct((b, l, d), BF16),
        compiler_params=_params("parallel", "parallel"),
        name="attn",
    )(*attn_in).reshape(t, d)

    lane_spec = pl.BlockSpec((tm, LANES), lambda i: (i, 0))
    x1, h2, route, meta, counts = pl.pallas_call(
        _merge_kernel,
        grid=(nt,),
        in_specs=[x_spec, mod_spec, row_spec, act_spec, act_spec, w_in_cols(2 * d, 3), full(wts["w_conv_out"]),
                  full(wts["w_attn_out"]), full(wts["w_o"]), row_spec, full(wts["wr_hi"]), full(wts["wr_lo"]),
                  full(wts["b_router"]), full(wts["tri"])],
        out_specs=[x_spec, x_spec, lane_spec, pl.BlockSpec((META_ROWS, tm), lambda i: (0, i)),
                   pl.BlockSpec((1, META_ROWS, LANES), lambda i: (i, 0, 0))],
        out_shape=[jax.ShapeDtypeStruct((t, d), F32), jax.ShapeDtypeStruct((t, d), F32),
                   jax.ShapeDtypeStruct((t, LANES), F32), jax.ShapeDtypeStruct((META_ROWS, t), F32),
                   jax.ShapeDtypeStruct((nt, META_ROWS, LANES), F32)],
        compiler_params=_params("parallel"),
        name="merge",
    )(x, mod, wts["norm1_g"], yc, oa, wts["w_in"], wts["w_conv_out"], wts["w_attn_out"], wts["w_o"],
      wts["norm2_g"], wts["wr_hi"], wts["wr_lo"], wts["b_router"], wts["tri"])

    outs = [x1, h2, route, meta, counts[:, 0, :N_EXPERTS]]
    if not latent:
        outs += [qkv_out[3], qkv_out[4]]
    return outs


def _routing_tables(meta, counts, tm, n_tiles):
    eid = meta[0:2].astype(jnp.int32)
    rank = meta[4:6].astype(jnp.int32)
    counts = counts.astype(jnp.int32)
    totals = counts.sum(axis=0)
    tiles_e = (totals + MOE_ROWS - 1) // MOE_ROWS
    tile_end = jnp.cumsum(tiles_e)
    row_start = (tile_end - tiles_e) * MOE_ROWS
    base = row_start[None, :] + jnp.cumsum(counts, axis=0) - counts
    base_tok = jnp.repeat(base.T, tm, axis=1)
    ex = jnp.arange(N_EXPERTS, dtype=jnp.int32)[None, :, None]
    pos = jnp.sum(jnp.where(eid[:, None, :] == ex, base_tok[None], 0), axis=1) + rank
    n_used = tile_end[-1]
    tile_ids = jnp.minimum(jnp.arange(n_tiles, dtype=jnp.int32), n_used - 1)
    tile_expert = jnp.sum((tile_ids[:, None] >= tile_end[None, :]).astype(jnp.int32), axis=1)
    last_tile_row = jnp.where(tiles_e > 0, (tile_end - 1) * MOE_ROWS, -1)
    return (pos.reshape(-1), tile_expert.astype(jnp.int32), n_used.astype(jnp.int32).reshape(1),
            last_tile_row.astype(jnp.int32))


def _routed_moe(streams, mod, w1, w3, w2, per_seq_tiles, tm):
    (x1p, h2p, rp, _, _), (x1s, h2s, rs, _, _) = streams
    d = x1p.shape[1]
    tp, ts = x1p.shape[0], x1s.shape[0]
    t = tp + ts
    n_tiles = 2 * t // MOE_ROWS + N_EXPERTS
    meta = jnp.concatenate([s[3] for s in streams], axis=1)
    counts = jnp.concatenate([s[4] for s in streams], axis=0)
    pos, tile_expert, n_used, last_tile_row = _routing_tables(meta, counts, tm, n_tiles)

    sorted_shape = jax.ShapeDtypeStruct((n_tiles * MOE_ROWS, d), F32)
    any_spec = pl.BlockSpec(memory_space=pl.ANY)
    n_p = tp // tm
    x_sorted = pl.pallas_call(
        functools.partial(_dispatch_kernel, n_p=n_p, n_tok=t, n_tiles=n_tiles),
        grid_spec=pltpu.PrefetchScalarGridSpec(
            num_scalar_prefetch=3, grid=(t // tm,),
            in_specs=[pl.BlockSpec((tm, d), lambda i, *_: (jnp.minimum(i, n_p - 1), 0)),
                      pl.BlockSpec((tm, d), lambda i, *_: (jnp.maximum(i - n_p, 0), 0))],
            out_specs=any_spec,
            scratch_shapes=[pltpu.VMEM((MOE_ROWS, d), F32), pltpu.SemaphoreType.DMA]),
        out_shape=sorted_shape,
        compiler_params=_params("arbitrary"),
        name="dispatch",
    )(pos, last_tile_row, n_used, h2p, h2s)

    def row_blk(j, te, nu):
        return (jnp.minimum(j, nu[0] - 1), 0)

    y_sorted = pl.pallas_call(
        _expert_kernel,
        grid_spec=pltpu.PrefetchScalarGridSpec(
            num_scalar_prefetch=2,
            grid=(n_tiles,),
            in_specs=[pl.BlockSpec((MOE_ROWS, d), row_blk),
                      pl.BlockSpec((1, d, D_EXPERT), lambda j, te, nu: (te[j], 0, 0)),
                      pl.BlockSpec((1, d, D_EXPERT), lambda j, te, nu: (te[j], 0, 0)),
                      pl.BlockSpec((1, D_EXPERT, d), lambda j, te, nu: (te[j], 0, 0))],
            out_specs=pl.BlockSpec((MOE_ROWS, d), lambda j, te, nu: (j, 0)),
            scratch_shapes=[pltpu.VMEM((d, D_EXPERT), BF16), pltpu.VMEM((d, D_EXPERT), BF16),
                            pltpu.VMEM((D_EXPERT, d), BF16)]),
        out_shape=jax.ShapeDtypeStruct((n_tiles * MOE_ROWS, d), F32),
        compiler_params=_params("arbitrary"),
        name="experts",
    )(tile_expert, n_used, x_sorted, w1, w3, w2)

    ctx_blk = lambda i, p: (jnp.minimum(i, n_p - 1), 0)
    lat_blk = lambda i, p: (jnp.maximum(i - n_p, 0), 0)
    return pl.pallas_call(
        functools.partial(_combine_kernel, n_p=n_p, n_tok=t, n_steps=t // tm),
        grid_spec=pltpu.PrefetchScalarGridSpec(
            num_scalar_prefetch=1, grid=(t // tm,),
            in_specs=[pl.BlockSpec((tm, d), ctx_blk), pl.BlockSpec((tm, d), lat_blk),
                      pl.BlockSpec((tm, LANES), ctx_blk), pl.BlockSpec((tm, LANES), lat_blk),
                      pl.BlockSpec((1, 6, d),
                                   lambda i, p: (jnp.where(i < n_p, 0, 1 + (i - n_p) // per_seq_tiles), 0, 0)),
                      any_spec],
            out_specs=[pl.BlockSpec((tm, d), ctx_blk), pl.BlockSpec((tm, d), lat_blk)],
            scratch_shapes=[pltpu.VMEM((2, 2, tm, d), F32), pltpu.SemaphoreType.DMA((2,))]),
        out_shape=[jax.ShapeDtypeStruct((tp, d), F32), jax.ShapeDtypeStruct((ts, d), F32)],
        compiler_params=_params("arbitrary"),
        name="combine",
    )(pos, x1p, x1s, rp, rs, mod, y_sorted)


def _rope_tables(n_tok):
    rows = n_tok // GRID_W
    pos_row = jnp.repeat(jnp.arange(rows, dtype=F32), GRID_W)
    pos_col = jnp.tile(jnp.arange(GRID_W, dtype=F32), rows)
    inv_freq = 1.0 / (ROPE_BASE ** (jnp.arange(0, AXIS_DIM, 2, dtype=F32) / AXIS_DIM))
    ang = jnp.concatenate([pos_row[:, None] * inv_freq[None, :]] * 2
                          + [pos_col[:, None] * inv_freq[None, :]] * 2, axis=1)
    sign = jnp.tile(jnp.concatenate([-jnp.ones((AXIS_DIM // 2,), F32), jnp.ones((AXIS_DIM // 2,), F32)]), 2)
    cos = jnp.cos(ang)
    sin = jnp.sin(ang) * sign[None, :]
    return jnp.tile(cos, (1, LANES // QK_DIM)), jnp.tile(sin, (1, LANES // QK_DIM))


def kernel(x_prompt, x_sample, cache_k, cache_v, c, c_ctx, w_ada, b_ada, norm1_g, w_in, conv_w, w_conv_out,
           q_norm_g, k_norm_g, lambda_q1, lambda_k1, lambda_q2, lambda_k2, subln_g, w_attn_out, w_o, norm2_g,
           w_grp, b_grp, w_exp, b_exp, w1, w3, w2):
    depth = w_in.shape[0]
    assert depth == 1
    d = D_MODEL
    dec_b, dec_l = x_sample.shape[0], x_sample.shape[1]
    cos, sin = _rope_tables(dec_l)
    bd = (jnp.arange(MXU_DIM)[:, None] // QK_DIM == jnp.arange(MXU_DIM)[None, :] // QK_DIM).astype(BF16)
    tri = (jnp.arange(MOE_TOKEN_TILE)[:, None] > jnp.arange(MOE_TOKEN_TILE)[None, :]).astype(BF16)
    xp, xs = x_prompt, x_sample
    new_k, new_v = [], []
    for l in range(depth):
        lam_init = 0.8 - 0.6 * math.exp(-0.3 * l)
        mod_rows = 16
        cvec = jnp.concatenate([c_ctx[None, :], c, jnp.zeros((mod_rows - 1 - dec_b, d), F32)], axis=0)
        mod = _ada(cvec, w_ada[l], b_ada[l][None, :]).reshape(mod_rows, 6, d)

        wi = w_in[l].astype(BF16)
        w_router = jnp.concatenate([w_grp[l], w_exp[l]], axis=1)
        w_router = jnp.pad(w_router, ((0, 0), (0, LANES - w_router.shape[1])))
        wr_hi = w_router.astype(BF16)
        wr_lo = (w_router - wr_hi.astype(F32)).astype(BF16)
        b_router = jnp.pad(jnp.concatenate([b_grp[l], b_exp[l]]), (0, LANES - N_GROUPS - N_EXPERTS))[None, :]
        wts = dict(
            norm1_g=norm1_g[l][None, :], norm2_g=norm2_g[l][None, :],
            w_in=wi,
            conv_w=conv_w[l], w_conv_out=w_conv_out[l].astype(BF16),
            gq=jnp.tile(q_norm_g[l], d // QK_DIM)[None, :], gk=jnp.tile(k_norm_g[l], d // QK_DIM)[None, :],
            w_kt=wi[:, 4 * d:5 * d].T, gkt=jnp.broadcast_to(jnp.tile(k_norm_g[l], d // QK_DIM)[:, None], (d, LANES)),
            bd=bd, cos=cos, sin=sin,
            lamv=jnp.stack([lambda_q1[l], lambda_k1[l], lambda_q2[l], lambda_k2[l]]),
            subln_g=subln_g[l][None, :], w_attn_out=w_attn_out[l].astype(BF16), w_o=w_o[l].astype(BF16),
            wr_hi=wr_hi, wr_lo=wr_lo, b_router=b_router, tri=tri,
        )
        *ctx, k_ctx, v_ctx = _layer(xp, mod, 0, False, None, wts, lam_init)
        new_k.append(k_ctx.reshape(x_prompt.shape[0], N_HEADS, 2, QK_DIM, x_prompt.shape[1]).transpose(0, 4, 1, 2, 3))
        new_v.append(v_ctx.reshape(x_prompt.shape[0], x_prompt.shape[1], N_HEADS, V_DIM))
        past = cache_k.shape[2]
        cache = (cache_k[:, l].transpose(0, 2, 3, 4, 1).reshape(dec_b, d, past), cache_v[:, l].reshape(dec_b, past, d))
        lat = _layer(xs, mod, 1, True, cache, wts, lam_init)
        tm = MOE_TOKEN_TILE
        yp, ys = _routed_moe((ctx, lat), mod, w1[l], w3[l], w2[l], dec_l // tm, tm)
        xp, xs = yp.reshape(x_prompt.shape), ys.reshape(x_sample.shape)
    return (xp, xs, jnp.stack(new_k, axis=1), jnp.stack(new_v, axis=1))
```

```python
import functools
import math

import jax
import jax.numpy as jnp
from jax import lax
from jax.experimental import pallas as pl
from jax.experimental.pallas import tpu as pltpu

D_MODEL = 1024
GRID_W = 64
N_HEADS = 8
QK_DIM = 64
V_DIM = 2 * QK_DIM
AXIS_DIM = QK_DIM // 2
ROPE_BASE = 10000.0
N_GROUPS = 4
EXPERTS_PER_GROUP = 4
N_EXPERTS = N_GROUPS * EXPERTS_PER_GROUP
D_EXPERT = 512
EPS = 1e-6

LANES = 128
MXU_DIM = 256
VMEM_LIMIT_BYTES = 56 * 1024 * 1024
ROUTER_LANE0 = N_GROUPS
CONV_TOKEN_TILE = 1024
MOE_TOKEN_TILE = 512
MOE_ROWS = 512
META_ROWS = 8
DMA_UNROLL = 512
ATTN_Q_ROWS = 256
ATTN_UNITS_PER_STEP = 8
ATTN_SCORES_AHEAD = 1

F32 = jnp.float32
BF16 = jnp.bfloat16


def _dot(a, b):
    return jnp.dot(a, b, preferred_element_type=F32)


def _dot_nt(a, b):
    return lax.dot_general(a, b, (((1,), (1,)), ((), ())), preferred_element_type=F32)


def _split_bf16(x):
    hi = x.astype(BF16)
    lo = (x - hi.astype(F32)).astype(BF16)
    return hi, lo


def _sigmoid(x):
    return 1.0 / (1.0 + jnp.exp(-x))


def _rms(x):
    return x * lax.rsqrt(jnp.mean(x * x, axis=-1, keepdims=True) + EPS)


def _params(*sem):
    return pltpu.CompilerParams(dimension_semantics=sem, vmem_limit_bytes=VMEM_LIMIT_BYTES)


def _ada_kernel(c_ref, w_ref, b_ref, o_ref):
    c = c_ref[...]
    s = c * _sigmoid(c)
    o_ref[...] = _dot(s.astype(BF16), w_ref[...].astype(BF16)) + b_ref[...]


def _ada(cvec, w_ada, b_ada):
    rows, d = cvec.shape
    n = w_ada.shape[1]
    tn = 1024
    return pl.pallas_call(
        _ada_kernel,
        grid=(n // tn,),
        in_specs=[pl.BlockSpec((rows, d), lambda j: (0, 0)),
                  pl.BlockSpec((d, tn), lambda j: (0, j)),
                  pl.BlockSpec((1, tn), lambda j: (0, j))],
        out_specs=pl.BlockSpec((rows, tn), lambda j: (0, j)),
        out_shape=jax.ShapeDtypeStruct((rows, n), F32),
        compiler_params=_params("arbitrary"),
        name="ada",
    )(cvec, w_ada, b_ada)


def _norm_mod(x, g, scale, shift):
    return (_rms(x) * g) * (1.0 + scale) + shift


def _conv_kernel(x_ref, mod_ref, g1_ref, w_ref, cw_ref, o_ref, *, seq_len):
    x = x_ref[...]
    tm = x.shape[0]
    hb = _norm_mod(x, g1_ref[...], mod_ref[0, 1:2, :], mod_ref[0, 0:1, :]).astype(BF16)
    pos = lax.broadcasted_iota(jnp.int32, (tm, 1), 0) % seq_len
    first = pos == 0
    last = pos == seq_len - 1
    for c in range(D_MODEL // MXU_DIM):
        sl = slice(c * MXU_DIM, (c + 1) * MXU_DIM)
        cb, cc, cx = (_dot(hb, w_ref[:, pl.ds(j * D_MODEL + c * MXU_DIM, MXU_DIM)]) for j in range(3))
        u = cc * cx
        up = jnp.where(first, 0.0, pltpu.roll(u, 1, 0))
        dn = jnp.where(last, 0.0, pltpu.roll(u, tm - 1, 0))
        cw = cw_ref[:, sl]
        conv = cw[0:1] * up + cw[1:2] * u + cw[2:3] * dn
        o_ref[:, sl] = (cb * conv).astype(BF16)


def _chunk_norm(z, bd, g):
    sq = (z * z).astype(BF16)
    ss = jnp.concatenate(
        [_dot(sq[:, s * MXU_DIM:(s + 1) * MXU_DIM], bd) for s in range(z.shape[1] // MXU_DIM)], axis=1)
    return z * lax.rsqrt(ss * (1.0 / QK_DIM) + EPS) * g


def _rope(z, cos, sin, hi_half):
    n = z.shape[1]
    swap = jnp.where(hi_half, pltpu.roll(z, AXIS_DIM // 2, 1), pltpu.roll(z, n - AXIS_DIM // 2, 1))
    return z * cos + swap * sin


def _qkv_kernel(*refs, latent):
    if latent:
        x_ref, mod_ref, g1_ref, w_ref, bd_ref, gq_ref, gk_ref, cos_ref, sin_ref, q_ref, k_ref, v_ref = refs
    else:
        x_ref, mod_ref, g1_ref, w_ref, wkt_ref, bd_ref, gq_ref, gkt_ref, q_ref, kt_ref, v_ref, knt_ref, vn_ref = refs
    hb = _norm_mod(x_ref[...], g1_ref[...], mod_ref[0, 1:2, :], mod_ref[0, 0:1, :]).astype(BF16)
    tm = hb.shape[0]
    bd = bd_ref[...]
    q = _chunk_norm(_dot(hb, w_ref[:, 0:D_MODEL]), bd, gq_ref[...])
    if latent:
        k = _chunk_norm(_dot(hb, w_ref[:, D_MODEL:2 * D_MODEL]), bd, gk_ref[...])
    v = _dot(hb, w_ref[:, 2 * D_MODEL:3 * D_MODEL])
    if latent:
        reps = D_MODEL // LANES
        cos = jnp.concatenate([cos_ref[...]] * reps, axis=1)
        sin = jnp.concatenate([sin_ref[...]] * reps, axis=1)
        lane = lax.broadcasted_iota(jnp.int32, (1, D_MODEL), 1)
        hi_half = (lane & (AXIS_DIM // 2)) != 0
        q = _rope(q, cos, sin, hi_half)
        k_att = _rope(k, cos, sin, hi_half)
    else:
        kt = _dot_nt(wkt_ref[...], hb)
        gkt = jnp.concatenate([gkt_ref[...]] * (tm // LANES), axis=1)
        slabs = []
        for s in range(D_MODEL // MXU_DIM):
            z = kt[s * MXU_DIM:(s + 1) * MXU_DIM, :]
            ss = _dot(bd, (z * z).astype(BF16))
            slabs.append(z * lax.rsqrt(ss * (1.0 / QK_DIM) + EPS))
        knt = jnp.concatenate(slabs, axis=0) * gkt
        seq = knt_ref.shape[2]
        for j in range(tm // seq):
            knt_ref[j] = knt[:, j * seq:(j + 1) * seq]
            kt_ref[j] = knt[:, j * seq:(j + 1) * seq].astype(BF16)
        vn_ref[...] = v
    q_ref[...] = (q * (math.log2(math.e) / math.sqrt(QK_DIM))).astype(BF16)
    if latent:
        k_ref[...] = k_att.astype(BF16)
    v_ref[...] = v.astype(BF16)


def _attn_kernel(*refs, latent, lam_init, tq):
    if latent:
        lamv_ref, q_ref, k_ref, v_ref, ck_ref, cv_ref, sg_ref, o_ref = refs
    else:
        lamv_ref, q_ref, k_ref, v_ref, sg_ref, o_ref = refs
    lv = lamv_ref[...]
    lam = (jnp.exp(jnp.sum(lv[0:1] * lv[1:2], axis=1, keepdims=True))
           - jnp.exp(jnp.sum(lv[2:3] * lv[3:4], axis=1, keepdims=True)) + lam_init)
    lane = lax.broadcasted_iota(jnp.int32, (1, V_DIM), 1)
    n_q = q_ref.shape[1]
    units = [(hh, c) for hh in range(q_ref.shape[2] // V_DIM) for c in range(n_q // tq)]

    def kv(hh):
        hs = slice(hh * V_DIM, (hh + 1) * V_DIM)
        vals = [v_ref[0, :, hs]]
        if latent:
            keys = [(k_ref[0, :, hs], False), (ck_ref[0, hs, :].astype(BF16), True)]
            g = pl.program_id(1) * (q_ref.shape[2] // V_DIM) + hh
            vals.append(cv_ref[0, pl.ds(g, cv_ref.shape[1] // N_HEADS, stride=N_HEADS), :].astype(BF16))
        else:
            keys = [(k_ref[0, hs, :], True)]
        return keys, vals

    def scores(u):
        hh, c = units[u]
        q = q_ref[0, c * tq:(c + 1) * tq, hh * V_DIM:(hh + 1) * V_DIM]
        keys, _ = kv(hh)
        halves = (jnp.where(lane < QK_DIM, q, jnp.zeros_like(q)), jnp.where(lane >= QK_DIM, q, jnp.zeros_like(q)))
        return [[_dot(qh, kk) if feature_major else _dot_nt(qh, kk) for kk, feature_major in keys] for qh in halves]

    ones_col = jnp.where(lane == 0, 1.0, 0.0).astype(BF16)

    def softmax_pv(s, vals):
        m = s[0].max(axis=1, keepdims=True)
        for si in s[1:]:
            m = jnp.maximum(m, si.max(axis=1, keepdims=True))
        acc = None
        for si, vv in zip(s, vals):
            v_aug = jnp.concatenate([vv, jnp.broadcast_to(ones_col, vv.shape)], axis=1)
            part = _dot(jnp.exp2(si - m).astype(BF16), v_aug)
            acc = part if acc is None else acc + part
        return acc[:, :V_DIM] * (1.0 / acc[:, V_DIM:V_DIM + 1])

    def finish(u, s):
        hh, c = units[u]
        _, vals = kv(hh)
        o = softmax_pv(s[0], vals) - lam * softmax_pv(s[1], vals)
        o_ref[0, c * tq:(c + 1) * tq, hh * V_DIM:(hh + 1) * V_DIM] = (
            _rms(o) * sg_ref[...] * (1.0 - lam_init)).astype(BF16)

    ahead = ATTN_SCORES_AHEAD
    pending = [scores(u) for u in range(min(ahead, len(units)))]
    for u in range(len(units)):
        if u + ahead < len(units):
            pending.append(scores(u + ahead))
        finish(u, pending.pop(0))


def _merge_kernel(x_ref, mod_ref, g1_ref, yc_ref, oa_ref, wg_ref, wco_ref, wao_ref, wo_ref, g2_ref,
                  wrh_ref, wrl_ref, br_ref, tri_ref, x1_ref, h2_ref, route_ref, meta_ref, cnt_ref):
    x = x_ref[...]
    hb = _norm_mod(x, g1_ref[...], mod_ref[0, 1:2, :], mod_ref[0, 0:1, :]).astype(BF16)
    y_conv = _dot(yc_ref[...], wco_ref[...])
    y_attn = _dot(oa_ref[...], wao_ref[...])
    mixed = (_sigmoid(_dot(hb, wg_ref[:, 0:D_MODEL])) * y_conv
             + _sigmoid(_dot(hb, wg_ref[:, D_MODEL:2 * D_MODEL])) * y_attn)
    x1 = x + mod_ref[0, 2:3, :] * _dot(mixed.astype(BF16), wo_ref[...])
    x1_ref[...] = x1
    h2 = _norm_mod(x1, g2_ref[...], mod_ref[0, 4:5, :], mod_ref[0, 3:4, :])
    h2_ref[...] = h2

    h_hi, h_lo = _split_bf16(h2)
    logits = _dot(h_hi, wrh_ref[...]) + _dot(h_lo, wrh_ref[...]) + _dot(h_hi, wrl_ref[...]) + br_ref[...]
    tm = x.shape[0]
    lane = lax.broadcasted_iota(jnp.int32, (tm, LANES), 1).astype(F32)
    ninf = -jnp.inf
    big = float(LANES)

    def top(vals):
        vmax = vals.max(axis=1, keepdims=True)
        idx = jnp.where(vals == vmax, lane, big).min(axis=1, keepdims=True)
        return vmax, idx

    is_grp = lane < N_GROUPS
    gmax, gidx = top(jnp.where(is_grp, logits, ninf))
    g_p = 1.0 / jnp.where(is_grp, jnp.exp(logits - gmax), 0.0).sum(axis=1, keepdims=True)
    lo = ROUTER_LANE0 + EXPERTS_PER_GROUP * gidx
    el = jnp.where((lane >= lo) & (lane < lo + EXPERTS_PER_GROUP), logits, ninf)
    v1, i1 = top(el)
    v2, i2 = top(jnp.where(lane == i1, ninf, el))
    e2 = jnp.exp(v2 - v1)
    den = 1.0 / (1.0 + e2)
    e0 = i1 - ROUTER_LANE0
    e1 = i2 - ROUTER_LANE0
    onehot = jnp.where((lane == e0) | (lane == e1), 1.0, 0.0)
    before = _dot(tri_ref[...], onehot.astype(BF16))
    r0 = jnp.where(lane == e0, before, 0.0).sum(axis=1, keepdims=True)
    r1 = jnp.where(lane == e1, before, 0.0).sum(axis=1, keepdims=True)
    route = (jnp.where(lane == 0.0, e0, 0.0) + jnp.where(lane == 1.0, e1, 0.0)
             + jnp.where(lane == 2.0, den * g_p, 0.0) + jnp.where(lane == 3.0, e2 * den * g_p, 0.0)
             + jnp.where(lane == 4.0, r0, 0.0) + jnp.where(lane == 5.0, r1, 0.0))
    route_ref[...] = route
    meta_ref[...] = jnp.transpose(route)[:META_ROWS, :]
    cnt_ref[0] = jnp.broadcast_to(onehot.sum(axis=0, keepdims=True), (META_ROWS, LANES))


def _row_copy(src, dst, sem):
    return pltpu.make_async_copy(src, dst, sem)


def _for_groups(n_groups, body):
    if n_groups == 1:
        body(0, 0)
    else:
        lax.fori_loop(0, n_groups, body, 0)


def _dispatch_kernel(pos_ref, last_ref, nu_ref, h2p_ref, h2s_ref, xs_ref, zero_buf, sem, *, n_p, n_tok, n_tiles):
    i = pl.program_id(0)
    tm = h2p_ref.shape[0]

    @pl.when(i == 0)
    def _():
        zero_buf[...] = jnp.zeros_like(zero_buf)

        def zero_tile(row):
            return _row_copy(zero_buf, xs_ref.at[pl.ds(pl.multiple_of(row, MOE_ROWS), MOE_ROWS)], sem)

        for e in range(N_EXPERTS):
            @pl.when(last_ref[e] >= 0)
            def _():
                zero_tile(last_ref[e]).start()

        def start_tail(j, carry):
            zero_tile(j * MOE_ROWS).start()
            return carry

        lax.fori_loop(nu_ref[0], n_tiles, start_tail, 0)
        for e in range(N_EXPERTS):
            @pl.when(last_ref[e] >= 0)
            def _():
                zero_tile(0).wait()

        def wait_tail(j, carry):
            zero_tile(0).wait()
            return carry

        lax.fori_loop(nu_ref[0], n_tiles, wait_tail, 0)

    base = i * tm

    def scatter(h2_ref):
        def issue(g, carry):
            for u in range(DMA_UNROLL):
                r = g * DMA_UNROLL + u
                for slot in range(2):
                    p = pos_ref[slot * n_tok + base + r]
                    _row_copy(h2_ref.at[pl.ds(r, 1)], xs_ref.at[pl.ds(p, 1)], sem).start(priority=slot)
            return carry

        _for_groups(tm // DMA_UNROLL, issue)
        for slot in range(2):
            _row_copy(h2_ref, xs_ref.at[pl.ds(0, tm)], sem).wait()

    @pl.when(i < n_p)
    def _():
        scatter(h2p_ref)

    @pl.when(i >= n_p)
    def _():
        scatter(h2s_ref)


def _expert_kernel(te_ref, nu_ref, x_ref, w1_ref, w3_ref, w2_ref, y_ref, w1b, w3b, w2b):
    j = pl.program_id(0)

    @pl.when((j == 0) | (te_ref[j] != te_ref[jnp.maximum(j - 1, 0)]))
    def _():
        w1b[...] = w1_ref[0].astype(BF16)
        w3b[...] = w3_ref[0].astype(BF16)
        w2b[...] = w2_ref[0].astype(BF16)

    @pl.when(j < nu_ref[0])
    def _():
        t = x_ref[...].astype(BF16)
        a = _dot(t, w1b[...])
        he = (a * _sigmoid(a)) * _dot(t, w3b[...])
        y_ref[...] = _dot(he.astype(BF16), w2b[...])

    @pl.when(j >= nu_ref[0])
    def _():
        y_ref[...] = jnp.zeros_like(y_ref)


def _combine_kernel(pos_ref, x1p_ref, x1s_ref, rp_ref, rs_ref, mod_ref, ys_ref, op_ref, os_ref, ybuf, sem,
                    *, n_p, n_tok, n_steps):
    i = pl.program_id(0)
    tm = ybuf.shape[2]

    def gather(tile, buf):
        base = tile * tm
        for r in range(tm):
            for slot in range(2):
                p = pos_ref[slot * n_tok + base + r]
                _row_copy(ys_ref.at[pl.ds(p, 1)], ybuf.at[buf, slot, pl.ds(r, 1)], sem.at[buf]).start(priority=slot)

    @pl.when(i == 0)
    def _():
        gather(0, 0)

    cur = i % 2

    @pl.when(i + 1 < n_steps)
    def _():
        gather(i + 1, 1 - cur)

    for slot in range(2):
        _row_copy(ybuf.at[cur, slot], ybuf.at[cur, slot], sem.at[cur]).wait()

    def out(x1_ref, r_ref, o_ref):
        r = r_ref[...]
        o_ref[...] = x1_ref[...] + mod_ref[0, 5:6, :] * (r[:, 2:3] * ybuf[cur, 0] + r[:, 3:4] * ybuf[cur, 1])

    @pl.when(i < n_p)
    def _():
        out(x1p_ref, rp_ref, op_ref)

    @pl.when(i >= n_p)
    def _():
        out(x1s_ref, rs_ref, os_ref)


def _layer(x3, mod, row0, latent, cache, wts, lam_init):
    b, l, d = x3.shape
    t = b * l
    x = x3.reshape(t, d)
    tm = MOE_TOKEN_TILE
    nt = t // tm
    per_seq = l // tm if l >= tm else 1
    if latent:
        mod_row = lambda i: row0 + i // per_seq
    else:
        mod_row = lambda i: row0

    x_spec = pl.BlockSpec((tm, d), lambda i: (i, 0))
    mod_spec = pl.BlockSpec((1, 6, d), lambda i: (mod_row(i), 0, 0))
    row_spec = pl.BlockSpec((1, d), lambda i: (0, 0))
    act_spec = pl.BlockSpec((tm, d), lambda i: (i, 0))

    def full(a):
        nd = a.ndim
        return pl.BlockSpec(a.shape, lambda i: (0,) * nd)

    def w_in_cols(width, idx):
        return pl.BlockSpec((d, width), lambda i: (0, idx))

    tmc = max(CONV_TOKEN_TILE, l)
    assert tmc % l == 0 and t % tmc == 0
    seq_per_tile = tmc // l
    xc_spec = pl.BlockSpec((tmc, d), lambda i: (i, 0))
    modc_spec = pl.BlockSpec((1, 6, d), lambda i: (row0 + (i * seq_per_tile if latent else 0), 0, 0))
    assert not latent or seq_per_tile == 1
    yc = pl.pallas_call(
        functools.partial(_conv_kernel, seq_len=l),
        grid=(t // tmc,),
        in_specs=[xc_spec, modc_spec, row_spec, w_in_cols(3 * d, 0), full(wts["conv_w"])],
        out_specs=xc_spec,
        out_shape=jax.ShapeDtypeStruct((t, d), BF16),
        compiler_params=_params("parallel"),
        name="conv_in",
    )(x, mod, wts["norm1_g"], wts["w_in"], wts["conv_w"])

    if latent:
        qkv_in = [x, mod, wts["norm1_g"], wts["w_in"], wts["bd"], wts["gq"], wts["gk"], wts["cos"], wts["sin"]]
        tab_spec = pl.BlockSpec((tm, LANES), lambda i: (i % per_seq, 0))
        qkv_specs = [x_spec, mod_spec, row_spec, w_in_cols(3 * d, 1), full(wts["bd"]), row_spec, row_spec,
                     tab_spec, tab_spec]
        out_shapes = [jax.ShapeDtypeStruct((t, d), BF16)] * 3
        out_specs = [act_spec] * 3
    else:
        assert tm % l == 0
        kt_spec = pl.BlockSpec((tm // l, d, l), lambda i: (i, 0, 0))
        qkv_in = [x, mod, wts["norm1_g"], wts["w_in"], wts["w_kt"], wts["bd"], wts["gq"], wts["gkt"]]
        qkv_specs = [x_spec, mod_spec, row_spec, w_in_cols(3 * d, 1), full(wts["w_kt"]), full(wts["bd"]), row_spec,
                     full(wts["gkt"])]
        out_shapes = [jax.ShapeDtypeStruct((t, d), BF16), jax.ShapeDtypeStruct((b, d, l), BF16),
                      jax.ShapeDtypeStruct((t, d), BF16), jax.ShapeDtypeStruct((b, d, l), F32),
                      jax.ShapeDtypeStruct((t, d), F32)]
        out_specs = [act_spec, kt_spec, act_spec, kt_spec, act_spec]
    qkv_out = pl.pallas_call(
        functools.partial(_qkv_kernel, latent=latent),
        grid=(nt,),
        in_specs=qkv_specs,
        out_specs=out_specs,
        out_shape=out_shapes,
        compiler_params=_params("parallel"),
        name="qkv",
    )(*qkv_in)
    q, k, v = qkv_out[:3]

    tq = min(ATTN_Q_ROWS, l)
    heads_per_step = max(1, ATTN_UNITS_PER_STEP // (l // tq))
    hw = heads_per_step * V_DIM
    q3, v3 = q.reshape(b, l, d), v.reshape(b, l, d)
    qkv_spec = pl.BlockSpec((1, l, hw), lambda bi, h: (bi, 0, h))
    kt_blk = lambda n_tok: pl.BlockSpec((1, hw, n_tok), lambda bi, h: (bi, h, 0))
    sg_spec = pl.BlockSpec((1, V_DIM), lambda bi, h: (0, 0))
    lam_spec = pl.BlockSpec(wts["lamv"].shape, lambda bi, h: (0, 0))
    if latent:
        ckt, cv = cache
        attn_in = [wts["lamv"], q3, k.reshape(b, l, d), v3, ckt, cv]
        attn_specs = [lam_spec, qkv_spec, qkv_spec, qkv_spec, kt_blk(ckt.shape[2]),
                      pl.BlockSpec((1, cv.shape[1], V_DIM), lambda bi, h: (bi, 0, 0))]
    else:
        attn_in = [wts["lamv"], q3, k, v3]
        attn_specs = [lam_spec, qkv_spec, kt_blk(l), qkv_spec]
    attn_in.append(wts["subln_g"])
    attn_specs.append(sg_spec)
    oa = pl.pallas_call(
        functools.partial(_attn_kernel, latent=latent, lam_init=lam_init, tq=tq),
        grid=(b, N_HEADS // heads_per_step),
        in_specs=attn_specs,
        out_specs=qkv_spec,
        out_shape=jax.ShapeDtypeStruct((b, l, d), BF16),
        compiler_params=_params("parallel", "parallel"),
        name="attn",
    )(*attn_in).reshape(t, d)

    lane_spec = pl.BlockSpec((tm, LANES), lambda i: (i, 0))
    x1, h2, route, meta, counts = pl.pallas_call(
        _merge_kernel,
        grid=(nt,),
        in_specs=[x_spec, mod_spec, row_spec, act_spec, act_spec, w_in_cols(2 * d, 3), full(wts["w_conv_out"]),
                  full(wts["w_attn_out"]), full(wts["w_o"]), row_spec, full(wts["wr_hi"]), full(wts["wr_lo"]),
                  full(wts["b_router"]), full(wts["tri"])],
        out_specs=[x_spec, x_spec, lane_spec, pl.BlockSpec((META_ROWS, tm), lambda i: (0, i)),
                   pl.BlockSpec((1, META_ROWS, LANES), lambda i: (i, 0, 0))],
        out_shape=[jax.ShapeDtypeStruct((t, d), F32), jax.ShapeDtypeStruct((t, d), F32),
                   jax.ShapeDtypeStruct((t, LANES), F32), jax.ShapeDtypeStruct((META_ROWS, t), F32),
                   jax.ShapeDtypeStruct((nt, META_ROWS, LANES), F32)],
        compiler_params=_params("parallel"),
        name="merge",
    )(x, mod, wts["norm1_g"], yc, oa, wts["w_in"], wts["w_conv_out"], wts["w_attn_out"], wts["w_o"],
      wts["norm2_g"], wts["wr_hi"], wts["wr_lo"], wts["b_router"], wts["tri"])

    outs = [x1, h2, route, meta, counts[:, 0, :N_EXPERTS]]
    if not latent:
        outs += [qkv_out[3], qkv_out[4]]
    return outs


def _routing_tables(meta, counts, tm, n_tiles):
    eid = meta[0:2].astype(jnp.int32)
    rank = meta[4:6].astype(jnp.int32)
    counts = counts.astype(jnp.int32)
    totals = counts.sum(axis=0)
    tiles_e = (totals + MOE_ROWS - 1) // MOE_ROWS
    tile_end = jnp.cumsum(tiles_e)
    row_start = (tile_end - tiles_e) * MOE_ROWS
    base = row_start[None, :] + jnp.cumsum(counts, axis=0) - counts
    base_tok = jnp.repeat(base.T, tm, axis=1)
    ex = jnp.arange(N_EXPERTS, dtype=jnp.int32)[None, :, None]
    pos = jnp.sum(jnp.where(eid[:, None, :] == ex, base_tok[None], 0), axis=1) + rank
    n_used = tile_end[-1]
    tile_ids = jnp.minimum(jnp.arange(n_tiles, dtype=jnp.int32), n_used - 1)
    tile_expert = jnp.sum((tile_ids[:, None] >= tile_end[None, :]).astype(jnp.int32), axis=1)
    last_tile_row = jnp.where(tiles_e > 0, (tile_end - 1) * MOE_ROWS, -1)
    return (pos.reshape(-1), tile_expert.astype(jnp.int32), n_used.astype(jnp.int32).reshape(1),
            last_tile_row.astype(jnp.int32))


def _routed_moe(streams, mod, w1, w3, w2, per_seq_tiles, tm):
    (x1p, h2p, rp, _, _), (x1s, h2s, rs, _, _) = streams
    d = x1p.shape[1]
    tp, ts = x1p.shape[0], x1s.shape[0]
    t = tp + ts
    n_tiles = 2 * t // MOE_ROWS + N_EXPERTS
    meta = jnp.concatenate([s[3] for s in streams], axis=1)
    counts = jnp.concatenate([s[4] for s in streams], axis=0)
    pos, tile_expert, n_used, last_tile_row = _routing_tables(meta, counts, tm, n_tiles)

    sorted_shape = jax.ShapeDtypeStruct((n_tiles * MOE_ROWS, d), F32)
    any_spec = pl.BlockSpec(memory_space=pl.ANY)
    n_p = tp // tm
    x_sorted = pl.pallas_call(
        functools.partial(_dispatch_kernel, n_p=n_p, n_tok=t, n_tiles=n_tiles),
        grid_spec=pltpu.PrefetchScalarGridSpec(
            num_scalar_prefetch=3, grid=(t // tm,),
            in_specs=[pl.BlockSpec((tm, d), lambda i, *_: (jnp.minimum(i, n_p - 1), 0)),
                      pl.BlockSpec((tm, d), lambda i, *_: (jnp.maximum(i - n_p, 0), 0))],
            out_specs=any_spec,
            scratch_shapes=[pltpu.VMEM((MOE_ROWS, d), F32), pltpu.SemaphoreType.DMA]),
        out_shape=sorted_shape,
        compiler_params=_params("arbitrary"),
        name="dispatch",
    )(pos, last_tile_row, n_used, h2p, h2s)

    def row_blk(j, te, nu):
        return (jnp.minimum(j, nu[0] - 1), 0)

    y_sorted = pl.pallas_call(
        _expert_kernel,
        grid_spec=pltpu.PrefetchScalarGridSpec(
            num_scalar_prefetch=2,
            grid=(n_tiles,),
            in_specs=[pl.BlockSpec((MOE_ROWS, d), row_blk),
                      pl.BlockSpec((1, d, D_EXPERT), lambda j, te, nu: (te[j], 0, 0)),
                      pl.BlockSpec((1, d, D_EXPERT), lambda j, te, nu: (te[j], 0, 0)),
                      pl.BlockSpec((1, D_EXPERT, d), lambda j, te, nu: (te[j], 0, 0))],
            out_specs=pl.BlockSpec((MOE_ROWS, d), lambda j, te, nu: (j, 0)),
            scratch_shapes=[pltpu.VMEM((d, D_EXPERT), BF16), pltpu.VMEM((d, D_EXPERT), BF16),
                            pltpu.VMEM((D_EXPERT, d), BF16)]),
        out_shape=jax.ShapeDtypeStruct((n_tiles * MOE_ROWS, d), F32),
        compiler_params=_params("arbitrary"),
        name="experts",
    )(tile_expert, n_used, x_sorted, w1, w3, w2)

    ctx_blk = lambda i, p: (jnp.minimum(i, n_p - 1), 0)
    lat_blk = lambda i, p: (jnp.maximum(i - n_p, 0), 0)
    return pl.pallas_call(
        functools.partial(_combine_kernel, n_p=n_p, n_tok=t, n_steps=t // tm),
        grid_spec=pltpu.PrefetchScalarGridSpec(
            num_scalar_prefetch=1, grid=(t // tm,),
            in_specs=[pl.BlockSpec((tm, d), ctx_blk), pl.BlockSpec((tm, d), lat_blk),
                      pl.BlockSpec((tm, LANES), ctx_blk), pl.BlockSpec((tm, LANES), lat_blk),
                      pl.BlockSpec((1, 6, d),
                                   lambda i, p: (jnp.where(i < n_p, 0, 1 + (i - n_p) // per_seq_tiles), 0, 0)),
                      any_spec],
            out_specs=[pl.BlockSpec((tm, d), ctx_blk), pl.BlockSpec((tm, d), lat_blk)],
            scratch_shapes=[pltpu.VMEM((2, 2, tm, d), F32), pltpu.SemaphoreType.DMA((2,))]),
        out_shape=[jax.ShapeDtypeStruct((tp, d), F32), jax.ShapeDtypeStruct((ts, d), F32)],
        compiler_params=_params("arbitrary"),
        name="combine",
    )(pos, x1p, x1s, rp, rs, mod, y_sorted)


def _rope_tables(n_tok):
    rows = n_tok // GRID_W
    pos_row = jnp.repeat(jnp.arange(rows, dtype=F32), GRID_W)
    pos_col = jnp.tile(jnp.arange(GRID_W, dtype=F32), rows)
    inv_freq = 1.0 / (ROPE_BASE ** (jnp.arange(0, AXIS_DIM, 2, dtype=F32) / AXIS_DIM))
    ang = jnp.concatenate([pos_row[:, None] * inv_freq[None, :]] * 2
                          + [pos_col[:, None] * inv_freq[None, :]] * 2, axis=1)
    sign = jnp.tile(jnp.concatenate([-jnp.ones((AXIS_DIM // 2,), F32), jnp.ones((AXIS_DIM // 2,), F32)]), 2)
    cos = jnp.cos(ang)
    sin = jnp.sin(ang) * sign[None, :]
    return jnp.tile(cos, (1, LANES // QK_DIM)), jnp.tile(sin, (1, LANES // QK_DIM))


def kernel(x_prompt, x_sample, cache_k, cache_v, c, c_ctx, w_ada, b_ada, norm1_g, w_in, conv_w, w_conv_out,
           q_norm_g, k_norm_g, lambda_q1, lambda_k1, lambda_q2, lambda_k2, subln_g, w_attn_out, w_o, norm2_g,
           w_grp, b_grp, w_exp, b_exp, w1, w3, w2):
    depth = w_in.shape[0]
    assert depth == 1
    d = D_MODEL
    dec_b, dec_l = x_sample.shape[0], x_sample.shape[1]
    cos, sin = _rope_tables(dec_l)
    bd = (jnp.arange(MXU_DIM)[:, None] // QK_DIM == jnp.arange(MXU_DIM)[None, :] // QK_DIM).astype(BF16)
    tri = (jnp.arange(MOE_TOKEN_TILE)[:, None] > jnp.arange(MOE_TOKEN_TILE)[None, :]).astype(BF16)
    xp, xs = x_prompt, x_sample
    new_k, new_v = [], []
    for l in range(depth):
        lam_init = 0.8 - 0.6 * math.exp(-0.3 * l)
        mod_rows = 16
        cvec = jnp.concatenate([c_ctx[None, :], c, jnp.zeros((mod_rows - 1 - dec_b, d), F32)], axis=0)
        mod = _ada(cvec, w_ada[l], b_ada[l][None, :]).reshape(mod_rows, 6, d)

        wi = w_in[l].astype(BF16)
        w_router = jnp.concatenate([w_grp[l], w_exp[l]], axis=1)
        w_router = jnp.pad(w_router, ((0, 0), (0, LANES - w_router.shape[1])))
        wr_hi = w_router.astype(BF16)
        wr_lo = (w_router - wr_hi.astype(F32)).astype(BF16)
        b_router = jnp.pad(jnp.concatenate([b_grp[l], b_exp[l]]), (0, LANES - N_GROUPS - N_EXPERTS))[None, :]
        wts = dict(
            norm1_g=norm1_g[l][None, :], norm2_g=norm2_g[l][None, :],
            w_in=wi,
            conv_w=conv_w[l], w_conv_out=w_conv_out[l].astype(BF16),
            gq=jnp.tile(q_norm_g[l], d // QK_DIM)[None, :], gk=jnp.tile(k_norm_g[l], d // QK_DIM)[None, :],
            w_kt=wi[:, 4 * d:5 * d].T, gkt=jnp.broadcast_to(jnp.tile(k_norm_g[l], d // QK_DIM)[:, None], (d, LANES)),
            bd=bd, cos=cos, sin=sin,
            lamv=jnp.stack([lambda_q1[l], lambda_k1[l], lambda_q2[l], lambda_k2[l]]),
            subln_g=subln_g[l][None, :], w_attn_out=w_attn_out[l].astype(BF16), w_o=w_o[l].astype(BF16),
            wr_hi=wr_hi, wr_lo=wr_lo, b_router=b_router, tri=tri,
        )
        *ctx, k_ctx, v_ctx = _layer(xp, mod, 0, False, None, wts, lam_init)
        new_k.append(k_ctx.reshape(x_prompt.shape[0], N_HEADS, 2, QK_DIM, x_prompt.shape[1]).transpose(0, 4, 1, 2, 3))
        new_v.append(v_ctx.reshape(x_prompt.shape[0], x_prompt.shape[1], N_HEADS, V_DIM))
        past = cache_k.shape[2]
        cache = (cache_k[:, l].transpose(0, 2, 3, 4, 1).reshape(dec_b, d, past), cache_v[:, l].reshape(dec_b, past * N_HEADS, V_DIM))
        lat = _layer(xs, mod, 1, True, cache, wts, lam_init)
        tm = MOE_TOKEN_TILE
        yp, ys = _routed_moe((ctx, lat), mod, w1[l], w3[l], w2[l], dec_l // tm, tm)
        xp, xs = yp.reshape(x_prompt.shape), ys.reshape(x_sample.shape)
    return (xp, xs, jnp.stack(new_k, axis=1), jnp.stack(new_v, axis=1))
```

```python
import functools
import math

import jax
import jax.numpy as jnp
from jax import lax
from jax.experimental import pallas as pl
from jax.experimental.pallas import tpu as pltpu

D_MODEL = 1024
GRID_W = 64
N_HEADS = 8
QK_DIM = 64
V_DIM = 2 * QK_DIM
AXIS_DIM = QK_DIM // 2
ROPE_BASE = 10000.0
N_GROUPS = 4
EXPERTS_PER_GROUP = 4
N_EXPERTS = N_GROUPS * EXPERTS_PER_GROUP
D_EXPERT = 512
EPS = 1e-6

LANES = 128
MXU_DIM = 256
VMEM_LIMIT_BYTES = 56 * 1024 * 1024
ROUTER_LANE0 = N_GROUPS
CONV_TOKEN_TILE = 1024
MOE_TOKEN_TILE = 512
MOE_ROWS = 512
META_ROWS = 8
DMA_UNROLL = 512
ATTN_Q_ROWS = 256
ATTN_UNITS_PER_STEP = 8
ATTN_SCORES_AHEAD = 1

F32 = jnp.float32
BF16 = jnp.bfloat16


def _dot(a, b):
    return jnp.dot(a, b, preferred_element_type=F32)


def _dot_nt(a, b):
    return lax.dot_general(a, b, (((1,), (1,)), ((), ())), preferred_element_type=F32)


def _split_bf16(x):
    hi = x.astype(BF16)
    lo = (x - hi.astype(F32)).astype(BF16)
    return hi, lo


def _sigmoid(x):
    return 1.0 / (1.0 + jnp.exp(-x))


def _rms(x):
    return x * lax.rsqrt(jnp.mean(x * x, axis=-1, keepdims=True) + EPS)


def _params(*sem):
    return pltpu.CompilerParams(dimension_semantics=sem, vmem_limit_bytes=VMEM_LIMIT_BYTES)


def _ada_kernel(c_ref, w_ref, b_ref, o_ref):
    c = c_ref[...]
    s = c * _sigmoid(c)
    o_ref[...] = _dot(s.astype(BF16), w_ref[...].astype(BF16)) + b_ref[...]


def _ada(cvec, w_ada, b_ada):
    rows, d = cvec.shape
    n = w_ada.shape[1]
    tn = 1024
    return pl.pallas_call(
        _ada_kernel,
        grid=(n // tn,),
        in_specs=[pl.BlockSpec((rows, d), lambda j: (0, 0)),
                  pl.BlockSpec((d, tn), lambda j: (0, j)),
                  pl.BlockSpec((1, tn), lambda j: (0, j))],
        out_specs=pl.BlockSpec((rows, tn), lambda j: (0, j)),
        out_shape=jax.ShapeDtypeStruct((rows, n), F32),
        compiler_params=_params("arbitrary"),
        name="ada",
    )(cvec, w_ada, b_ada)


def _norm_mod(x, g, scale, shift):
    return (_rms(x) * g) * (1.0 + scale) + shift


def _conv_kernel(x_ref, mod_ref, g1_ref, w_ref, cw_ref, o_ref, *, seq_len):
    x = x_ref[...]
    tm = x.shape[0]
    hb = _norm_mod(x, g1_ref[...], mod_ref[0, 1:2, :], mod_ref[0, 0:1, :]).astype(BF16)
    pos = lax.broadcasted_iota(jnp.int32, (tm, 1), 0) % seq_len
    first = pos == 0
    last = pos == seq_len - 1
    for c in range(D_MODEL // MXU_DIM):
        sl = slice(c * MXU_DIM, (c + 1) * MXU_DIM)
        cb, cc, cx = (_dot(hb, w_ref[:, pl.ds(j * D_MODEL + c * MXU_DIM, MXU_DIM)]) for j in range(3))
        u = cc * cx
        up = jnp.where(first, 0.0, pltpu.roll(u, 1, 0))
        dn = jnp.where(last, 0.0, pltpu.roll(u, tm - 1, 0))
        cw = cw_ref[:, sl]
        conv = cw[0:1] * up + cw[1:2] * u + cw[2:3] * dn
        o_ref[:, sl] = (cb * conv).astype(BF16)


def _chunk_norm(z, bd, g):
    sq = (z * z).astype(BF16)
    ss = jnp.concatenate(
        [_dot(sq[:, s * MXU_DIM:(s + 1) * MXU_DIM], bd) for s in range(z.shape[1] // MXU_DIM)], axis=1)
    return z * lax.rsqrt(ss * (1.0 / QK_DIM) + EPS) * g


def _rope(z, cos, sin, hi_half):
    n = z.shape[1]
    swap = jnp.where(hi_half, pltpu.roll(z, AXIS_DIM // 2, 1), pltpu.roll(z, n - AXIS_DIM // 2, 1))
    return z * cos + swap * sin


def _qkv_kernel(*refs, latent):
    if latent:
        x_ref, mod_ref, g1_ref, w_ref, bd_ref, gq_ref, gk_ref, cos_ref, sin_ref, q_ref, k_ref, v_ref = refs
    else:
        x_ref, mod_ref, g1_ref, w_ref, wkt_ref, bd_ref, gq_ref, gkt_ref, q_ref, kt_ref, v_ref, knt_ref, vn_ref = refs
    hb = _norm_mod(x_ref[...], g1_ref[...], mod_ref[0, 1:2, :], mod_ref[0, 0:1, :]).astype(BF16)
    tm = hb.shape[0]
    bd = bd_ref[...]
    q = _chunk_norm(_dot(hb, w_ref[:, 0:D_MODEL]), bd, gq_ref[...])
    if latent:
        k = _chunk_norm(_dot(hb, w_ref[:, D_MODEL:2 * D_MODEL]), bd, gk_ref[...])
    v = _dot(hb, w_ref[:, 2 * D_MODEL:3 * D_MODEL])
    if latent:
        reps = D_MODEL // LANES
        cos = jnp.concatenate([cos_ref[...]] * reps, axis=1)
        sin = jnp.concatenate([sin_ref[...]] * reps, axis=1)
        lane = lax.broadcasted_iota(jnp.int32, (1, D_MODEL), 1)
        hi_half = (lane & (AXIS_DIM // 2)) != 0
        q = _rope(q, cos, sin, hi_half)
        k_att = _rope(k, cos, sin, hi_half)
    else:
        kt = _dot_nt(wkt_ref[...], hb)
        gkt = jnp.concatenate([gkt_ref[...]] * (tm // LANES), axis=1)
        slabs = []
        for s in range(D_MODEL // MXU_DIM):
            z = kt[s * MXU_DIM:(s + 1) * MXU_DIM, :]
            ss = _dot(bd, (z * z).astype(BF16))
            slabs.append(z * lax.rsqrt(ss * (1.0 / QK_DIM) + EPS))
        knt = jnp.concatenate(slabs, axis=0) * gkt
        seq = knt_ref.shape[2]
        for j in range(tm // seq):
            knt_ref[j] = knt[:, j * seq:(j + 1) * seq]
            kt_ref[j] = knt[:, j * seq:(j + 1) * seq].astype(BF16)
        vn_ref[...] = v
    q_ref[...] = (q * (math.log2(math.e) / math.sqrt(QK_DIM))).astype(BF16)
    if latent:
        k_ref[...] = k_att.astype(BF16)
    v_ref[...] = v.astype(BF16)


def _attn_kernel(*refs, latent, lam_init, tq):
    if latent:
        lamv_ref, q_ref, k_ref, v_ref, ck_ref, cv_ref, sg_ref, o_ref = refs
    else:
        lamv_ref, q_ref, k_ref, v_ref, sg_ref, o_ref = refs
    lv = lamv_ref[...]
    lam = (jnp.exp(jnp.sum(lv[0:1] * lv[1:2], axis=1, keepdims=True))
           - jnp.exp(jnp.sum(lv[2:3] * lv[3:4], axis=1, keepdims=True)) + lam_init)
    lane = lax.broadcasted_iota(jnp.int32, (1, V_DIM), 1)
    n_q = q_ref.shape[1]
    units = [(hh, c) for hh in range(q_ref.shape[2] // V_DIM) for c in range(n_q // tq)]

    def kv(hh):
        hs = slice(hh * V_DIM, (hh + 1) * V_DIM)
        vals = [v_ref[0, :, hs]]
        if latent:
            keys = [(k_ref[0, :, hs], False), (ck_ref[0, hs, :].astype(BF16), True)]
            g = pl.program_id(1) * (q_ref.shape[2] // V_DIM) + hh
            vals.append(cv_ref[0, pl.ds(g, cv_ref.shape[1] // N_HEADS, stride=N_HEADS), :].astype(BF16))
        else:
            keys = [(k_ref[0, hs, :], True)]
        return keys, vals

    def scores(u):
        hh, c = units[u]
        q = q_ref[0, c * tq:(c + 1) * tq, hh * V_DIM:(hh + 1) * V_DIM]
        keys, _ = kv(hh)
        halves = (jnp.where(lane < QK_DIM, q, jnp.zeros_like(q)), jnp.where(lane >= QK_DIM, q, jnp.zeros_like(q)))
        return [[_dot(qh, kk) if feature_major else _dot_nt(qh, kk) for kk, feature_major in keys] for qh in halves]

    ones_col = jnp.where(lane == 0, 1.0, 0.0).astype(BF16)

    def softmax_pv(s, vals):
        m = s[0].max(axis=1, keepdims=True)
        for si in s[1:]:
            m = jnp.maximum(m, si.max(axis=1, keepdims=True))
        acc = None
        for si, vv in zip(s, vals):
            v_aug = jnp.concatenate([vv, jnp.broadcast_to(ones_col, vv.shape)], axis=1)
            part = _dot(jnp.exp2(si - m).astype(BF16), v_aug)
            acc = part if acc is None else acc + part
        return acc[:, :V_DIM] * (1.0 / acc[:, V_DIM:V_DIM + 1])

    def finish(u, s):
        hh, c = units[u]
        _, vals = kv(hh)
        o = softmax_pv(s[0], vals) - lam * softmax_pv(s[1], vals)
        o_ref[0, c * tq:(c + 1) * tq, hh * V_DIM:(hh + 1) * V_DIM] = (
            _rms(o) * sg_ref[...] * (1.0 - lam_init)).astype(BF16)

    ahead = ATTN_SCORES_AHEAD
    pending = [scores(u) for u in range(min(ahead, len(units)))]
    for u in range(len(units)):
        if u + ahead < len(units):
            pending.append(scores(u + ahead))
        finish(u, pending.pop(0))


def _merge_kernel(x_ref, mod_ref, g1_ref, yc_ref, oa_ref, wg_ref, wco_ref, wao_ref, wo_ref, g2_ref,
                  wrh_ref, wrl_ref, br_ref, tri_ref, x1_ref, h2_ref, route_ref, meta_ref, cnt_ref):
    x = x_ref[...]
    hb = _norm_mod(x, g1_ref[...], mod_ref[0, 1:2, :], mod_ref[0, 0:1, :]).astype(BF16)
    y_conv = _dot(yc_ref[...], wco_ref[...])
    y_attn = _dot(oa_ref[...], wao_ref[...])
    mixed = (_sigmoid(_dot(hb, wg_ref[:, 0:D_MODEL])) * y_conv
             + _sigmoid(_dot(hb, wg_ref[:, D_MODEL:2 * D_MODEL])) * y_attn)
    x1 = x + mod_ref[0, 2:3, :] * _dot(mixed.astype(BF16), wo_ref[...])
    x1_ref[...] = x1
    h2 = _norm_mod(x1, g2_ref[...], mod_ref[0, 4:5, :], mod_ref[0, 3:4, :])
    h2_ref[...] = h2

    h_hi, h_lo = _split_bf16(h2)
    logits = _dot(h_hi, wrh_ref[...]) + _dot(h_lo, wrh_ref[...]) + _dot(h_hi, wrl_ref[...]) + br_ref[...]
    tm = x.shape[0]
    lane = lax.broadcasted_iota(jnp.int32, (tm, LANES), 1).astype(F32)
    ninf = -jnp.inf
    big = float(LANES)

    def top(vals):
        vmax = vals.max(axis=1, keepdims=True)
        idx = jnp.where(vals == vmax, lane, big).min(axis=1, keepdims=True)
        return vmax, idx

    is_grp = lane < N_GROUPS
    gmax, gidx = top(jnp.where(is_grp, logits, ninf))
    g_p = 1.0 / jnp.where(is_grp, jnp.exp(logits - gmax), 0.0).sum(axis=1, keepdims=True)
    lo = ROUTER_LANE0 + EXPERTS_PER_GROUP * gidx
    el = jnp.where((lane >= lo) & (lane < lo + EXPERTS_PER_GROUP), logits, ninf)
    v1, i1 = top(el)
    v2, i2 = top(jnp.where(lane == i1, ninf, el))
    e2 = jnp.exp(v2 - v1)
    den = 1.0 / (1.0 + e2)
    e0 = i1 - ROUTER_LANE0
    e1 = i2 - ROUTER_LANE0
    onehot = jnp.where((lane == e0) | (lane == e1), 1.0, 0.0)
    before = _dot(tri_ref[...], onehot.astype(BF16))
    r0 = jnp.where(lane == e0, before, 0.0).sum(axis=1, keepdims=True)
    r1 = jnp.where(lane == e1, before, 0.0).sum(axis=1, keepdims=True)
    route = (jnp.where(lane == 0.0, e0, 0.0) + jnp.where(lane == 1.0, e1, 0.0)
             + jnp.where(lane == 2.0, den * g_p, 0.0) + jnp.where(lane == 3.0, e2 * den * g_p, 0.0)
             + jnp.where(lane == 4.0, r0, 0.0) + jnp.where(lane == 5.0, r1, 0.0))
    route_ref[...] = route
    meta_ref[...] = jnp.transpose(route)[:META_ROWS, :]
    cnt_ref[0] = jnp.broadcast_to(onehot.sum(axis=0, keepdims=True), (META_ROWS, LANES))


def _row_copy(src, dst, sem):
    return pltpu.make_async_copy(src, dst, sem)


def _for_groups(n_groups, body):
    if n_groups == 1:
        body(0, 0)
    else:
        lax.fori_loop(0, n_groups, body, 0)


def _dispatch_kernel(pos_ref, last_ref, nu_ref, h2p_ref, h2s_ref, xs_ref, zero_buf, sem, *, n_p, n_tok, n_tiles):
    i = pl.program_id(0)
    tm = h2p_ref.shape[0]

    @pl.when(i == 0)
    def _():
        zero_buf[...] = jnp.zeros_like(zero_buf)

        def zero_tile(row):
            return _row_copy(zero_buf, xs_ref.at[pl.ds(pl.multiple_of(row, MOE_ROWS), MOE_ROWS)], sem)

        for e in range(N_EXPERTS):
            @pl.when(last_ref[e] >= 0)
            def _():
                zero_tile(last_ref[e]).start()

        def start_tail(j, carry):
            zero_tile(j * MOE_ROWS).start()
            return carry

        lax.fori_loop(nu_ref[0], n_tiles, start_tail, 0)
        for e in range(N_EXPERTS):
            @pl.when(last_ref[e] >= 0)
            def _():
                zero_tile(0).wait()

        def wait_tail(j, carry):
            zero_tile(0).wait()
            return carry

        lax.fori_loop(nu_ref[0], n_tiles, wait_tail, 0)

    base = i * tm

    def scatter(h2_ref):
        def issue(g, carry):
            for u in range(DMA_UNROLL):
                r = g * DMA_UNROLL + u
                for slot in range(2):
                    p = pos_ref[slot * n_tok + base + r]
                    _row_copy(h2_ref.at[pl.ds(r, 1)], xs_ref.at[pl.ds(p, 1)], sem).start(priority=slot)
            return carry

        _for_groups(tm // DMA_UNROLL, issue)
        for slot in range(2):
            _row_copy(h2_ref, xs_ref.at[pl.ds(0, tm)], sem).wait()

    @pl.when(i < n_p)
    def _():
        scatter(h2p_ref)

    @pl.when(i >= n_p)
    def _():
        scatter(h2s_ref)


def _expert_kernel(te_ref, nu_ref, x_ref, w1_ref, w3_ref, w2_ref, y_ref, w13b, w2b):
    j = pl.program_id(0)

    @pl.when((j == 0) | (te_ref[j] != te_ref[jnp.maximum(j - 1, 0)]))
    def _():
        w13b[:, 0:D_EXPERT] = w1_ref[0].astype(BF16)
        w13b[:, D_EXPERT:2 * D_EXPERT] = w3_ref[0].astype(BF16)
        w2b[...] = w2_ref[0].astype(BF16)

    @pl.when(j < nu_ref[0])
    def _():
        t = x_ref[...].astype(BF16)
        ab = _dot(t, w13b[...])
        a = ab[:, 0:D_EXPERT]
        he = (a * _sigmoid(a)) * ab[:, D_EXPERT:2 * D_EXPERT]
        y_ref[...] = _dot(he.astype(BF16), w2b[...])

    @pl.when(j >= nu_ref[0])
    def _():
        y_ref[...] = jnp.zeros_like(y_ref)


def _combine_kernel(pos_ref, x1p_ref, x1s_ref, rp_ref, rs_ref, mod_ref, ys_ref, op_ref, os_ref, ybuf, sem,
                    *, n_p, n_tok, n_steps):
    i = pl.program_id(0)
    tm = ybuf.shape[2]

    def gather(tile, buf):
        base = tile * tm
        for r in range(tm):
            for slot in range(2):
                p = pos_ref[slot * n_tok + base + r]
                _row_copy(ys_ref.at[pl.ds(p, 1)], ybuf.at[buf, slot, pl.ds(r, 1)], sem.at[buf]).start(priority=slot)

    @pl.when(i == 0)
    def _():
        gather(0, 0)

    cur = i % 2

    @pl.when(i + 1 < n_steps)
    def _():
        gather(i + 1, 1 - cur)

    for slot in range(2):
        _row_copy(ybuf.at[cur, slot], ybuf.at[cur, slot], sem.at[cur]).wait()

    def out(x1_ref, r_ref, o_ref):
        r = r_ref[...]
        o_ref[...] = x1_ref[...] + mod_ref[0, 5:6, :] * (r[:, 2:3] * ybuf[cur, 0] + r[:, 3:4] * ybuf[cur, 1])

    @pl.when(i < n_p)
    def _():
        out(x1p_ref, rp_ref, op_ref)

    @pl.when(i >= n_p)
    def _():
        out(x1s_ref, rs_ref, os_ref)


def _layer(x3, mod, row0, latent, cache, wts, lam_init):
    b, l, d = x3.shape
    t = b * l
    x = x3.reshape(t, d)
    tm = MOE_TOKEN_TILE
    nt = t // tm
    per_seq = l // tm if l >= tm else 1
    if latent:
        mod_row = lambda i: row0 + i // per_seq
    else:
        mod_row = lambda i: row0

    x_spec = pl.BlockSpec((tm, d), lambda i: (i, 0))
    mod_spec = pl.BlockSpec((1, 6, d), lambda i: (mod_row(i), 0, 0))
    row_spec = pl.BlockSpec((1, d), lambda i: (0, 0))
    act_spec = pl.BlockSpec((tm, d), lambda i: (i, 0))

    def full(a):
        nd = a.ndim
        return pl.BlockSpec(a.shape, lambda i: (0,) * nd)

    def w_in_cols(width, idx):
        return pl.BlockSpec((d, width), lambda i: (0, idx))

    tmc = max(CONV_TOKEN_TILE, l)
    assert tmc % l == 0 and t % tmc == 0
    seq_per_tile = tmc // l
    xc_spec = pl.BlockSpec((tmc, d), lambda i: (i, 0))
    modc_spec = pl.BlockSpec((1, 6, d), lambda i: (row0 + (i * seq_per_tile if latent else 0), 0, 0))
    assert not latent or seq_per_tile == 1
    yc = pl.pallas_call(
        functools.partial(_conv_kernel, seq_len=l),
        grid=(t // tmc,),
        in_specs=[xc_spec, modc_spec, row_spec, w_in_cols(3 * d, 0), full(wts["conv_w"])],
        out_specs=xc_spec,
        out_shape=jax.ShapeDtypeStruct((t, d), BF16),
        compiler_params=_params("parallel"),
        name="conv_in",
    )(x, mod, wts["norm1_g"], wts["w_in"], wts["conv_w"])

    if latent:
        qkv_in = [x, mod, wts["norm1_g"], wts["w_in"], wts["bd"], wts["gq"], wts["gk"], wts["cos"], wts["sin"]]
        tab_spec = pl.BlockSpec((tm, LANES), lambda i: (i % per_seq, 0))
        qkv_specs = [x_spec, mod_spec, row_spec, w_in_cols(3 * d, 1), full(wts["bd"]), row_spec, row_spec,
                     tab_spec, tab_spec]
        out_shapes = [jax.ShapeDtypeStruct((t, d), BF16)] * 3
        out_specs = [act_spec] * 3
    else:
        assert tm % l == 0
        kt_spec = pl.BlockSpec((tm // l, d, l), lambda i: (i, 0, 0))
        qkv_in = [x, mod, wts["norm1_g"], wts["w_in"], wts["w_kt"], wts["bd"], wts["gq"], wts["gkt"]]
        qkv_specs = [x_spec, mod_spec, row_spec, w_in_cols(3 * d, 1), full(wts["w_kt"]), full(wts["bd"]), row_spec,
                     full(wts["gkt"])]
        out_shapes = [jax.ShapeDtypeStruct((t, d), BF16), jax.ShapeDtypeStruct((b, d, l), BF16),
                      jax.ShapeDtypeStruct((t, d), BF16), jax.ShapeDtypeStruct((b, d, l), F32),
                      jax.ShapeDtypeStruct((t, d), F32)]
        out_specs = [act_spec, kt_spec, act_spec, kt_spec, act_spec]
    qkv_out = pl.pallas_call(
        functools.partial(_qkv_kernel, latent=latent),
        grid=(nt,),
        in_specs=qkv_specs,
        out_specs=out_specs,
        out_shape=out_shapes,
        compiler_params=_params("parallel"),
        name="qkv",
    )(*qkv_in)
    q, k, v = qkv_out[:3]

    tq = min(ATTN_Q_ROWS, l)
    heads_per_step = max(1, ATTN_UNITS_PER_STEP // (l // tq))
    hw = heads_per_step * V_DIM
    q3, v3 = q.reshape(b, l, d), v.reshape(b, l, d)
    qkv_spec = pl.BlockSpec((1, l, hw), lambda bi, h: (bi, 0, h))
    kt_blk = lambda n_tok: pl.BlockSpec((1, hw, n_tok), lambda bi, h: (bi, h, 0))
    sg_spec = pl.BlockSpec((1, V_DIM), lambda bi, h: (0, 0))
    lam_spec = pl.BlockSpec(wts["lamv"].shape, lambda bi, h: (0, 0))
    if latent:
        ckt, cv = cache
        attn_in = [wts["lamv"], q3, k.reshape(b, l, d), v3, ckt, cv]
        attn_specs = [lam_spec, qkv_spec, qkv_spec, qkv_spec, kt_blk(ckt.shape[2]),
                      pl.BlockSpec((1, cv.shape[1], V_DIM), lambda bi, h: (bi, 0, 0))]
    else:
        attn_in = [wts["lamv"], q3, k, v3]
        attn_specs = [lam_spec, qkv_spec, kt_blk(l), qkv_spec]
    attn_in.append(wts["subln_g"])
    attn_specs.append(sg_spec)
    oa = pl.pallas_call(
        functools.partial(_attn_kernel, latent=latent, lam_init=lam_init, tq=tq),
        grid=(b, N_HEADS // heads_per_step),
        in_specs=attn_specs,
        out_specs=qkv_spec,
        out_shape=jax.ShapeDtypeStruct((b, l, d), BF16),
        compiler_params=_params("parallel", "parallel"),
        name="attn",
    )(*attn_in).reshape(t, d)

    lane_spec = pl.BlockSpec((tm, LANES), lambda i: (i, 0))
    x1, h2, route, meta, counts = pl.pallas_call(
        _merge_kernel,
        grid=(nt,),
        in_specs=[x_spec, mod_spec, row_spec, act_spec, act_spec, w_in_cols(2 * d, 3), full(wts["w_conv_out"]),
                  full(wts["w_attn_out"]), full(wts["w_o"]), row_spec, full(wts["wr_hi"]), full(wts["wr_lo"]),
                  full(wts["b_router"]), full(wts["tri"])],
        out_specs=[x_spec, x_spec, lane_spec, pl.BlockSpec((META_ROWS, tm), lambda i: (0, i)),
                   pl.BlockSpec((1, META_ROWS, LANES), lambda i: (i, 0, 0))],
        out_shape=[jax.ShapeDtypeStruct((t, d), F32), jax.ShapeDtypeStruct((t, d), F32),
                   jax.ShapeDtypeStruct((t, LANES), F32), jax.ShapeDtypeStruct((META_ROWS, t), F32),
                   jax.ShapeDtypeStruct((nt, META_ROWS, LANES), F32)],
        compiler_params=_params("parallel"),
        name="merge",
    )(x, mod, wts["norm1_g"], yc, oa, wts["w_in"], wts["w_conv_out"], wts["w_attn_out"], wts["w_o"],
      wts["norm2_g"], wts["wr_hi"], wts["wr_lo"], wts["b_router"], wts["tri"])

    outs = [x1, h2, route, meta, counts[:, 0, :N_EXPERTS]]
    if not latent:
        outs += [qkv_out[3], qkv_out[4]]
    return outs


def _routing_tables(meta, counts, tm, n_tiles):
    eid = meta[0:2].astype(jnp.int32)
    rank = meta[4:6].astype(jnp.int32)
    counts = counts.astype(jnp.int32)
    totals = counts.sum(axis=0)
    tiles_e = (totals + MOE_ROWS - 1) // MOE_ROWS
    tile_end = jnp.cumsum(tiles_e)
    row_start = (tile_end - tiles_e) * MOE_ROWS
    base = row_start[None, :] + jnp.cumsum(counts, axis=0) - counts
    base_tok = jnp.repeat(base.T, tm, axis=1)
    ex = jnp.arange(N_EXPERTS, dtype=jnp.int32)[None, :, None]
    pos = jnp.sum(jnp.where(eid[:, None, :] == ex, base_tok[None], 0), axis=1) + rank
    n_used = tile_end[-1]
    tile_ids = jnp.minimum(jnp.arange(n_tiles, dtype=jnp.int32), n_used - 1)
    tile_expert = jnp.sum((tile_ids[:, None] >= tile_end[None, :]).astype(jnp.int32), axis=1)
    last_tile_row = jnp.where(tiles_e > 0, (tile_end - 1) * MOE_ROWS, -1)
    return (pos.reshape(-1), tile_expert.astype(jnp.int32), n_used.astype(jnp.int32).reshape(1),
            last_tile_row.astype(jnp.int32))


def _routed_moe(streams, mod, w1, w3, w2, per_seq_tiles, tm):
    (x1p, h2p, rp, _, _), (x1s, h2s, rs, _, _) = streams
    d = x1p.shape[1]
    tp, ts = x1p.shape[0], x1s.shape[0]
    t = tp + ts
    n_tiles = 2 * t // MOE_ROWS + N_EXPERTS
    meta = jnp.concatenate([s[3] for s in streams], axis=1)
    counts = jnp.concatenate([s[4] for s in streams], axis=0)
    pos, tile_expert, n_used, last_tile_row = _routing_tables(meta, counts, tm, n_tiles)

    sorted_shape = jax.ShapeDtypeStruct((n_tiles * MOE_ROWS, d), F32)
    any_spec = pl.BlockSpec(memory_space=pl.ANY)
    n_p = tp // tm
    x_sorted = pl.pallas_call(
        functools.partial(_dispatch_kernel, n_p=n_p, n_tok=t, n_tiles=n_tiles),
        grid_spec=pltpu.PrefetchScalarGridSpec(
            num_scalar_prefetch=3, grid=(t // tm,),
            in_specs=[pl.BlockSpec((tm, d), lambda i, *_: (jnp.minimum(i, n_p - 1), 0)),
                      pl.BlockSpec((tm, d), lambda i, *_: (jnp.maximum(i - n_p, 0), 0))],
            out_specs=any_spec,
            scratch_shapes=[pltpu.VMEM((MOE_ROWS, d), F32), pltpu.SemaphoreType.DMA]),
        out_shape=sorted_shape,
        compiler_params=_params("arbitrary"),
        name="dispatch",
    )(pos, last_tile_row, n_used, h2p, h2s)

    def row_blk(j, te, nu):
        return (jnp.minimum(j, nu[0] - 1), 0)

    y_sorted = pl.pallas_call(
        _expert_kernel,
        grid_spec=pltpu.PrefetchScalarGridSpec(
            num_scalar_prefetch=2,
            grid=(n_tiles,),
            in_specs=[pl.BlockSpec((MOE_ROWS, d), row_blk),
                      pl.BlockSpec((1, d, D_EXPERT), lambda j, te, nu: (te[j], 0, 0)),
                      pl.BlockSpec((1, d, D_EXPERT), lambda j, te, nu: (te[j], 0, 0)),
                      pl.BlockSpec((1, D_EXPERT, d), lambda j, te, nu: (te[j], 0, 0))],
            out_specs=pl.BlockSpec((MOE_ROWS, d), lambda j, te, nu: (j, 0)),
            scratch_shapes=[pltpu.VMEM((d, 2 * D_EXPERT), BF16),
                            pltpu.VMEM((D_EXPERT, d), BF16)]),
        out_shape=jax.ShapeDtypeStruct((n_tiles * MOE_ROWS, d), F32),
        compiler_params=_params("arbitrary"),
        name="experts",
    )(tile_expert, n_used, x_sorted, w1, w3, w2)

    ctx_blk = lambda i, p: (jnp.minimum(i, n_p - 1), 0)
    lat_blk = lambda i, p: (jnp.maximum(i - n_p, 0), 0)
    return pl.pallas_call(
        functools.partial(_combine_kernel, n_p=n_p, n_tok=t, n_steps=t // tm),
        grid_spec=pltpu.PrefetchScalarGridSpec(
            num_scalar_prefetch=1, grid=(t // tm,),
            in_specs=[pl.BlockSpec((tm, d), ctx_blk), pl.BlockSpec((tm, d), lat_blk),
                      pl.BlockSpec((tm, LANES), ctx_blk), pl.BlockSpec((tm, LANES), lat_blk),
                      pl.BlockSpec((1, 6, d),
                                   lambda i, p: (jnp.where(i < n_p, 0, 1 + (i - n_p) // per_seq_tiles), 0, 0)),
                      any_spec],
            out_specs=[pl.BlockSpec((tm, d), ctx_blk), pl.BlockSpec((tm, d), lat_blk)],
            scratch_shapes=[pltpu.VMEM((2, 2, tm, d), F32), pltpu.SemaphoreType.DMA((2,))]),
        out_shape=[jax.ShapeDtypeStruct((tp, d), F32), jax.ShapeDtypeStruct((ts, d), F32)],
        compiler_params=_params("arbitrary"),
        name="combine",
    )(pos, x1p, x1s, rp, rs, mod, y_sorted)


def _rope_tables(n_tok):
    rows = n_tok // GRID_W
    pos_row = jnp.repeat(jnp.arange(rows, dtype=F32), GRID_W)
    pos_col = jnp.tile(jnp.arange(GRID_W, dtype=F32), rows)
    inv_freq = 1.0 / (ROPE_BASE ** (jnp.arange(0, AXIS_DIM, 2, dtype=F32) / AXIS_DIM))
    ang = jnp.concatenate([pos_row[:, None] * inv_freq[None, :]] * 2
                          + [pos_col[:, None] * inv_freq[None, :]] * 2, axis=1)
    sign = jnp.tile(jnp.concatenate([-jnp.ones((AXIS_DIM // 2,), F32), jnp.ones((AXIS_DIM // 2,), F32)]), 2)
    cos = jnp.cos(ang)
    sin = jnp.sin(ang) * sign[None, :]
    return jnp.tile(cos, (1, LANES // QK_DIM)), jnp.tile(sin, (1, LANES // QK_DIM))


def kernel(x_prompt, x_sample, cache_k, cache_v, c, c_ctx, w_ada, b_ada, norm1_g, w_in, conv_w, w_conv_out,
           q_norm_g, k_norm_g, lambda_q1, lambda_k1, lambda_q2, lambda_k2, subln_g, w_attn_out, w_o, norm2_g,
           w_grp, b_grp, w_exp, b_exp, w1, w3, w2):
    depth = w_in.shape[0]
    assert depth == 1
    d = D_MODEL
    dec_b, dec_l = x_sample.shape[0], x_sample.shape[1]
    cos, sin = _rope_tables(dec_l)
    bd = (jnp.arange(MXU_DIM)[:, None] // QK_DIM == jnp.arange(MXU_DIM)[None, :] // QK_DIM).astype(BF16)
    tri = (jnp.arange(MOE_TOKEN_TILE)[:, None] > jnp.arange(MOE_TOKEN_TILE)[None, :]).astype(BF16)
    xp, xs = x_prompt, x_sample
    new_k, new_v = [], []
    for l in range(depth):
        lam_init = 0.8 - 0.6 * math.exp(-0.3 * l)
        mod_rows = 16
        cvec = jnp.concatenate([c_ctx[None, :], c, jnp.zeros((mod_rows - 1 - dec_b, d), F32)], axis=0)
        mod = _ada(cvec, w_ada[l], b_ada[l][None, :]).reshape(mod_rows, 6, d)

        wi = w_in[l].astype(BF16)
        w_router = jnp.concatenate([w_grp[l], w_exp[l]], axis=1)
        w_router = jnp.pad(w_router, ((0, 0), (0, LANES - w_router.shape[1])))
        wr_hi = w_router.astype(BF16)
        wr_lo = (w_router - wr_hi.astype(F32)).astype(BF16)
        b_router = jnp.pad(jnp.concatenate([b_grp[l], b_exp[l]]), (0, LANES - N_GROUPS - N_EXPERTS))[None, :]
        wts = dict(
            norm1_g=norm1_g[l][None, :], norm2_g=norm2_g[l][None, :],
            w_in=wi,
            conv_w=conv_w[l], w_conv_out=w_conv_out[l].astype(BF16),
            gq=jnp.tile(q_norm_g[l], d // QK_DIM)[None, :], gk=jnp.tile(k_norm_g[l], d // QK_DIM)[None, :],
            w_kt=wi[:, 4 * d:5 * d].T, gkt=jnp.broadcast_to(jnp.tile(k_norm_g[l], d // QK_DIM)[:, None], (d, LANES)),
            bd=bd, cos=cos, sin=sin,
            lamv=jnp.stack([lambda_q1[l], lambda_k1[l], lambda_q2[l], lambda_k2[l]]),
            subln_g=subln_g[l][None, :], w_attn_out=w_attn_out[l].astype(BF16), w_o=w_o[l].astype(BF16),
            wr_hi=wr_hi, wr_lo=wr_lo, b_router=b_router, tri=tri,
        )
        *ctx, k_ctx, v_ctx = _layer(xp, mod, 0, False, None, wts, lam_init)
        new_k.append(k_ctx.reshape(x_prompt.shape[0], N_HEADS, 2, QK_DIM, x_prompt.shape[1]).transpose(0, 4, 1, 2, 3))
        new_v.append(v_ctx.reshape(x_prompt.shape[0], x_prompt.shape[1], N_HEADS, V_DIM))
        past = cache_k.shape[2]
        cache = (cache_k[:, l].transpose(0, 2, 3, 4, 1).reshape(dec_b, d, past), cache_v[:, l].reshape(dec_b, past * N_HEADS, V_DIM))
        lat = _layer(xs, mod, 1, True, cache, wts, lam_init)
        tm = MOE_TOKEN_TILE
        yp, ys = _routed_moe((ctx, lat), mod, w1[l], w3[l], w2[l], dec_l // tm, tm)
        xp, xs = yp.reshape(x_prompt.shape), ys.reshape(x_sample.shape)
    return (xp, xs, jnp.stack(new_k, axis=1), jnp.stack(new_v, axis=1))
```
